```python
import jax, jax.numpy as jnp
from jax import lax
import numpy as np

D_MODEL = 1024
BATCH = 8
SEQ = 4096
DEPTH = 1

HG_HEADS = 4
HG_DK = 128
HG_DV = 128
HG_WIDTH = HG_HEADS * HG_DK
HG_CHUNK = 64

ATT_HEADS = 8
ATT_GROUPS = 2
ATT_HEAD_DIM = 64
ATT_REP = ATT_HEADS // ATT_GROUPS
ATT_WIDTH = ATT_HEADS * ATT_HEAD_DIM
ATT_KV_WIDTH = ATT_GROUPS * ATT_HEAD_DIM
IDX_HEADS = 8
IDX_DIM = 64
IDX_TOPK_MAX = 256
Q_BLOCK = 128

PEER_HEADS = 8
PEER_NKEYS = 128
PEER_NEXP = PEER_NKEYS * PEER_NKEYS
PEER_DKEY = 256
PEER_TOPK = 16
PEER_TOK_BLOCK = 128

RMS_EPS = 1e-6

IN_SPLIT = (HG_WIDTH, HG_WIDTH, HG_WIDTH, HG_WIDTH,
            ATT_WIDTH, ATT_KV_WIDTH, ATT_KV_WIDTH,
            IDX_HEADS * IDX_DIM, IDX_DIM, IDX_HEADS,
            D_MODEL, D_MODEL)
IN_WIDTH = 4 * HG_WIDTH + ATT_WIDTH + 2 * ATT_KV_WIDTH + IDX_HEADS * IDX_DIM + IDX_DIM + IDX_HEADS + 2 * D_MODEL

kernel_name = 'hybrid_hgrn2_dsa_peer_block'


def rmsnorm(x, g):
    xf = x.astype(jnp.float32)
    y = xf * lax.rsqrt(jnp.mean(xf * xf, axis=-1, keepdims=True) + RMS_EPS)
    return (y * g.astype(jnp.float32)).astype(x.dtype)


def alibi_slopes(n_heads):
    return jnp.exp2(-8.0 * (jnp.arange(n_heads, dtype=jnp.float32) + 1.0) / n_heads)


def hgrn2_mix(q, f_pre, i, g, lb, onorm_g):
    B, S, _ = q.shape
    C = HG_CHUNK
    N = S // C
    dt = q.dtype

    def heads(t):
        return t.astype(jnp.float32).reshape(B, N, C, HG_HEADS, -1).transpose(0, 3, 1, 2, 4)

    lbf = lb.astype(jnp.float32).reshape(HG_HEADS, 1, 1, HG_DK)
    qh = jax.nn.silu(heads(q))
    f = lbf + (1.0 - lbf) * jax.nn.sigmoid(heads(f_pre))
    kh = 1.0 - f
    vh = heads(i)
    b = jnp.cumsum(jnp.log(f), axis=3)
    b_mid = b[:, :, :, C // 2 - 1:C // 2]
    b_last = b[:, :, :, -1:]
    A = jnp.einsum('bhncd,bhnsd->bhncs', qh * jnp.exp(b - b_mid), kh * jnp.exp(b_mid - b))
    causal = jnp.tril(jnp.ones((C, C), dtype=bool))
    A = jnp.where(causal, A, 0.0)
    o = jnp.einsum('bhncs,bhnse->bhnce', A, vh)
    dS = jnp.einsum('bhnsd,bhnse->bhnde', kh * jnp.exp(b_last - b), vh)
    decay = jnp.exp(b_last[:, :, :, 0])

    def step(S_prev, inp):
        dec, ds = inp
        return dec[..., None] * S_prev + ds, S_prev

    S0 = jnp.zeros((B, HG_HEADS, HG_DK, HG_DV), jnp.float32)
    _, S_before = lax.scan(step, S0, (jnp.moveaxis(decay, 2, 0), jnp.moveaxis(dS, 2, 0)))
    S_before = jnp.moveaxis(S_before, 0, 2)
    o = o + jnp.einsum('bhncd,bhnde->bhnce', qh * jnp.exp(b), S_before)
    o = rmsnorm(o, onorm_g) * jax.nn.silu(heads(g))
    return o.transpose(0, 2, 3, 1, 4).reshape(B, S, HG_WIDTH).astype(dt)


def dsa_mix(q, k, v, q_idx, k_idx, w_idx):
    B, S = q.shape[0], q.shape[1]
    topk = min(IDX_TOPK_MAX, S // 4)
    nb = S // Q_BLOCK
    slopes = alibi_slopes(ATT_HEADS).reshape(ATT_GROUPS, ATT_REP)
    kpos = jnp.arange(S)
    idx_scale = (IDX_DIM ** -0.5) * (IDX_HEADS ** -0.5)
    att_scale = ATT_HEAD_DIM ** -0.5

    def to_blocks(t):
        return jnp.moveaxis(t.reshape(B, nb, Q_BLOCK, *t.shape[2:]), 1, 0)

    def block(args):
        qb, qib, wb, t0 = args
        tpos = t0 + jnp.arange(Q_BLOCK)
        s = jnp.einsum('bqhd,bsd->bqhs', qib, k_idx)
        I = jnp.einsum('bqh,bqhs->bqs', wb, jax.nn.relu(s)).astype(jnp.float32) * idx_scale
        causal = kpos[None, :] <= tpos[:, None]
        I = jnp.where(causal[None], I, -jnp.inf)
        _, sel = lax.top_k(I, topk)
        valid = sel <= tpos[None, :, None]
        kg = jax.vmap(lambda kk, ii: kk[ii])(k, sel)
        vg = jax.vmap(lambda vv, ii: vv[ii])(v, sel)
        qg = qb.reshape(B, Q_BLOCK, ATT_GROUPS, ATT_REP, ATT_HEAD_DIM)
        sc = jnp.einsum('bqgrd,bqkgd->bqgrk', qg, kg).astype(jnp.float32) * att_scale
        dist = (tpos[None, :, None] - sel).astype(jnp.float32)
        sc = sc - slopes[None, None, :, :, None] * dist[:, :, None, None, :]
        sc = jnp.where(valid[:, :, None, None, :], sc, -jnp.inf)
        p = jax.nn.softmax(sc, axis=-1).astype(vg.dtype)
        o = jnp.einsum('bqgrk,bqkgd->bqgrd', p, vg)
        return o.reshape(B, Q_BLOCK, ATT_WIDTH)

    out = lax.map(block, (to_blocks(q), to_blocks(q_idx), to_blocks(w_idx),
                          jnp.arange(nb, dtype=jnp.int32) * Q_BLOCK))
    return jnp.moveaxis(out, 0, 1).reshape(B, S, ATT_WIDTH)


def peer_ffn(h, wq, subkeys, u, vtab):
    B, S, D = h.shape
    T = B * S
    hf = h.reshape(T, D)
    qry = (hf @ wq).reshape(T, PEER_HEADS, 2, PEER_DKEY // 2)
    sc = jnp.einsum('thpd,hpnd->thpn', qry, subkeys).astype(jnp.float32)
    s1, i1 = lax.top_k(sc[:, :, 0], PEER_TOPK)
    s2, i2 = lax.top_k(sc[:, :, 1], PEER_TOPK)
    cand = (s1[..., :, None] + s2[..., None, :]).reshape(T, PEER_HEADS, PEER_TOPK * PEER_TOPK)
    cidx = (i1[..., :, None] * PEER_NKEYS + i2[..., None, :]).reshape(T, PEER_HEADS, PEER_TOPK * PEER_TOPK)
    top_s, pos = lax.top_k(cand, PEER_TOPK)
    eidx = jnp.take_along_axis(cidx, pos, axis=-1)
    gates = jax.nn.softmax(top_s, axis=-1).astype(h.dtype)
    nb = T // PEER_TOK_BLOCK

    def block(args):
        xb, eb, gb = args
        a = jax.nn.gelu(jnp.einsum('td,thkd->thk', xb, u[eb]), approximate=False)
        return jnp.einsum('thk,thkd->td', gb * a, vtab[eb])

    out = lax.map(block, (hf.reshape(nb, PEER_TOK_BLOCK, D),
                          eidx.reshape(nb, PEER_TOK_BLOCK, PEER_HEADS, PEER_TOPK),
                          gates.reshape(nb, PEER_TOK_BLOCK, PEER_HEADS, PEER_TOPK)))
    return out.reshape(B, S, D)


def setup_inputs(seed: int = 0) -> dict:
    key = jax.random.key(seed)
    ks = jax.random.split(key, 18)
    n = jax.random.normal
    L, D = DEPTH, D_MODEL
    f32 = jnp.float32
    return {
        'x': n(ks[0], (BATCH, SEQ, D), f32),
        'c': n(ks[1], (BATCH, D), f32),
        'w_ada': n(ks[2], (L, D, 6 * D), f32) * (0.5 * D ** -0.5),
        'b_ada': n(ks[3], (L, 6 * D), f32) * 0.01,
        'norm1_g': 1.0 + 0.02 * n(ks[4], (L, D), f32),
        'norm2_g': 1.0 + 0.02 * n(ks[5], (L, D), f32),
        'w_in': n(ks[6], (L, D, IN_WIDTH), f32) * D ** -0.5,
        'hg_lb': 0.1 * n(ks[7], (L + 1, HG_WIDTH), f32),
        'hg_onorm_g': 1.0 + 0.02 * n(ks[8], (L, HG_DV), f32),
        'q_norm_g': 1.0 + 0.02 * n(ks[9], (L, ATT_HEAD_DIM), f32),
        'k_norm_g': 1.0 + 0.02 * n(ks[10], (L, ATT_HEAD_DIM), f32),
        'w_a': n(ks[11], (L, HG_WIDTH, D), f32) * HG_WIDTH ** -0.5,
        'w_b': n(ks[12], (L, ATT_WIDTH, D), f32) * ATT_WIDTH ** -0.5,
        'w_out': n(ks[13], (L, D, D), f32) * D ** -0.5,
        'peer_wq': n(ks[14], (L, D, PEER_HEADS * PEER_DKEY), f32) * D ** -0.5,
        'peer_subkeys': n(ks[15], (L, PEER_HEADS, 2, PEER_NKEYS, PEER_DKEY // 2), f32) * (PEER_DKEY // 2) ** -0.5,
        'peer_u': n(ks[16], (L, PEER_NEXP, D), f32) * D ** -0.5,
        'peer_v': n(ks[17], (L, PEER_NEXP, D), f32),
    }


def reference(x, c, w_ada, b_ada, norm1_g, norm2_g, w_in, hg_lb, hg_onorm_g, q_norm_g, k_norm_g,
              w_a, w_b, w_out, peer_wq, peer_subkeys, peer_u, peer_v):
    B, S, D = x.shape
    lb_all = jnp.cumsum(jax.nn.softmax(hg_lb.astype(jnp.float32), axis=0), axis=0)
    split_at = [int(v) for v in np.cumsum(np.array(IN_SPLIT))[:-1]]
    cs = jax.nn.silu(c)
    for l in range(DEPTH):
        mod = cs @ w_ada[l] + b_ada[l]
        sh1, sc1, g1, sh2, sc2, g2 = jnp.split(mod, 6, axis=-1)
        h = rmsnorm(x, norm1_g[l]) * (1.0 + sc1[:, None]) + sh1[:, None]
        proj = h @ w_in[l]
        hq, hfp, hi, hg, aq, ak, av, iq, ik, iw, ga, gb = jnp.split(proj, split_at, axis=-1)
        ya = hgrn2_mix(hq, hfp, hi, hg, lb_all[l], hg_onorm_g[l])
        q = rmsnorm(aq.reshape(B, S, ATT_HEADS, ATT_HEAD_DIM), q_norm_g[l])
        k = rmsnorm(ak.reshape(B, S, ATT_GROUPS, ATT_HEAD_DIM), k_norm_g[l])
        v = av.reshape(B, S, ATT_GROUPS, ATT_HEAD_DIM)
        yb = dsa_mix(q, k, v, iq.reshape(B, S, IDX_HEADS, IDX_DIM), ik, iw)
        merged = jax.nn.sigmoid(ga) * (ya @ w_a[l]) + jax.nn.sigmoid(gb) * (yb @ w_b[l])
        x = x + g1[:, None] * (merged @ w_out[l])
        h2 = rmsnorm(x, norm2_g[l]) * (1.0 + sc2[:, None]) + sh2[:, None]
        x = x + g2[:, None] * peer_ffn(h2, peer_wq[l], peer_subkeys[l], peer_u[l], peer_v[l])
    return x
```

```python
import functools

import jax
import jax.numpy as jnp
import numpy as np
from jax import lax
from jax.experimental import pallas as pl
from jax.experimental.pallas import tpu as pltpu

D_MODEL = 1024
HG_HEADS = 4
HG_DK = 128
HG_DV = 128
HG_WIDTH = HG_HEADS * HG_DK
HG_CHUNK = 64
ATT_HEADS = 8
ATT_GROUPS = 2
ATT_HEAD_DIM = 64
ATT_REP = ATT_HEADS // ATT_GROUPS
ATT_WIDTH = ATT_HEADS * ATT_HEAD_DIM
ATT_KV_WIDTH = ATT_GROUPS * ATT_HEAD_DIM
IDX_HEADS = 8
IDX_DIM = 64
IDX_TOPK_MAX = 256
Q_BLOCK = 128
PEER_HEADS = 8
PEER_NKEYS = 128
PEER_DKEY = 256
PEER_TOPK = 16
PEER_TOK_BLOCK = 128
RMS_EPS = 1e-6
IN_SPLIT = (HG_WIDTH, HG_WIDTH, HG_WIDTH, HG_WIDTH,
            ATT_WIDTH, ATT_KV_WIDTH, ATT_KV_WIDTH,
            IDX_HEADS * IDX_DIM, IDX_DIM, IDX_HEADS,
            D_MODEL, D_MODEL)


def _rmsnorm(x, g):
    xf = x.astype(jnp.float32)
    y = xf * lax.rsqrt(jnp.mean(xf * xf, axis=-1, keepdims=True) + RMS_EPS)
    return (y * g.astype(jnp.float32)).astype(x.dtype)


def _alibi_slopes(n_heads):
    return jnp.exp2(-8.0 * (jnp.arange(n_heads, dtype=jnp.float32) + 1.0) / n_heads)


def _hgrn2_mix(q, f_pre, i, g, lb, onorm_g):
    B, S, _ = q.shape
    C = HG_CHUNK
    N = S // C
    dt = q.dtype

    def heads(t):
        return t.astype(jnp.float32).reshape(B, N, C, HG_HEADS, -1).transpose(0, 3, 1, 2, 4)

    lbf = lb.astype(jnp.float32).reshape(HG_HEADS, 1, 1, HG_DK)
    qh = jax.nn.silu(heads(q))
    f = lbf + (1.0 - lbf) * jax.nn.sigmoid(heads(f_pre))
    kh = 1.0 - f
    vh = heads(i)
    b = jnp.cumsum(jnp.log(f), axis=3)
    b_mid = b[:, :, :, C // 2 - 1:C // 2]
    b_last = b[:, :, :, -1:]
    A = jnp.einsum('bhncd,bhnsd->bhncs', qh * jnp.exp(b - b_mid), kh * jnp.exp(b_mid - b))
    causal = jnp.tril(jnp.ones((C, C), dtype=bool))
    A = jnp.where(causal, A, 0.0)
    o = jnp.einsum('bhncs,bhnse->bhnce', A, vh)
    dS = jnp.einsum('bhnsd,bhnse->bhnde', kh * jnp.exp(b_last - b), vh)
    decay = jnp.exp(b_last[:, :, :, 0])

    def step(S_prev, inp):
        dec, ds = inp
        return dec[..., None] * S_prev + ds, S_prev

    S0 = jnp.zeros((B, HG_HEADS, HG_DK, HG_DV), jnp.float32)
    _, S_before = lax.scan(step, S0, (jnp.moveaxis(decay, 2, 0), jnp.moveaxis(dS, 2, 0)))
    S_before = jnp.moveaxis(S_before, 0, 2)
    o = o + jnp.einsum('bhncd,bhnde->bhnce', qh * jnp.exp(b), S_before)
    o = _rmsnorm(o, onorm_g) * jax.nn.silu(heads(g))
    return o.transpose(0, 2, 3, 1, 4).reshape(B, S, HG_WIDTH).astype(dt)


def _dsa_mix(q, k, v, q_idx, k_idx, w_idx):
    B, S = q.shape[0], q.shape[1]
    topk = min(IDX_TOPK_MAX, S // 4)
    nb = S // Q_BLOCK
    slopes = _alibi_slopes(ATT_HEADS).reshape(ATT_GROUPS, ATT_REP)
    kpos = jnp.arange(S)
    idx_scale = (IDX_DIM ** -0.5) * (IDX_HEADS ** -0.5)
    att_scale = ATT_HEAD_DIM ** -0.5

    def to_blocks(t):
        return jnp.moveaxis(t.reshape(B, nb, Q_BLOCK, *t.shape[2:]), 1, 0)

    def block(args):
        qb, qib, wb, t0 = args
        tpos = t0 + jnp.arange(Q_BLOCK)
        s = jnp.einsum('bqhd,bsd->bqhs', qib, k_idx)
        I = jnp.einsum('bqh,bqhs->bqs', wb, jax.nn.relu(s)).astype(jnp.float32) * idx_scale
        causal = kpos[None, :] <= tpos[:, None]
        I = jnp.where(causal[None], I, -jnp.inf)
        _, sel = lax.top_k(I, topk)
        valid = sel <= tpos[None, :, None]
        kg = jax.vmap(lambda kk, ii: kk[ii])(k, sel)
        vg = jax.vmap(lambda vv, ii: vv[ii])(v, sel)
        qg = qb.reshape(B, Q_BLOCK, ATT_GROUPS, ATT_REP, ATT_HEAD_DIM)
        sc = jnp.einsum('bqgrd,bqkgd->bqgrk', qg, kg).astype(jnp.float32) * att_scale
        dist = (tpos[None, :, None] - sel).astype(jnp.float32)
        sc = sc - slopes[None, None, :, :, None] * dist[:, :, None, None, :]
        sc = jnp.where(valid[:, :, None, None, :], sc, -jnp.inf)
        p = jax.nn.softmax(sc, axis=-1).astype(vg.dtype)
        o = jnp.einsum('bqgrk,bqkgd->bqgrd', p, vg)
        return o.reshape(B, Q_BLOCK, ATT_WIDTH)

    out = lax.map(block, (to_blocks(q), to_blocks(q_idx), to_blocks(w_idx),
                          jnp.arange(nb, dtype=jnp.int32) * Q_BLOCK))
    return jnp.moveaxis(out, 0, 1).reshape(B, S, ATT_WIDTH)


def _peer_ffn(h, wq, subkeys, u, vtab):
    B, S, D = h.shape
    T = B * S
    hf = h.reshape(T, D)
    qry = (hf @ wq).reshape(T, PEER_HEADS, 2, PEER_DKEY // 2)
    sc = jnp.einsum('thpd,hpnd->thpn', qry, subkeys).astype(jnp.float32)
    s1, i1 = lax.top_k(sc[:, :, 0], PEER_TOPK)
    s2, i2 = lax.top_k(sc[:, :, 1], PEER_TOPK)
    cand = (s1[..., :, None] + s2[..., None, :]).reshape(T, PEER_HEADS, PEER_TOPK * PEER_TOPK)
    cidx = (i1[..., :, None] * PEER_NKEYS + i2[..., None, :]).reshape(T, PEER_HEADS, PEER_TOPK * PEER_TOPK)
    top_s, pos = lax.top_k(cand, PEER_TOPK)
    eidx = jnp.take_along_axis(cidx, pos, axis=-1)
    gates = jax.nn.softmax(top_s, axis=-1).astype(h.dtype)
    nb = T // PEER_TOK_BLOCK

    def block(args):
        xb, eb, gb = args
        a = jax.nn.gelu(jnp.einsum('td,thkd->thk', xb, u[eb]), approximate=False)
        return jnp.einsum('thk,thkd->td', gb * a, vtab[eb])

    out = lax.map(block, (hf.reshape(nb, PEER_TOK_BLOCK, D),
                          eidx.reshape(nb, PEER_TOK_BLOCK, PEER_HEADS, PEER_TOPK),
                          gates.reshape(nb, PEER_TOK_BLOCK, PEER_HEADS, PEER_TOPK)))
    return out.reshape(B, S, D)


def _resid_matmul_kernel(a_ref, w_ref, x_ref, g_ref, o_ref):
    acc = jnp.dot(a_ref[0].astype(jnp.bfloat16), w_ref[...].astype(jnp.bfloat16),
                  preferred_element_type=jnp.float32)
    o_ref[0] = x_ref[0] + g_ref[0] * acc


def _resid_matmul(a, w, x, g, tm=512):
    B, S, K = a.shape
    N = w.shape[1]
    return pl.pallas_call(
        _resid_matmul_kernel,
        grid=(B, S // tm),
        in_specs=[
            pl.BlockSpec((1, tm, K), lambda b, i: (b, i, 0)),
            pl.BlockSpec((K, N), lambda b, i: (0, 0)),
            pl.BlockSpec((1, tm, N), lambda b, i: (b, i, 0)),
            pl.BlockSpec((1, 1, N), lambda b, i: (b, 0, 0)),
        ],
        out_specs=pl.BlockSpec((1, tm, N), lambda b, i: (b, i, 0)),
        out_shape=jax.ShapeDtypeStruct((B, S, N), jnp.float32),
    )(a, w, x, g.reshape(B, 1, N))


def kernel(x, c, w_ada, b_ada, norm1_g, norm2_g, w_in, hg_lb, hg_onorm_g, q_norm_g, k_norm_g,
           w_a, w_b, w_out, peer_wq, peer_subkeys, peer_u, peer_v):
    B, S, D = x.shape
    depth = w_ada.shape[0]
    lb_all = jnp.cumsum(jax.nn.softmax(hg_lb.astype(jnp.float32), axis=0), axis=0)
    split_at = [int(v) for v in np.cumsum(np.array(IN_SPLIT))[:-1]]
    cs = jax.nn.silu(c)
    for l in range(depth):
        mod = cs @ w_ada[l] + b_ada[l]
        sh1, sc1, g1, sh2, sc2, g2 = jnp.split(mod, 6, axis=-1)
        h = _rmsnorm(x, norm1_g[l]) * (1.0 + sc1[:, None]) + sh1[:, None]
        proj = h @ w_in[l]
        hq, hfp, hi, hg, aq, ak, av, iq, ik, iw, ga, gb = jnp.split(proj, split_at, axis=-1)
        ya = _hgrn2_mix(hq, hfp, hi, hg, lb_all[l], hg_onorm_g[l])
        q = _rmsnorm(aq.reshape(B, S, ATT_HEADS, ATT_HEAD_DIM), q_norm_g[l])
        k = _rmsnorm(ak.reshape(B, S, ATT_GROUPS, ATT_HEAD_DIM), k_norm_g[l])
        v = av.reshape(B, S, ATT_GROUPS, ATT_HEAD_DIM)
        yb = _dsa_mix(q, k, v, iq.reshape(B, S, IDX_HEADS, IDX_DIM), ik, iw)
        merged = jax.nn.sigmoid(ga) * (ya @ w_a[l]) + jax.nn.sigmoid(gb) * (yb @ w_b[l])
        x = _resid_matmul(merged, w_out[l], x, g1)
        h2 = _rmsnorm(x, norm2_g[l]) * (1.0 + sc2[:, None]) + sh2[:, None]
        x = x + g2[:, None] * _peer_ffn(h2, peer_wq[l], peer_subkeys[l], peer_u[l], peer_v[l])
    return x
```

```python
import functools

import jax
import jax.numpy as jnp
import numpy as np
from jax import lax
from jax.experimental import pallas as pl
from jax.experimental.pallas import tpu as pltpu

D_MODEL = 1024
HG_HEADS = 4
HG_DK = 128
HG_DV = 128
HG_WIDTH = HG_HEADS * HG_DK
HG_CHUNK = 64
ATT_HEADS = 8
ATT_GROUPS = 2
ATT_HEAD_DIM = 64
ATT_REP = ATT_HEADS // ATT_GROUPS
ATT_WIDTH = ATT_HEADS * ATT_HEAD_DIM
ATT_KV_WIDTH = ATT_GROUPS * ATT_HEAD_DIM
IDX_HEADS = 8
IDX_DIM = 64
IDX_TOPK_MAX = 256
Q_BLOCK = 128
PEER_HEADS = 8
PEER_NKEYS = 128
PEER_DKEY = 256
PEER_TOPK = 16
PEER_TOK_BLOCK = 128
RMS_EPS = 1e-6
IN_SPLIT = (HG_WIDTH, HG_WIDTH, HG_WIDTH, HG_WIDTH,
            ATT_WIDTH, ATT_KV_WIDTH, ATT_KV_WIDTH,
            IDX_HEADS * IDX_DIM, IDX_DIM, IDX_HEADS,
            D_MODEL, D_MODEL)


def _rmsnorm(x, g):
    xf = x.astype(jnp.float32)
    y = xf * lax.rsqrt(jnp.mean(xf * xf, axis=-1, keepdims=True) + RMS_EPS)
    return (y * g.astype(jnp.float32)).astype(x.dtype)


def _hgrn2_mix(q, f_pre, i, g, lb, onorm_g):
    B, S, _ = q.shape
    C = HG_CHUNK
    N = S // C
    dt = q.dtype

    def heads(t):
        return t.astype(jnp.float32).reshape(B, N, C, HG_HEADS, -1).transpose(0, 3, 1, 2, 4)

    lbf = lb.astype(jnp.float32).reshape(HG_HEADS, 1, 1, HG_DK)
    qh = jax.nn.silu(heads(q))
    f = lbf + (1.0 - lbf) * jax.nn.sigmoid(heads(f_pre))
    kh = 1.0 - f
    vh = heads(i)
    b = jnp.cumsum(jnp.log(f), axis=3)
    b_mid = b[:, :, :, C // 2 - 1:C // 2]
    b_last = b[:, :, :, -1:]
    A = jnp.einsum('bhncd,bhnsd->bhncs', qh * jnp.exp(b - b_mid), kh * jnp.exp(b_mid - b))
    causal = jnp.tril(jnp.ones((C, C), dtype=bool))
    A = jnp.where(causal, A, 0.0)
    o = jnp.einsum('bhncs,bhnse->bhnce', A, vh)
    dS = jnp.einsum('bhnsd,bhnse->bhnde', kh * jnp.exp(b_last - b), vh)
    decay = jnp.exp(b_last[:, :, :, 0])

    def step(S_prev, inp):
        dec, ds = inp
        return dec[..., None] * S_prev + ds, S_prev

    S0 = jnp.zeros((B, HG_HEADS, HG_DK, HG_DV), jnp.float32)
    _, S_before = lax.scan(step, S0, (jnp.moveaxis(decay, 2, 0), jnp.moveaxis(dS, 2, 0)))
    S_before = jnp.moveaxis(S_before, 0, 2)
    o = o + jnp.einsum('bhncd,bhnde->bhnce', qh * jnp.exp(b), S_before)
    o = _rmsnorm(o, onorm_g) * jax.nn.silu(heads(g))
    return o.transpose(0, 2, 3, 1, 4).reshape(B, S, HG_WIDTH).astype(dt)


DSA_TQ = 128
DSA_KC = 512
_INT_MIN = -2 ** 31
_FILL_KEY = -2139095041
_NEG_BIG = -1e30


def _sortable_key(x):
    b = lax.bitcast_convert_type(x, jnp.int32)
    return b ^ ((b >> 31) & jnp.int32(0x7FFFFFFF))


def _dsa_kernel(q_ref, iq_ref, iw_ref, k_ref, vl_ref, vr_ref, ik_ref, qg_ref, kg_ref, o_ref,
                kn_ref, qs_ref, keys_ref, lim_ref, m_ref, l_ref, acc_ref, *, seq, topk):
    TQ, KC = DSA_TQ, DSA_KC
    j = pl.program_id(1)
    n_kc = (j * TQ + TQ + KC - 1) // KC
    idx_scale = (IDX_DIM ** -0.5) * (IDX_HEADS ** -0.5)
    att_scale = ATT_HEAD_DIM ** -0.5

    @pl.when(j == 0)
    def _():
        def body(c, carry):
            r0 = pl.multiple_of(c * KC, KC)
            for g in range(ATT_GROUPS):
                kk = k_ref[0, g, pl.ds(r0, KC), :]
                ms = jnp.mean(kk * kk, axis=-1, keepdims=True)
                kn_ref[g, pl.ds(r0, KC), :] = (kk * lax.rsqrt(ms + RMS_EPS) * kg_ref[...]).astype(jnp.bfloat16)
            return carry
        lax.fori_loop(0, seq // KC, body, 0)

    for h in range(ATT_HEADS):
        qq = q_ref[0, h]
        ms = jnp.mean(qq * qq, axis=-1, keepdims=True)
        qs_ref[h] = (qq * lax.rsqrt(ms + RMS_EPS) * qg_ref[...] * att_scale).astype(jnp.bfloat16)

    rows = j * TQ + lax.broadcasted_iota(jnp.int32, (TQ, 1), 0)
    lane_cols = lax.broadcasted_iota(jnp.int32, (1, KC), 1)

    def idx_body(c, carry):
        r0 = pl.multiple_of(c * KC, KC)
        ikc = ik_ref[0, pl.ds(r0, KC), :]
        acc = jnp.zeros((TQ, KC), jnp.float32)
        for h in range(IDX_HEADS):
            s = lax.dot_general(iq_ref[0, h], ikc, (((1,), (1,)), ((), ())),
                                preferred_element_type=jnp.float32)
            acc = acc + iw_ref[0, :, h:h + 1] * jnp.maximum(s, 0.0)
        score = acc * idx_scale
        cols = c * KC + lane_cols
        score = jnp.where(cols <= rows, score, -jnp.inf)
        keys_ref[c] = _sortable_key(score)
        return carry
    lax.fori_loop(0, n_kc, idx_body, 0)

    def lane_fold(m):
        part = m[:, 0:128]
        for i in range(1, KC // 128):
            part = part + m[:, i * 128:(i + 1) * 128]
        return part

    def count_ge(cand):
        def body(c, acc):
            return acc + lane_fold(jnp.where(keys_ref[c] >= cand, 1.0, 0.0))
        acc = lax.fori_loop(0, n_kc, body, jnp.zeros((TQ, 128), jnp.float32))
        return jnp.sum(acc, axis=1, keepdims=True)

    kf = float(topk)
    ans0 = jnp.where(count_ge(jnp.zeros((TQ, 1), jnp.int32)) >= kf,
                     jnp.int32(0), jnp.int32(_INT_MIN))

    def bit_body(i, ans):
        cand = ans | jnp.left_shift(jnp.int32(1), 30 - i)
        return jnp.where(count_ge(cand) >= kf, cand, ans)
    thr = lax.fori_loop(0, 31, bit_body, ans0)

    n_gt = count_ge(thr + 1)
    n_ge = count_ge(thr)
    need = kf - n_gt
    tie = jnp.logical_and(n_ge - n_gt > need, thr != _FILL_KEY)
    lim_ref[...] = jnp.full((TQ, 1), seq, jnp.int32)

    @pl.when(jnp.max(jnp.where(tie, 1.0, 0.0)) > 0.0)
    def _():
        def count_eq_lt(cand):
            def body(c, acc):
                cols = c * KC + lane_cols
                hit = jnp.logical_and(keys_ref[c] == thr, cols < cand)
                return acc + lane_fold(jnp.where(hit, 1.0, 0.0))
            acc = lax.fori_loop(0, n_kc, body, jnp.zeros((TQ, 128), jnp.float32))
            return jnp.sum(acc, axis=1, keepdims=True)

        nbits = max(1, int(np.ceil(np.log2(seq))))

        def bit2(i, lo):
            cand = lo | jnp.left_shift(jnp.int32(1), nbits - 1 - i)
            return jnp.where(count_eq_lt(cand) < need, cand, lo)
        lo = lax.fori_loop(0, nbits, bit2, jnp.zeros((TQ, 1), jnp.int32))
        lim_ref[...] = jnp.where(tie, lo, jnp.int32(seq))

    lim = lim_ref[...]

    m_ref[...] = jnp.full(m_ref.shape, _NEG_BIG, jnp.float32)
    l_ref[...] = jnp.zeros(l_ref.shape, jnp.float32)
    acc_ref[...] = jnp.zeros(acc_ref.shape, jnp.float32)
    left = lax.broadcasted_iota(jnp.int32, (1, 128), 1) < ATT_HEAD_DIM

    def att_body(c, carry):
        r0 = pl.multiple_of(c * KC, KC)
        cols = c * KC + lane_cols
        key = keys_ref[c]
        sel = jnp.logical_or(key > thr, jnp.logical_and(key == thr, cols <= lim))
        sel = jnp.logical_and(sel, cols <= rows)
        dist = (rows - cols).astype(jnp.float32)
        for g in range(ATT_GROUPS):
            kc = kn_ref[g, pl.ds(r0, KC), :]
            for r in range(ATT_REP):
                h = g * ATT_REP + r
                slope = float(2.0 ** (-8.0 * (h + 1) / ATT_HEADS))
                s = lax.dot_general(qs_ref[h], kc, (((1,), (1,)), ((), ())),
                                    preferred_element_type=jnp.float32)
                sc = jnp.where(sel, s - slope * dist, _NEG_BIG)
                m_old = m_ref[h]
                m_new = jnp.maximum(m_old, jnp.max(sc, axis=1, keepdims=True))
                p = jnp.where(sel, jnp.exp(sc - m_new), 0.0)
                alpha = jnp.exp(m_old - m_new)
                l_ref[h] = alpha * l_ref[h] + jnp.sum(p, axis=1, keepdims=True)
                m_ref[h] = m_new
                vsrc = vl_ref if h % 2 == 0 else vr_ref
                pv = jnp.dot(p.astype(jnp.bfloat16), vsrc[0, g, pl.ds(r0, KC), :],
                             preferred_element_type=jnp.float32)
                mine = left if h % 2 == 0 else jnp.logical_not(left)
                acc_ref[h // 2] = acc_ref[h // 2] * jnp.where(mine, alpha, 1.0) + pv
        return carry
    lax.fori_loop(0, n_kc, att_body, 0)

    for pair in range(ATT_HEADS // 2):
        den = jnp.where(left, l_ref[2 * pair], l_ref[2 * pair + 1])
        o_ref[0, :, pair * 128:(pair + 1) * 128] = acc_ref[pair] / den


def dsa_attention(qh, iqh, iw, kh, vl, vr, ik, qg, kg):
    B, _, S, _ = qh.shape
    TQ, KC = DSA_TQ, DSA_KC
    assert S % KC == 0 and S % TQ == 0
    topk = min(IDX_TOPK_MAX, S // 4)
    kern = functools.partial(_dsa_kernel, seq=S, topk=topk)
    return pl.pallas_call(
        kern,
        grid=(B, S // TQ),
        in_specs=[
            pl.BlockSpec((1, ATT_HEADS, TQ, 64), lambda b, j: (b, 0, j, 0)),
            pl.BlockSpec((1, IDX_HEADS, TQ, 64), lambda b, j: (b, 0, j, 0)),
            pl.BlockSpec((1, TQ, IDX_HEADS), lambda b, j: (b, j, 0)),
            pl.BlockSpec((1, ATT_GROUPS, S, 64), lambda b, j: (b, 0, 0, 0)),
            pl.BlockSpec((1, ATT_GROUPS, S, 128), lambda b, j: (b, 0, 0, 0)),
            pl.BlockSpec((1, ATT_GROUPS, S, 128), lambda b, j: (b, 0, 0, 0)),
            pl.BlockSpec((1, S, 64), lambda b, j: (b, 0, 0)),
            pl.BlockSpec((1, 64), lambda b, j: (0, 0)),
            pl.BlockSpec((1, 64), lambda b, j: (0, 0)),
        ],
        out_specs=pl.BlockSpec((1, TQ, ATT_HEADS * 64), lambda b, j: (b, j, 0)),
        out_shape=jax.ShapeDtypeStruct((B, S, ATT_HEADS * 64), jnp.float32),
        scratch_shapes=[
            pltpu.VMEM((ATT_GROUPS, S, 64), jnp.bfloat16),
            pltpu.VMEM((ATT_HEADS, TQ, 64), jnp.bfloat16),
            pltpu.VMEM((S // KC, TQ, KC), jnp.int32),
            pltpu.VMEM((TQ, 1), jnp.int32),
            pltpu.VMEM((ATT_HEADS, TQ, 1), jnp.float32),
            pltpu.VMEM((ATT_HEADS, TQ, 1), jnp.float32),
            pltpu.VMEM((ATT_HEADS // 2, TQ, 128), jnp.float32),
        ],
        compiler_params=pltpu.CompilerParams(
            dimension_semantics=("arbitrary", "arbitrary"),
            vmem_limit_bytes=48 * 1024 * 1024),
        name="dsa_attention",
    )(qh, iqh, iw, kh, vl, vr, ik, qg, kg)


def _dsa_glue(aq, ak, av, iq, ik, iw, qg, kg):
    B, S, _ = aq.shape
    qh = aq.reshape(B, S, ATT_HEADS, 64).transpose(0, 2, 1, 3)
    iqh = iq.reshape(B, S, IDX_HEADS, 64).transpose(0, 2, 1, 3).astype(jnp.bfloat16)
    kh = ak.reshape(B, S, ATT_GROUPS, 64).transpose(0, 2, 1, 3)
    vh = av.reshape(B, S, ATT_GROUPS, 64).transpose(0, 2, 1, 3).astype(jnp.bfloat16)
    z = jnp.zeros_like(vh)
    vl = jnp.concatenate([vh, z], -1)
    vr = jnp.concatenate([z, vh], -1)
    return dsa_attention(qh, iqh, iw, kh, vl, vr, ik.astype(jnp.bfloat16), qg.reshape(1, 64), kg.reshape(1, 64))


def _peer_ffn(h, wq, subkeys, u, vtab):
    B, S, D = h.shape
    T = B * S
    hf = h.reshape(T, D)
    qry = (hf @ wq).reshape(T, PEER_HEADS, 2, PEER_DKEY // 2)
    sc = jnp.einsum('thpd,hpnd->thpn', qry, subkeys).astype(jnp.float32)
    s1, i1 = lax.top_k(sc[:, :, 0], PEER_TOPK)
    s2, i2 = lax.top_k(sc[:, :, 1], PEER_TOPK)
    cand = (s1[..., :, None] + s2[..., None, :]).reshape(T, PEER_HEADS, PEER_TOPK * PEER_TOPK)
    cidx = (i1[..., :, None] * PEER_NKEYS + i2[..., None, :]).reshape(T, PEER_HEADS, PEER_TOPK * PEER_TOPK)
    top_s, pos = lax.top_k(cand, PEER_TOPK)
    eidx = jnp.take_along_axis(cidx, pos, axis=-1)
    gates = jax.nn.softmax(top_s, axis=-1).astype(h.dtype)
    nb = T // PEER_TOK_BLOCK

    def block(args):
        xb, eb, gb = args
        a = jax.nn.gelu(jnp.einsum('td,thkd->thk', xb, u[eb]), approximate=False)
        return jnp.einsum('thk,thkd->td', gb * a, vtab[eb])

    out = lax.map(block, (hf.reshape(nb, PEER_TOK_BLOCK, D),
                          eidx.reshape(nb, PEER_TOK_BLOCK, PEER_HEADS, PEER_TOPK),
                          gates.reshape(nb, PEER_TOK_BLOCK, PEER_HEADS, PEER_TOPK)))
    return out.reshape(B, S, D)


def _resid_matmul_kernel(a_ref, w_ref, x_ref, g_ref, o_ref):
    acc = jnp.dot(a_ref[0].astype(jnp.bfloat16), w_ref[...].astype(jnp.bfloat16),
                  preferred_element_type=jnp.float32)
    o_ref[0] = x_ref[0] + g_ref[0] * acc


def _resid_matmul(a, w, x, g, tm=512):
    B, S, K = a.shape
    N = w.shape[1]
    return pl.pallas_call(
        _resid_matmul_kernel,
        grid=(B, S // tm),
        in_specs=[
            pl.BlockSpec((1, tm, K), lambda b, i: (b, i, 0)),
            pl.BlockSpec((K, N), lambda b, i: (0, 0)),
            pl.BlockSpec((1, tm, N), lambda b, i: (b, i, 0)),
            pl.BlockSpec((1, 1, N), lambda b, i: (b, 0, 0)),
        ],
        out_specs=pl.BlockSpec((1, tm, N), lambda b, i: (b, i, 0)),
        out_shape=jax.ShapeDtypeStruct((B, S, N), jnp.float32),
    )(a, w, x, g.reshape(B, 1, N))


def kernel(x, c, w_ada, b_ada, norm1_g, norm2_g, w_in, hg_lb, hg_onorm_g, q_norm_g, k_norm_g,
           w_a, w_b, w_out, peer_wq, peer_subkeys, peer_u, peer_v):
    B, S, D = x.shape
    depth = w_ada.shape[0]
    lb_all = jnp.cumsum(jax.nn.softmax(hg_lb.astype(jnp.float32), axis=0), axis=0)
    split_at = [int(v) for v in np.cumsum(np.array(IN_SPLIT))[:-1]]
    cs = jax.nn.silu(c)
    for l in range(depth):
        mod = cs @ w_ada[l] + b_ada[l]
        sh1, sc1, g1, sh2, sc2, g2 = jnp.split(mod, 6, axis=-1)
        h = _rmsnorm(x, norm1_g[l]) * (1.0 + sc1[:, None]) + sh1[:, None]
        proj = h @ w_in[l]
        hq, hfp, hi, hg, aq, ak, av, iq, ik, iw, ga, gb = jnp.split(proj, split_at, axis=-1)
        ya = _hgrn2_mix(hq, hfp, hi, hg, lb_all[l], hg_onorm_g[l])
        yb = _dsa_glue(aq, ak, av, iq, ik, iw, q_norm_g[l], k_norm_g[l])
        merged = jax.nn.sigmoid(ga) * (ya @ w_a[l]) + jax.nn.sigmoid(gb) * (yb @ w_b[l])
        x = _resid_matmul(merged, w_out[l], x, g1)
        h2 = _rmsnorm(x, norm2_g[l]) * (1.0 + sc2[:, None]) + sh2[:, None]
        x = x + g2[:, None] * _peer_ffn(h2, peer_wq[l], peer_subkeys[l], peer_u[l], peer_v[l])
    return x
```

```python
import functools

import jax
import jax.numpy as jnp
import numpy as np
from jax import lax
from jax.experimental import pallas as pl
from jax.experimental.pallas import tpu as pltpu

D_MODEL = 1024
HG_HEADS = 4
HG_DK = 128
HG_DV = 128
HG_WIDTH = HG_HEADS * HG_DK
HG_CHUNK = 64
ATT_HEADS = 8
ATT_GROUPS = 2
ATT_HEAD_DIM = 64
ATT_REP = ATT_HEADS // ATT_GROUPS
ATT_WIDTH = ATT_HEADS * ATT_HEAD_DIM
ATT_KV_WIDTH = ATT_GROUPS * ATT_HEAD_DIM
IDX_HEADS = 8
IDX_DIM = 64
IDX_TOPK_MAX = 256
Q_BLOCK = 128
PEER_HEADS = 8
PEER_NKEYS = 128
PEER_DKEY = 256
PEER_TOPK = 16
PEER_TOK_BLOCK = 128
RMS_EPS = 1e-6
IN_SPLIT = (HG_WIDTH, HG_WIDTH, HG_WIDTH, HG_WIDTH,
            ATT_WIDTH, ATT_KV_WIDTH, ATT_KV_WIDTH,
            IDX_HEADS * IDX_DIM, IDX_DIM, IDX_HEADS,
            D_MODEL, D_MODEL)


def _rmsnorm(x, g):
    xf = x.astype(jnp.float32)
    y = xf * lax.rsqrt(jnp.mean(xf * xf, axis=-1, keepdims=True) + RMS_EPS)
    return (y * g.astype(jnp.float32)).astype(x.dtype)


def _hgrn2_mix(q, f_pre, i, g, lb, onorm_g):
    B, S, _ = q.shape
    C = HG_CHUNK
    N = S // C
    dt = q.dtype

    def heads(t):
        return t.astype(jnp.float32).reshape(B, N, C, HG_HEADS, -1).transpose(0, 3, 1, 2, 4)

    lbf = lb.astype(jnp.float32).reshape(HG_HEADS, 1, 1, HG_DK)
    qh = jax.nn.silu(heads(q))
    f = lbf + (1.0 - lbf) * jax.nn.sigmoid(heads(f_pre))
    kh = 1.0 - f
    vh = heads(i)
    b = jnp.cumsum(jnp.log(f), axis=3)
    b_mid = b[:, :, :, C // 2 - 1:C // 2]
    b_last = b[:, :, :, -1:]
    A = jnp.einsum('bhncd,bhnsd->bhncs', qh * jnp.exp(b - b_mid), kh * jnp.exp(b_mid - b))
    causal = jnp.tril(jnp.ones((C, C), dtype=bool))
    A = jnp.where(causal, A, 0.0)
    o = jnp.einsum('bhncs,bhnse->bhnce', A, vh)
    dS = jnp.einsum('bhnsd,bhnse->bhnde', kh * jnp.exp(b_last - b), vh)
    decay = jnp.exp(b_last[:, :, :, 0])

    def step(S_prev, inp):
        dec, ds = inp
        return dec[..., None] * S_prev + ds, S_prev

    S0 = jnp.zeros((B, HG_HEADS, HG_DK, HG_DV), jnp.float32)
    _, S_before = lax.scan(step, S0, (jnp.moveaxis(decay, 2, 0), jnp.moveaxis(dS, 2, 0)))
    S_before = jnp.moveaxis(S_before, 0, 2)
    o = o + jnp.einsum('bhncd,bhnde->bhnce', qh * jnp.exp(b), S_before)
    o = _rmsnorm(o, onorm_g) * jax.nn.silu(heads(g))
    return o.transpose(0, 2, 3, 1, 4).reshape(B, S, HG_WIDTH).astype(dt)


DSA_TQ = 128
DSA_KC = 512
_INT_MIN = -2 ** 31
_FILL_KEY = -2139095041
_NEG_BIG = -1e30


def _sortable_key(x):
    b = lax.bitcast_convert_type(x, jnp.int32)
    return b ^ ((b >> 31) & jnp.int32(0x7FFFFFFF))


def _dsa_kernel(q_ref, iq_ref, iw_ref, k_ref, vl_ref, vr_ref, ik_ref, qg_ref, kg_ref, o_ref,
                kn_ref, qs_ref, keys_ref, lim_ref, m_ref, l_ref, acc_ref, *, seq, topk):
    TQ, KC = DSA_TQ, DSA_KC
    j = pl.program_id(1)
    n_kc = (j * TQ + TQ + KC - 1) // KC
    idx_scale = (IDX_DIM ** -0.5) * (IDX_HEADS ** -0.5)
    att_scale = ATT_HEAD_DIM ** -0.5

    @pl.when(j == 0)
    def _():
        def body(c, carry):
            r0 = pl.multiple_of(c * KC, KC)
            for g in range(ATT_GROUPS):
                kk = k_ref[0, g, pl.ds(r0, KC), :]
                ms = jnp.mean(kk * kk, axis=-1, keepdims=True)
                kn_ref[g, pl.ds(r0, KC), :] = (kk * lax.rsqrt(ms + RMS_EPS) * kg_ref[...]).astype(jnp.bfloat16)
            return carry
        lax.fori_loop(0, seq // KC, body, 0)

    for h in range(ATT_HEADS):
        qq = q_ref[0, h]
        ms = jnp.mean(qq * qq, axis=-1, keepdims=True)
        qs_ref[h] = (qq * lax.rsqrt(ms + RMS_EPS) * qg_ref[...] * att_scale).astype(jnp.bfloat16)

    rows = j * TQ + lax.broadcasted_iota(jnp.int32, (TQ, 1), 0)
    lane_cols = lax.broadcasted_iota(jnp.int32, (1, KC), 1)

    def idx_body(c, carry):
        r0 = pl.multiple_of(c * KC, KC)
        ikc = ik_ref[0, pl.ds(r0, KC), :]
        acc = jnp.zeros((TQ, KC), jnp.float32)
        for h in range(IDX_HEADS):
            s = lax.dot_general(iq_ref[0, h], ikc, (((1,), (1,)), ((), ())),
                                preferred_element_type=jnp.float32)
            acc = acc + iw_ref[0, :, h:h + 1] * jnp.maximum(s, 0.0)
        score = acc * idx_scale
        cols = c * KC + lane_cols
        score = jnp.where(cols <= rows, score, -jnp.inf)
        keys_ref[c] = _sortable_key(score)
        return carry
    lax.fori_loop(0, n_kc, idx_body, 0)

    def lane_fold(m):
        part = m[:, 0:128]
        for i in range(1, KC // 128):
            part = part + m[:, i * 128:(i + 1) * 128]
        return part

    def count_ge(cand):
        def body(c, acc):
            return acc + lane_fold(jnp.where(keys_ref[c] >= cand, 1.0, 0.0))
        acc = lax.fori_loop(0, n_kc, body, jnp.zeros((TQ, 128), jnp.float32))
        return jnp.sum(acc, axis=1, keepdims=True)

    kf = float(topk)
    ans0 = jnp.where(count_ge(jnp.zeros((TQ, 1), jnp.int32)) >= kf,
                     jnp.int32(0), jnp.int32(_INT_MIN))

    def bit_body(i, ans):
        cand = ans | jnp.left_shift(jnp.int32(1), 30 - i)
        return jnp.where(count_ge(cand) >= kf, cand, ans)
    thr = lax.fori_loop(0, 31, bit_body, ans0)

    n_gt = count_ge(thr + 1)
    n_ge = count_ge(thr)
    need = kf - n_gt
    tie = jnp.logical_and(n_ge - n_gt > need, thr != _FILL_KEY)
    lim_ref[...] = jnp.full((TQ, 1), seq, jnp.int32)

    @pl.when(jnp.max(jnp.where(tie, 1.0, 0.0)) > 0.0)
    def _():
        def count_eq_lt(cand):
            def body(c, acc):
                cols = c * KC + lane_cols
                hit = jnp.logical_and(keys_ref[c] == thr, cols < cand)
                return acc + lane_fold(jnp.where(hit, 1.0, 0.0))
            acc = lax.fori_loop(0, n_kc, body, jnp.zeros((TQ, 128), jnp.float32))
            return jnp.sum(acc, axis=1, keepdims=True)

        nbits = max(1, int(np.ceil(np.log2(seq))))

        def bit2(i, lo):
            cand = lo | jnp.left_shift(jnp.int32(1), nbits - 1 - i)
            return jnp.where(count_eq_lt(cand) < need, cand, lo)
        lo = lax.fori_loop(0, nbits, bit2, jnp.zeros((TQ, 1), jnp.int32))
        lim_ref[...] = jnp.where(tie, lo, jnp.int32(seq))

    lim = lim_ref[...]

    m_ref[...] = jnp.full(m_ref.shape, _NEG_BIG, jnp.float32)
    l_ref[...] = jnp.zeros(l_ref.shape, jnp.float32)
    acc_ref[...] = jnp.zeros(acc_ref.shape, jnp.float32)
    left = lax.broadcasted_iota(jnp.int32, (1, 128), 1) < ATT_HEAD_DIM

    def att_body(c, carry):
        r0 = pl.multiple_of(c * KC, KC)
        cols = c * KC + lane_cols
        key = keys_ref[c]
        sel = jnp.logical_or(key > thr, jnp.logical_and(key == thr, cols <= lim))
        sel = jnp.logical_and(sel, cols <= rows)
        dist = (rows - cols).astype(jnp.float32)
        for g in range(ATT_GROUPS):
            kc = kn_ref[g, pl.ds(r0, KC), :]
            for r in range(ATT_REP):
                h = g * ATT_REP + r
                slope = float(2.0 ** (-8.0 * (h + 1) / ATT_HEADS))
                s = lax.dot_general(qs_ref[h], kc, (((1,), (1,)), ((), ())),
                                    preferred_element_type=jnp.float32)
                sc = jnp.where(sel, s - slope * dist, _NEG_BIG)
                m_old = m_ref[h]
                m_new = jnp.maximum(m_old, jnp.max(sc, axis=1, keepdims=True))
                p = jnp.where(sel, jnp.exp(sc - m_new), 0.0)
                alpha = jnp.exp(m_old - m_new)
                l_ref[h] = alpha * l_ref[h] + jnp.sum(p, axis=1, keepdims=True)
                m_ref[h] = m_new
                vsrc = vl_ref if h % 2 == 0 else vr_ref
                pv = jnp.dot(p.astype(jnp.bfloat16), vsrc[0, g, pl.ds(r0, KC), :],
                             preferred_element_type=jnp.float32)
                mine = left if h % 2 == 0 else jnp.logical_not(left)
                acc_ref[h // 2] = acc_ref[h // 2] * jnp.where(mine, alpha, 1.0) + pv
        return carry
    lax.fori_loop(0, n_kc, att_body, 0)

    for pair in range(ATT_HEADS // 2):
        den = jnp.where(left, l_ref[2 * pair], l_ref[2 * pair + 1])
        o_ref[0, :, pair * 128:(pair + 1) * 128] = acc_ref[pair] / den


def dsa_attention(qh, iqh, iw, kh, vl, vr, ik, qg, kg):
    B, _, S, _ = qh.shape
    TQ, KC = DSA_TQ, DSA_KC
    assert S % KC == 0 and S % TQ == 0
    topk = min(IDX_TOPK_MAX, S // 4)
    kern = functools.partial(_dsa_kernel, seq=S, topk=topk)
    return pl.pallas_call(
        kern,
        grid=(B, S // TQ),
        in_specs=[
            pl.BlockSpec((1, ATT_HEADS, TQ, 64), lambda b, j: (b, 0, j, 0)),
            pl.BlockSpec((1, IDX_HEADS, TQ, 64), lambda b, j: (b, 0, j, 0)),
            pl.BlockSpec((1, TQ, IDX_HEADS), lambda b, j: (b, j, 0)),
            pl.BlockSpec((1, ATT_GROUPS, S, 64), lambda b, j: (b, 0, 0, 0)),
            pl.BlockSpec((1, ATT_GROUPS, S, 128), lambda b, j: (b, 0, 0, 0)),
            pl.BlockSpec((1, ATT_GROUPS, S, 128), lambda b, j: (b, 0, 0, 0)),
            pl.BlockSpec((1, S, 64), lambda b, j: (b, 0, 0)),
            pl.BlockSpec((1, 64), lambda b, j: (0, 0)),
            pl.BlockSpec((1, 64), lambda b, j: (0, 0)),
        ],
        out_specs=pl.BlockSpec((1, TQ, ATT_HEADS * 64), lambda b, j: (b, j, 0)),
        out_shape=jax.ShapeDtypeStruct((B, S, ATT_HEADS * 64), jnp.float32),
        scratch_shapes=[
            pltpu.VMEM((ATT_GROUPS, S, 64), jnp.bfloat16),
            pltpu.VMEM((ATT_HEADS, TQ, 64), jnp.bfloat16),
            pltpu.VMEM((S // KC, TQ, KC), jnp.int32),
            pltpu.VMEM((TQ, 1), jnp.int32),
            pltpu.VMEM((ATT_HEADS, TQ, 1), jnp.float32),
            pltpu.VMEM((ATT_HEADS, TQ, 1), jnp.float32),
            pltpu.VMEM((ATT_HEADS // 2, TQ, 128), jnp.float32),
        ],
        compiler_params=pltpu.CompilerParams(
            dimension_semantics=("arbitrary", "arbitrary"),
            vmem_limit_bytes=48 * 1024 * 1024),
        name="dsa_attention",
    )(qh, iqh, iw, kh, vl, vr, ik, qg, kg)


def _dsa_glue(aq, ak, av, iq, ik, iw, qg, kg):
    B, S, _ = aq.shape
    qh = aq.reshape(B, S, ATT_HEADS, 64).transpose(0, 2, 1, 3)
    iqh = iq.reshape(B, S, IDX_HEADS, 64).transpose(0, 2, 1, 3).astype(jnp.bfloat16)
    kh = ak.reshape(B, S, ATT_GROUPS, 64).transpose(0, 2, 1, 3)
    vh = av.reshape(B, S, ATT_GROUPS, 64).transpose(0, 2, 1, 3).astype(jnp.bfloat16)
    z = jnp.zeros_like(vh)
    vl = jnp.concatenate([vh, z], -1)
    vr = jnp.concatenate([z, vh], -1)
    return dsa_attention(qh, iqh, iw, kh, vl, vr, ik.astype(jnp.bfloat16), qg.reshape(1, 64), kg.reshape(1, 64))


PEER_RT = 256
PEER_ET = 128
_ROW_CHUNKS = D_MODEL // 128
_EDGES = PEER_HEADS * PEER_TOPK


def _extract_topk_rows(work, n_out, val_ref, payload=None, pay_ref=None):
    nrows = work.shape[0]
    rowi = lax.broadcasted_iota(jnp.int32, work.shape, 0)
    for it in range(n_out):
        m = jnp.max(work, axis=0, keepdims=True)
        am = jnp.min(jnp.where(work == m, rowi, nrows), axis=0, keepdims=True)
        hit = rowi == am
        val_ref[it:it + 1, :] = m
        if payload is None:
            pay_ref[it:it + 1, :] = am
        else:
            pay_ref[it:it + 1, :] = jnp.max(jnp.where(hit, payload, -1), axis=0, keepdims=True)
        work = jnp.where(hit, -jnp.inf, work)


def _peer_route_kernel(q_ref, sk_ref, e_ref, g_ref, s1_ref, i1_ref, s2_ref, i2_ref, ts_ref, te_ref):
    K = PEER_TOPK
    for h in range(PEER_HEADS):
        for p, (s_ref, i_ref) in enumerate(((s1_ref, i1_ref), (s2_ref, i2_ref))):
            col = (h * 2 + p) * 128
            qs = q_ref[:, col:col + 128].astype(jnp.bfloat16)
            sk = sk_ref[h, p].astype(jnp.bfloat16)
            sc = lax.dot_general(sk, qs, (((1,), (1,)), ((), ())),
                                 preferred_element_type=jnp.float32)
            _extract_topk_rows(sc, K, s_ref, None, i_ref)
        s2 = s2_ref[...]
        i2 = i2_ref[...]
        cand = jnp.concatenate([s1_ref[a:a + 1, :] + s2 for a in range(K)], axis=0)
        cidx = jnp.concatenate([i1_ref[a:a + 1, :] * PEER_NKEYS + i2 for a in range(K)], axis=0)
        _extract_topk_rows(cand, K, ts_ref, cidx, te_ref)
        ts = ts_ref[...]
        ex = jnp.exp(ts - jnp.max(ts, axis=0, keepdims=True))
        g_ref[h * K:(h + 1) * K, :] = ex / jnp.sum(ex, axis=0, keepdims=True)
        e_ref[h * K:(h + 1) * K, :] = te_ref[...]


def peer_route(qry, subkeys):
    T = qry.shape[0]
    RT = PEER_RT
    assert T % RT == 0
    K = PEER_TOPK
    return pl.pallas_call(
        _peer_route_kernel,
        grid=(T // RT,),
        in_specs=[
            pl.BlockSpec((RT, qry.shape[1]), lambda i: (i, 0)),
            pl.BlockSpec(subkeys.shape, lambda i: (0, 0, 0, 0)),
        ],
        out_specs=[pl.BlockSpec((_EDGES, RT), lambda i: (0, i)),
                   pl.BlockSpec((_EDGES, RT), lambda i: (0, i))],
        out_shape=[jax.ShapeDtypeStruct((_EDGES, T), jnp.int32),
                   jax.ShapeDtypeStruct((_EDGES, T), jnp.float32)],
        scratch_shapes=[
            pltpu.VMEM((K, RT), jnp.float32), pltpu.VMEM((K, RT), jnp.int32),
            pltpu.VMEM((K, RT), jnp.float32), pltpu.VMEM((K, RT), jnp.int32),
            pltpu.VMEM((K, RT), jnp.float32), pltpu.VMEM((K, RT), jnp.int32),
        ],
        compiler_params=pltpu.CompilerParams(dimension_semantics=("arbitrary",)),
        name="peer_route",
    )(qry, subkeys)


def pack_table(tab):
    n = tab.shape[0]
    tb = lax.bitcast_convert_type(tab.astype(jnp.bfloat16), jnp.uint16).astype(jnp.uint32)
    tb = tb.reshape(n, _ROW_CHUNKS // 2, 2, 128)
    return tb[:, :, 0, :] | (tb[:, :, 1, :] << 16)


def _gather_rows(idx_ref, tab_ref, stage_ref, t):
    for i in range(_EDGES):
        e = idx_ref[t, i]
        stage_ref[4 * i:4 * i + 4, :] = tab_ref[e]


def _split_bf16(x):
    hi = x.astype(jnp.bfloat16)
    lo = (x - hi.astype(jnp.float32)).astype(jnp.bfloat16)
    return jnp.concatenate([hi, lo], axis=0)


def _peer_dot_kernel(idx_ref, tab_ref, x_ref, a_ref, stage_ref):
    n = _EDGES * _ROW_CHUNKS
    sub = lax.broadcasted_iota(jnp.int32, (2 * _ROW_CHUNKS, n), 0) % _ROW_CHUNKS
    lane = lax.broadcasted_iota(jnp.int32, (2 * _ROW_CHUNKS, n), 1)
    diag = sub == lane % _ROW_CHUNKS
    lane8 = lax.broadcasted_iota(jnp.int32, (_ROW_CHUNKS, n), 1)

    def body(t, carry):
        _gather_rows(idx_ref, tab_ref, stage_ref, t)
        m = pltpu.bitcast(stage_ref[...], jnp.bfloat16)
        x16 = _split_bf16(x_ref[t])
        r = lax.dot_general(x16, m, (((1,), (1,)), ((), ())),
                            preferred_element_type=jnp.float32)
        d = jnp.where(diag, r, 0.0)
        f = d[0:_ROW_CHUNKS] + d[_ROW_CHUNKS:]
        for sh in (1, 2, 4):
            f = f + jnp.where((lane8 & sh) == 0, pltpu.roll(f, n - sh, 1), pltpu.roll(f, sh, 1))
        tot = jnp.sum(f, axis=0, keepdims=True)
        for g in range(n // 128):
            a_ref[t, g:g + 1, :] = tot[:, g * 128:(g + 1) * 128]
        return carry
    lax.fori_loop(0, a_ref.shape[0], body, 0)


def _peer_mix_kernel(idx_ref, tab_ref, a_ref, g_ref, y_ref, stage_ref):
    sub = lax.broadcasted_iota(jnp.int32, (_ROW_CHUNKS, 128), 0)
    lane = lax.broadcasted_iota(jnp.int32, (_ROW_CHUNKS, 128), 1)
    diag = sub == lane % _ROW_CHUNKS
    ngroups = _EDGES * _ROW_CHUNKS // 128

    def body(t, carry):
        _gather_rows(idx_ref, tab_ref, stage_ref, t)
        m = pltpu.bitcast(stage_ref[...], jnp.bfloat16)
        a = a_ref[t]
        c = g_ref[t] * (0.5 * a * (1.0 + lax.erf(a * (2.0 ** -0.5))))
        lhs = jnp.concatenate(
            [jnp.where(diag, jnp.broadcast_to(c[g:g + 1, :], (_ROW_CHUNKS, 128)), 0.0) for g in range(ngroups)],
            axis=1)
        r = jnp.dot(_split_bf16(lhs), m, preferred_element_type=jnp.float32)
        y_ref[t] = r[0:_ROW_CHUNKS] + r[_ROW_CHUNKS:]
        return carry
    lax.fori_loop(0, y_ref.shape[0], body, 0)


def _expert_call(kern, name, idx, tab, ins):
    T = idx.shape[0]
    ET = PEER_ET
    assert T % ET == 0
    blk = pl.BlockSpec((ET, _ROW_CHUNKS, 128), lambda i: (i, 0, 0))
    return pl.pallas_call(
        kern,
        grid=(T // ET,),
        in_specs=[
            pl.BlockSpec((ET, _EDGES), lambda i: (i, 0), memory_space=pltpu.SMEM),
            pl.BlockSpec(tab.shape, lambda i: (0, 0, 0), pipeline_mode=pl.Buffered(1)),
        ] + [blk] * len(ins),
        out_specs=blk,
        out_shape=jax.ShapeDtypeStruct((T, _ROW_CHUNKS, 128), jnp.float32),
        scratch_shapes=[pltpu.VMEM((_EDGES * 4, 128), jnp.uint32)],
        compiler_params=pltpu.CompilerParams(
            dimension_semantics=("arbitrary",),
            vmem_limit_bytes=44 * 1024 * 1024),
        name=name,
    )(idx, tab, *ins)


def peer_experts(h2, eidx_t, gates_t, u, vtab):
    T = h2.shape[0]
    idx = eidx_t.T
    gexp = jnp.repeat(gates_t.T.reshape(T, _EDGES // 16, 16), _ROW_CHUNKS, axis=-1)
    x3 = h2.reshape(T, _ROW_CHUNKS, 128)
    a = _expert_call(_peer_dot_kernel, "peer_dot", idx, pack_table(u), [x3])
    y = _expert_call(_peer_mix_kernel, "peer_mix", idx, pack_table(vtab), [a, gexp])
    return y.reshape(T, D_MODEL)


def _peer_ffn(h, wq, subkeys, u, vtab):
    B, S, D = h.shape
    hf = h.reshape(B * S, D)
    e_t, g_t = peer_route(hf @ wq, subkeys)
    return peer_experts(hf, e_t, g_t, u, vtab).reshape(B, S, D)


def _resid_matmul_kernel(a_ref, w_ref, x_ref, g_ref, o_ref):
    acc = jnp.dot(a_ref[0].astype(jnp.bfloat16), w_ref[...].astype(jnp.bfloat16),
                  preferred_element_type=jnp.float32)
    o_ref[0] = x_ref[0] + g_ref[0] * acc


def _resid_matmul(a, w, x, g, tm=512):
    B, S, K = a.shape
    N = w.shape[1]
    return pl.pallas_call(
        _resid_matmul_kernel,
        grid=(B, S // tm),
        in_specs=[
            pl.BlockSpec((1, tm, K), lambda b, i: (b, i, 0)),
            pl.BlockSpec((K, N), lambda b, i: (0, 0)),
            pl.BlockSpec((1, tm, N), lambda b, i: (b, i, 0)),
            pl.BlockSpec((1, 1, N), lambda b, i: (b, 0, 0)),
        ],
        out_specs=pl.BlockSpec((1, tm, N), lambda b, i: (b, i, 0)),
        out_shape=jax.ShapeDtypeStruct((B, S, N), jnp.float32),
    )(a, w, x, g.reshape(B, 1, N))


def kernel(x, c, w_ada, b_ada, norm1_g, norm2_g, w_in, hg_lb, hg_onorm_g, q_norm_g, k_norm_g,
           w_a, w_b, w_out, peer_wq, peer_subkeys, peer_u, peer_v):
    B, S, D = x.shape
    depth = w_ada.shape[0]
    lb_all = jnp.cumsum(jax.nn.softmax(hg_lb.astype(jnp.float32), axis=0), axis=0)
    split_at = [int(v) for v in np.cumsum(np.array(IN_SPLIT))[:-1]]
    cs = jax.nn.silu(c)
    for l in range(depth):
        mod = cs @ w_ada[l] + b_ada[l]
        sh1, sc1, g1, sh2, sc2, g2 = jnp.split(mod, 6, axis=-1)
        h = _rmsnorm(x, norm1_g[l]) * (1.0 + sc1[:, None]) + sh1[:, None]
        proj = h @ w_in[l]
        hq, hfp, hi, hg, aq, ak, av, iq, ik, iw, ga, gb = jnp.split(proj, split_at, axis=-1)
        ya = _hgrn2_mix(hq, hfp, hi, hg, lb_all[l], hg_onorm_g[l])
        yb = _dsa_glue(aq, ak, av, iq, ik, iw, q_norm_g[l], k_norm_g[l])
        merged = jax.nn.sigmoid(ga) * (ya @ w_a[l]) + jax.nn.sigmoid(gb) * (yb @ w_b[l])
        x = _resid_matmul(merged, w_out[l], x, g1)
        h2 = _rmsnorm(x, norm2_g[l]) * (1.0 + sc2[:, None]) + sh2[:, None]
        x = x + g2[:, None] * _peer_ffn(h2, peer_wq[l], peer_subkeys[l], peer_u[l], peer_v[l])
    return x
```

```python
import functools

import jax
import jax.numpy as jnp
import numpy as np
from jax import lax
from jax.experimental import pallas as pl
from jax.experimental.pallas import tpu as pltpu

D_MODEL = 1024
HG_HEADS = 4
HG_DK = 128
HG_WIDTH = HG_HEADS * HG_DK
HG_CHUNK = 64
ATT_HEADS = 8
ATT_GROUPS = 2
ATT_HEAD_DIM = 64
ATT_REP = ATT_HEADS // ATT_GROUPS
IDX_HEADS = 8
IDX_DIM = 64
IDX_TOPK_MAX = 256
PEER_HEADS = 8
PEER_NKEYS = 128
PEER_TOPK = 16
RMS_EPS = 1e-6
IN_SPLIT = (HG_WIDTH, HG_WIDTH, HG_WIDTH, HG_WIDTH,
            ATT_HEADS * ATT_HEAD_DIM, ATT_GROUPS * ATT_HEAD_DIM, ATT_GROUPS * ATT_HEAD_DIM,
            IDX_HEADS * IDX_DIM, IDX_DIM, IDX_HEADS,
            D_MODEL, D_MODEL)

_VMEM_LIMIT = 48 * 1024 * 1024


def _ada_kernel(c_ref, w_ref, b_ref, o_ref):
    c = c_ref[...]
    cs = c * (1.0 / (1.0 + jnp.exp(-c)))
    o_ref[...] = jnp.dot(cs, w_ref[...], preferred_element_type=jnp.float32,
                         precision=lax.Precision.HIGHEST) + b_ref[...]


def ada_modulation(c, w, b):
    B, D = c.shape
    N = w.shape[1]
    tn = 1536
    assert N % tn == 0
    return pl.pallas_call(
        _ada_kernel,
        grid=(N // tn,),
        in_specs=[pl.BlockSpec((B, D), lambda j: (0, 0)),
                  pl.BlockSpec((D, tn), lambda j: (0, j)),
                  pl.BlockSpec((1, tn), lambda j: (0, j))],
        out_specs=pl.BlockSpec((B, tn), lambda j: (0, j)),
        out_shape=jax.ShapeDtypeStruct((B, N), jnp.float32),
        compiler_params=pltpu.CompilerParams(dimension_semantics=("arbitrary",),
                                             vmem_limit_bytes=_VMEM_LIMIT),
        name="ada_modulation",
    )(c, w, b.reshape(1, N))


_PROJ_SEGS = (("hg", 2048, jnp.float32), ("q", 1024, jnp.float32), ("k", 256, jnp.float32),
              ("iw", 128, jnp.float32), ("iq", 1024, jnp.bfloat16), ("ik", 128, jnp.bfloat16),
              ("v", 512, jnp.bfloat16), ("gate", 2048, jnp.float32))
PROJ_TM = 256


def pack_w_in(w_in):
    D = w_in.shape[0]
    offs = np.cumsum((0,) + IN_SPLIT)
    hq, hf, hi, hg, aq, ak, av, iq, ik, iw, ga, gb = [w_in[:, offs[i]:offs[i + 1]] for i in range(12)]

    def pad_heads(w, nh, low=True):
        w = w.reshape(D, nh, 64)
        z = jnp.zeros_like(w)
        return jnp.concatenate([w, z] if low else [z, w], axis=-1).reshape(D, nh * 128)

    iwp = jnp.concatenate([iw, jnp.zeros((D, 128 - IDX_HEADS), w_in.dtype)], axis=1)
    cols = [hq, hf, hi, hg, pad_heads(aq, ATT_HEADS), pad_heads(ak, ATT_GROUPS), iwp,
            pad_heads(iq, IDX_HEADS), pad_heads(ik, 1),
            pad_heads(av, ATT_GROUPS, True), pad_heads(av, ATT_GROUPS, False), ga, gb]
    return jnp.concatenate(cols, axis=1).astype(jnp.bfloat16)


def _proj_kernel(x_ref, g_ref, sc_ref, sh_ref, w_ref, *o_refs):
    x = x_ref[0]
    ms = jnp.mean(x * x, axis=-1, keepdims=True)
    h = x * lax.rsqrt(ms + RMS_EPS) * g_ref[...] * (1.0 + sc_ref[0]) + sh_ref[0]
    hb = h.astype(jnp.bfloat16)
    c0 = 0
    for (_, width, _), o_ref in zip(_PROJ_SEGS, o_refs):
        for s in range(0, width, 512):
            e = min(s + 512, width)
            o_ref[0, :, s:e] = jnp.dot(hb, w_ref[:, c0 + s:c0 + e],
                                       preferred_element_type=jnp.float32).astype(o_ref.dtype)
        c0 += width


def in_projection(x, norm_g, sc, sh, w_packed):
    B, S, D = x.shape
    tm = PROJ_TM
    assert S % tm == 0
    ntot = sum(w for _, w, _ in _PROJ_SEGS)
    assert w_packed.shape == (D, ntot)
    return pl.pallas_call(
        _proj_kernel,
        grid=(B, S // tm),
        in_specs=[
            pl.BlockSpec((1, tm, D), lambda b, i: (b, i, 0)),
            pl.BlockSpec((1, D), lambda b, i: (0, 0)),
            pl.BlockSpec((1, 1, D), lambda b, i: (b, 0, 0)),
            pl.BlockSpec((1, 1, D), lambda b, i: (b, 0, 0)),
            pl.BlockSpec((D, ntot), lambda b, i: (0, 0), pipeline_mode=pl.Buffered(1)),
        ],
        out_specs=[pl.BlockSpec((1, tm, w), lambda b, i: (b, i, 0)) for _, w, _ in _PROJ_SEGS],
        out_shape=[jax.ShapeDtypeStruct((B, S, w), dt) for _, w, dt in _PROJ_SEGS],
        compiler_params=pltpu.CompilerParams(dimension_semantics=("arbitrary", "arbitrary"),
                                             vmem_limit_bytes=_VMEM_LIMIT),
        name="in_projection",
    )(x, norm_g.reshape(1, D), sc.reshape(B, 1, D), sh.reshape(B, 1, D), w_packed)


HG_LT = 512


def _split3_bf16(x):
    a = x.astype(jnp.bfloat16)
    r = x - a.astype(jnp.float32)
    b = r.astype(jnp.bfloat16)
    c = (r - b.astype(jnp.float32)).astype(jnp.bfloat16)
    return a, b, c


def _hgrn_kernel(q_ref, f_ref, i_ref, g_ref, lb_ref, og_ref, o_ref, st_ref, *, layer):
    C = HG_CHUNK
    bf = jnp.bfloat16

    @pl.when(pl.program_id(1) == 0)
    def _():
        st_ref[...] = jnp.zeros(st_ref.shape, jnp.float32)

    lbr = lb_ref[...]
    e = jnp.exp(lbr - jnp.max(lbr, axis=0, keepdims=True))
    sm = e / jnp.sum(e, axis=0, keepdims=True)
    lb_all = jnp.sum(sm[0:layer + 1], axis=0, keepdims=True)

    r_i = lax.broadcasted_iota(jnp.int32, (C, C), 0)
    c_i = lax.broadcasted_iota(jnp.int32, (C, C), 1)
    causal = c_i <= r_i
    tri = jnp.where(causal, 1.0, 0.0).astype(bf)
    og = og_ref[...]

    def chunk(c, carry):
        r0 = pl.multiple_of(c * C, C)
        for h in range(HG_HEADS):
            cs = slice(h * HG_DK, (h + 1) * HG_DK)
            lb = lb_all[:, cs]
            qr = q_ref[0, pl.ds(r0, C), cs]
            q = qr * (1.0 / (1.0 + jnp.exp(-qr)))
            f = lb + (1.0 - lb) * (1.0 / (1.0 + jnp.exp(-f_ref[0, pl.ds(r0, C), cs])))
            k = 1.0 - f
            v = i_ref[0, pl.ds(r0, C), cs]
            l1, l2, l3 = _split3_bf16(jnp.log(f))
            b = (jnp.dot(tri, l1, preferred_element_type=jnp.float32)
                 + jnp.dot(tri, l2, preferred_element_type=jnp.float32)
                 + jnp.dot(tri, l3, preferred_element_type=jnp.float32))
            bm = b[C // 2 - 1:C // 2, :]
            bl = b[C - 1:C, :]
            vb = v.astype(bf)
            a = lax.dot_general((q * jnp.exp(b - bm)).astype(bf), (k * jnp.exp(bm - b)).astype(bf),
                                (((1,), (1,)), ((), ())), preferred_element_type=jnp.float32)
            a = jnp.where(causal, a, 0.0)
            o = jnp.dot(a.astype(bf), vb, preferred_element_type=jnp.float32)
            st = st_ref[h]
            o = o + lax.dot_general((q * jnp.exp(b)).astype(bf), st.astype(bf),
                                    (((1,), (1,)), ((), ())), preferred_element_type=jnp.float32)
            kd = (k * jnp.exp(bl - b)).astype(bf)
            ds = lax.dot_general(vb, kd, (((0,), (0,)), ((), ())), preferred_element_type=jnp.float32)
            st_ref[h] = st * jnp.exp(bl) + ds
            o = o * lax.rsqrt(jnp.mean(o * o, axis=-1, keepdims=True) + RMS_EPS) * og
            gr = g_ref[0, pl.ds(r0, C), cs]
            o_ref[0, pl.ds(r0, C), cs] = (o * (gr * (1.0 / (1.0 + jnp.exp(-gr))))).astype(o_ref.dtype)
        return carry
    lax.fori_loop(0, q_ref.shape[1] // C, chunk, 0)


def hgrn2(hg, hg_lb, onorm_g, layer):
    B, S, _ = hg.shape
    LT = HG_LT
    assert S % LT == 0

    def piece(p):
        return pl.BlockSpec((1, LT, HG_WIDTH), lambda b, i, p=p: (b, i, p))
    return pl.pallas_call(
        functools.partial(_hgrn_kernel, layer=layer),
        grid=(B, S // LT),
        in_specs=[piece(0), piece(1), piece(2), piece(3),
                  pl.BlockSpec(hg_lb.shape, lambda b, i: (0, 0)),
                  pl.BlockSpec((1, HG_DK), lambda b, i: (0, 0))],
        out_specs=pl.BlockSpec((1, LT, HG_WIDTH), lambda b, i: (b, i, 0)),
        out_shape=jax.ShapeDtypeStruct((B, S, HG_WIDTH), jnp.bfloat16),
        scratch_shapes=[pltpu.VMEM((HG_HEADS, HG_DK, HG_DK), jnp.float32)],
        compiler_params=pltpu.CompilerParams(dimension_semantics=("arbitrary", "arbitrary"),
                                             vmem_limit_bytes=_VMEM_LIMIT),
        name="hgrn2",
    )(hg, hg, hg, hg, hg_lb, onorm_g.reshape(1, HG_DK))


OUT_TM = 256


def _out_kernel(ya_ref, yb_ref, ga_ref, gb_ref, x_ref, g1_ref, n2_ref, sc_ref, sh_ref,
                wa_ref, wb_ref, wo_ref, wq_ref, x1_ref, h2_ref, qry_ref):
    bf = jnp.bfloat16

    def sig(z):
        return 1.0 / (1.0 + jnp.exp(-z))
    ma = jnp.dot(ya_ref[0], wa_ref[...], preferred_element_type=jnp.float32)
    mb = jnp.dot(yb_ref[0], wb_ref[...], preferred_element_type=jnp.float32)
    merged = sig(ga_ref[0]) * ma + sig(gb_ref[0]) * mb
    x1 = x_ref[0] + g1_ref[0] * jnp.dot(merged.astype(bf), wo_ref[...], preferred_element_type=jnp.float32)
    x1_ref[0] = x1
    ms = jnp.mean(x1 * x1, axis=-1, keepdims=True)
    h2 = x1 * lax.rsqrt(ms + RMS_EPS) * n2_ref[...] * (1.0 + sc_ref[0]) + sh_ref[0]
    h2_ref[0] = h2
    qry_ref[0] = jnp.dot(h2.astype(bf), wq_ref[...], preferred_element_type=jnp.float32)


def out_projection(ya, yb, gate, x, g1, norm2_g, sc2, sh2, w_a, w_b, w_out, wq):
    B, S, D = x.shape
    tm = OUT_TM
    bf = jnp.bfloat16
    NQ = wq.shape[1]

    def tok(width, col=0):
        return pl.BlockSpec((1, tm, width), lambda b, i, col=col: (b, i, col))

    def per_batch():
        return pl.BlockSpec((1, 1, D), lambda b, i: (b, 0, 0))

    def const(shape):
        return pl.BlockSpec(shape, lambda b, i: (0,) * len(shape), pipeline_mode=pl.Buffered(1))
    return pl.pallas_call(
        _out_kernel,
        grid=(B, S // tm),
        in_specs=[tok(HG_WIDTH), tok(512), tok(D, 0), tok(D, 1), tok(D), per_batch(),
                  const((1, D)), per_batch(), per_batch(),
                  const(w_a.shape), const(w_b.shape), const(w_out.shape), const(wq.shape)],
        out_specs=[tok(D), tok(D), tok(NQ)],
        out_shape=[jax.ShapeDtypeStruct((B, S, D), jnp.float32),
                   jax.ShapeDtypeStruct((B, S, D), jnp.float32),
                   jax.ShapeDtypeStruct((B, S, NQ), jnp.float32)],
        compiler_params=pltpu.CompilerParams(dimension_semantics=("arbitrary", "arbitrary"),
                                             vmem_limit_bytes=_VMEM_LIMIT),
        name="out_projection",
    )(ya, yb, gate, gate, x, g1.reshape(B, 1, D), norm2_g.reshape(1, D), sc2.reshape(B, 1, D),
      sh2.reshape(B, 1, D), w_a.astype(bf), w_b.astype(bf), w_out.astype(bf), wq.astype(bf))


DSA_TQ = 128
DSA_KC = 512
_INT_MIN = -2 ** 31
_FILL_KEY = -2139095041
_NEG_BIG = -1e30


def _sortable_key(x):
    b = lax.bitcast_convert_type(x, jnp.int32)
    return b ^ ((b >> 31) & jnp.int32(0x7FFFFFFF))


def _dsa_kernel(q_ref, iq_ref, iw_ref, k_ref, v_ref, ik_ref, qg_ref, kg_ref, o_ref,
                kn_ref, qs_ref, keys_ref, lim_ref, m_ref, l_ref, acc_ref, *, seq, topk):
    TQ, KC = DSA_TQ, DSA_KC
    j = pl.program_id(1)
    n_kc = (j * TQ + TQ + KC - 1) // KC
    idx_scale = (IDX_DIM ** -0.5) * (IDX_HEADS ** -0.5)
    att_scale = ATT_HEAD_DIM ** -0.5

    @pl.when(j == 0)
    def _():
        def body(c, carry):
            r0 = pl.multiple_of(c * KC, KC)
            for g in range(ATT_GROUPS):
                kk = k_ref[0, pl.ds(r0, KC), g * 128:(g + 1) * 128]
                ms = jnp.sum(kk * kk, axis=-1, keepdims=True) * (1.0 / ATT_HEAD_DIM)
                kn_ref[g, pl.ds(r0, KC), :] = (kk * lax.rsqrt(ms + RMS_EPS) * kg_ref[...]).astype(jnp.bfloat16)
            return carry
        lax.fori_loop(0, seq // KC, body, 0)

    for h in range(ATT_HEADS):
        qq = q_ref[0, :, h * 128:(h + 1) * 128]
        ms = jnp.sum(qq * qq, axis=-1, keepdims=True) * (1.0 / ATT_HEAD_DIM)
        qs_ref[h] = (qq * lax.rsqrt(ms + RMS_EPS) * qg_ref[...] * att_scale).astype(jnp.bfloat16)

    rows = j * TQ + lax.broadcasted_iota(jnp.int32, (TQ, 1), 0)
    lane_cols = lax.broadcasted_iota(jnp.int32, (1, KC), 1)

    def idx_body(c, carry):
        r0 = pl.multiple_of(c * KC, KC)
        ikc = ik_ref[0, pl.ds(r0, KC), :]
        acc = jnp.zeros((TQ, KC), jnp.float32)
        for h in range(IDX_HEADS):
            s = lax.dot_general(iq_ref[0, :, h * 128:(h + 1) * 128], ikc, (((1,), (1,)), ((), ())),
                                preferred_element_type=jnp.float32)
            acc = acc + iw_ref[0, :, h:h + 1] * jnp.maximum(s, 0.0)
        score = acc * idx_scale
        cols = c * KC + lane_cols
        score = jnp.where(cols <= rows, score, -jnp.inf)
        keys_ref[c] = _sortable_key(score)
        return carry
    lax.fori_loop(0, n_kc, idx_body, 0)

    def lane_fold(m):
        part = m[:, 0:128]
        for i in range(1, KC // 128):
            part = part + m[:, i * 128:(i + 1) * 128]
        return part

    def count_ge(cand):
        def body(c, acc):
            return acc + lane_fold(jnp.where(keys_ref[c] >= cand, 1.0, 0.0))
        acc = lax.fori_loop(0, n_kc, body, jnp.zeros((TQ, 128), jnp.float32))
        return jnp.sum(acc, axis=1, keepdims=True)

    kf = float(topk)
    ans0 = jnp.where(count_ge(jnp.zeros((TQ, 1), jnp.int32)) >= kf,
                     jnp.int32(0), jnp.int32(_INT_MIN))

    def bit_body(i, ans):
        cand = ans | jnp.left_shift(jnp.int32(1), 30 - i)
        return jnp.where(count_ge(cand) >= kf, cand, ans)
    thr = lax.fori_loop(0, 31, bit_body, ans0)

    n_gt = count_ge(thr + 1)
    n_ge = count_ge(thr)
    need = kf - n_gt
    tie = jnp.logical_and(n_ge - n_gt > need, thr != _FILL_KEY)
    lim_ref[...] = jnp.full((TQ, 1), seq, jnp.int32)

    @pl.when(jnp.max(jnp.where(tie, 1.0, 0.0)) > 0.0)
    def _():
        def count_eq_lt(cand):
            def body(c, acc):
                cols = c * KC + lane_cols
                hit = jnp.logical_and(keys_ref[c] == thr, cols < cand)
                return acc + lane_fold(jnp.where(hit, 1.0, 0.0))
            acc = lax.fori_loop(0, n_kc, body, jnp.zeros((TQ, 128), jnp.float32))
            return jnp.sum(acc, axis=1, keepdims=True)

        nbits = max(1, int(np.ceil(np.log2(seq))))

        def bit2(i, lo):
            cand = lo | jnp.left_shift(jnp.int32(1), nbits - 1 - i)
            return jnp.where(count_eq_lt(cand) < need, cand, lo)
        lo = lax.fori_loop(0, nbits, bit2, jnp.zeros((TQ, 1), jnp.int32))
        lim_ref[...] = jnp.where(tie, lo, jnp.int32(seq))

    lim = lim_ref[...]

    m_ref[...] = jnp.full(m_ref.shape, _NEG_BIG, jnp.float32)
    l_ref[...] = jnp.zeros(l_ref.shape, jnp.float32)
    acc_ref[...] = jnp.zeros(acc_ref.shape, jnp.float32)
    left = lax.broadcasted_iota(jnp.int32, (1, 128), 1) < ATT_HEAD_DIM

    def att_body(c, carry):
        r0 = pl.multiple_of(c * KC, KC)
        cols = c * KC + lane_cols
        key = keys_ref[c]
        sel = jnp.logical_or(key > thr, jnp.logical_and(key == thr, cols <= lim))
        sel = jnp.logical_and(sel, cols <= rows)
        dist = (rows - cols).astype(jnp.float32)
        for g in range(ATT_GROUPS):
            kc = kn_ref[g, pl.ds(r0, KC), :]
            for r in range(ATT_REP):
                h = g * ATT_REP + r
                slope = float(2.0 ** (-8.0 * (h + 1) / ATT_HEADS))
                s = lax.dot_general(qs_ref[h], kc, (((1,), (1,)), ((), ())),
                                    preferred_element_type=jnp.float32)
                sc = jnp.where(sel, s - slope * dist, _NEG_BIG)
                m_old = m_ref[h]
                m_new = jnp.maximum(m_old, jnp.max(sc, axis=1, keepdims=True))
                p = jnp.where(sel, jnp.exp(sc - m_new), 0.0)
                alpha = jnp.exp(m_old - m_new)
                l_ref[h] = alpha * l_ref[h] + jnp.sum(p, axis=1, keepdims=True)
                m_ref[h] = m_new
                vt = (h % 2) * ATT_GROUPS + g
                pv = jnp.dot(p.astype(jnp.bfloat16), v_ref[0, pl.ds(r0, KC), vt * 128:(vt + 1) * 128],
                             preferred_element_type=jnp.float32)
                mine = left if h % 2 == 0 else jnp.logical_not(left)
                acc_ref[h // 2] = acc_ref[h // 2] * jnp.where(mine, alpha, 1.0) + pv
        return carry
    lax.fori_loop(0, n_kc, att_body, 0)

    for pair in range(ATT_HEADS // 2):
        den = jnp.where(left, l_ref[2 * pair], l_ref[2 * pair + 1])
        o_ref[0, :, pair * 128:(pair + 1) * 128] = (acc_ref[pair] / den).astype(o_ref.dtype)


def dsa_attention(q, iq, iw, k, v, ik, qg, kg):
    B, S, _ = q.shape
    TQ, KC = DSA_TQ, DSA_KC
    assert S % KC == 0 and S % TQ == 0
    topk = min(IDX_TOPK_MAX, S // 4)
    kern = functools.partial(_dsa_kernel, seq=S, topk=topk)

    def pad_gain(g):
        return jnp.concatenate([g, jnp.zeros_like(g)]).reshape(1, 128)
    return pl.pallas_call(
        kern,
        grid=(B, S // TQ),
        in_specs=[
            pl.BlockSpec((1, TQ, ATT_HEADS * 128), lambda b, j: (b, j, 0)),
            pl.BlockSpec((1, TQ, IDX_HEADS * 128), lambda b, j: (b, j, 0)),
            pl.BlockSpec((1, TQ, 128), lambda b, j: (b, j, 0)),
            pl.BlockSpec((1, S, ATT_GROUPS * 128), lambda b, j: (b, 0, 0)),
            pl.BlockSpec((1, S, 2 * ATT_GROUPS * 128), lambda b, j: (b, 0, 0)),
            pl.BlockSpec((1, S, 128), lambda b, j: (b, 0, 0)),
            pl.BlockSpec((1, 128), lambda b, j: (0, 0)),
            pl.BlockSpec((1, 128), lambda b, j: (0, 0)),
        ],
        out_specs=pl.BlockSpec((1, TQ, ATT_HEADS * 64), lambda b, j: (b, j, 0)),
        out_shape=jax.ShapeDtypeStruct((B, S, ATT_HEADS * 64), jnp.bfloat16),
        scratch_shapes=[
            pltpu.VMEM((ATT_GROUPS, S, 128), jnp.bfloat16),
            pltpu.VMEM((ATT_HEADS, TQ, 128), jnp.bfloat16),
            pltpu.VMEM((S // KC, TQ, KC), jnp.int32),
            pltpu.VMEM((TQ, 1), jnp.int32),
            pltpu.VMEM((ATT_HEADS, TQ, 1), jnp.float32),
            pltpu.VMEM((ATT_HEADS, TQ, 1), jnp.float32),
            pltpu.VMEM((ATT_HEADS // 2, TQ, 128), jnp.float32),
        ],
        compiler_params=pltpu.CompilerParams(
            dimension_semantics=("arbitrary", "arbitrary"),
            vmem_limit_bytes=48 * 1024 * 1024),
        name="dsa_attention",
    )(q, iq, iw, k, v, ik, pad_gain(qg), pad_gain(kg))


PEER_RT = 256
PEER_ET = 128
_ROW_CHUNKS = D_MODEL // 128
_EDGES = PEER_HEADS * PEER_TOPK


def _extract_topk_rows(work, n_out, val_ref, payload=None, pay_ref=None):
    nrows = work.shape[0]
    rowi = lax.broadcasted_iota(jnp.int32, work.shape, 0)
    for it in range(n_out):
        m = jnp.max(work, axis=0, keepdims=True)
        am = jnp.min(jnp.where(work == m, rowi, nrows), axis=0, keepdims=True)
        hit = rowi == am
        val_ref[it:it + 1, :] = m
        if payload is None:
            pay_ref[it:it + 1, :] = am
        else:
            pay_ref[it:it + 1, :] = jnp.max(jnp.where(hit, payload, -1), axis=0, keepdims=True)
        work = jnp.where(hit, -jnp.inf, work)


def _peer_route_kernel(q_ref, sk_ref, e_ref, g_ref, s1_ref, i1_ref, s2_ref, i2_ref, ts_ref, te_ref):
    K = PEER_TOPK
    for h in range(PEER_HEADS):
        for p, (s_ref, i_ref) in enumerate(((s1_ref, i1_ref), (s2_ref, i2_ref))):
            col = (h * 2 + p) * 128
            qs = q_ref[:, col:col + 128].astype(jnp.bfloat16)
            sk = sk_ref[h, p].astype(jnp.bfloat16)
            sc = lax.dot_general(sk, qs, (((1,), (1,)), ((), ())),
                                 preferred_element_type=jnp.float32)
            _extract_topk_rows(sc, K, s_ref, None, i_ref)
        s2 = s2_ref[...]
        i2 = i2_ref[...]
        cand = jnp.concatenate([s1_ref[a:a + 1, :] + s2 for a in range(K)], axis=0)
        cidx = jnp.concatenate([i1_ref[a:a + 1, :] * PEER_NKEYS + i2 for a in range(K)], axis=0)
        _extract_topk_rows(cand, K, ts_ref, cidx, te_ref)
        ts = ts_ref[...]
        ex = jnp.exp(ts - jnp.max(ts, axis=0, keepdims=True))
        g_ref[h * K:(h + 1) * K, :] = ex / jnp.sum(ex, axis=0, keepdims=True)
        e_ref[h * K:(h + 1) * K, :] = te_ref[...]


def peer_route(qry, subkeys):
    T = qry.shape[0]
    RT = PEER_RT
    assert T % RT == 0
    K = PEER_TOPK
    return pl.pallas_call(
        _peer_route_kernel,
        grid=(T // RT,),
        in_specs=[
            pl.BlockSpec((RT, qry.shape[1]), lambda i: (i, 0)),
            pl.BlockSpec(subkeys.shape, lambda i: (0, 0, 0, 0)),
        ],
        out_specs=[pl.BlockSpec((_EDGES, RT), lambda i: (0, i)),
                   pl.BlockSpec((_EDGES, RT), lambda i: (0, i))],
        out_shape=[jax.ShapeDtypeStruct((_EDGES, T), jnp.int32),
                   jax.ShapeDtypeStruct((_EDGES, T), jnp.float32)],
        scratch_shapes=[
            pltpu.VMEM((K, RT), jnp.float32), pltpu.VMEM((K, RT), jnp.int32),
            pltpu.VMEM((K, RT), jnp.float32), pltpu.VMEM((K, RT), jnp.int32),
            pltpu.VMEM((K, RT), jnp.float32), pltpu.VMEM((K, RT), jnp.int32),
        ],
        compiler_params=pltpu.CompilerParams(dimension_semantics=("arbitrary",)),
        name="peer_route",
    )(qry, subkeys)


def pack_table(tab):
    n = tab.shape[0]
    tb = lax.bitcast_convert_type(tab.astype(jnp.bfloat16), jnp.uint16).astype(jnp.uint32)
    tb = tb.reshape(n, _ROW_CHUNKS // 2, 2, 128)
    return tb[:, :, 0, :] | (tb[:, :, 1, :] << 16)


def _gather_rows(idx_ref, tab_ref, stage_ref, t):
    for i in range(_EDGES):
        e = idx_ref[t, i]
        stage_ref[4 * i:4 * i + 4, :] = tab_ref[e]


def _split_bf16(x):
    hi = x.astype(jnp.bfloat16)
    lo = (x - hi.astype(jnp.float32)).astype(jnp.bfloat16)
    return jnp.concatenate([hi, lo], axis=0)


def _pipelined_tokens(n_tok, gather, compute):
    gather(0, 0)

    def body(tt, carry):
        t = 2 * tt
        gather(t + 1, 1)
        compute(t, 0)
        gather(jnp.minimum(t + 2, n_tok - 1), 0)
        compute(t + 1, 1)
        return carry
    lax.fori_loop(0, n_tok // 2, body, 0)


def _peer_dot_kernel(idx_ref, tab_ref, x_ref, a_ref, stage0_ref, stage1_ref):
    n = _EDGES * _ROW_CHUNKS
    sub = lax.broadcasted_iota(jnp.int32, (2 * _ROW_CHUNKS, n), 0) % _ROW_CHUNKS
    lane = lax.broadcasted_iota(jnp.int32, (2 * _ROW_CHUNKS, n), 1)
    diag = sub == lane % _ROW_CHUNKS
    lane8 = lax.broadcasted_iota(jnp.int32, (_ROW_CHUNKS, n), 1)
    stages = (stage0_ref, stage1_ref)

    def gather(t, slot):
        _gather_rows(idx_ref, tab_ref, stages[slot], t)

    def compute(t, slot):
        m = pltpu.bitcast(stages[slot][...], jnp.bfloat16)
        x16 = _split_bf16(x_ref[t])
        r = lax.dot_general(x16, m, (((1,), (1,)), ((), ())),
                            preferred_element_type=jnp.float32)
        d = jnp.where(diag, r, 0.0)
        f = d[0:_ROW_CHUNKS] + d[_ROW_CHUNKS:]
        for sh in (1, 2, 4):
            f = f + jnp.where((lane8 & sh) == 0, pltpu.roll(f, n - sh, 1), pltpu.roll(f, sh, 1))
        tot = jnp.sum(f, axis=0, keepdims=True)
        for g in range(n // 128):
            a_ref[t, g:g + 1, :] = tot[:, g * 128:(g + 1) * 128]

    _pipelined_tokens(a_ref.shape[0], gather, compute)


def _peer_mix_kernel(idx_ref, tab_ref, a_ref, g_ref, x1_ref, g2_ref, y_ref, stage0_ref, stage1_ref):
    sub = lax.broadcasted_iota(jnp.int32, (_ROW_CHUNKS, 128), 0)
    lane = lax.broadcasted_iota(jnp.int32, (_ROW_CHUNKS, 128), 1)
    diag = sub == lane % _ROW_CHUNKS
    ngroups = _EDGES * _ROW_CHUNKS // 128
    stages = (stage0_ref, stage1_ref)

    def gather(t, slot):
        _gather_rows(idx_ref, tab_ref, stages[slot], t)

    def compute(t, slot):
        m = pltpu.bitcast(stages[slot][...], jnp.bfloat16)
        a = a_ref[t]
        c = g_ref[t] * (0.5 * a * (1.0 + lax.erf(a * (2.0 ** -0.5))))
        lhs = jnp.concatenate(
            [jnp.where(diag, jnp.broadcast_to(c[g:g + 1, :], (_ROW_CHUNKS, 128)), 0.0) for g in range(ngroups)],
            axis=1)
        r = jnp.dot(_split_bf16(lhs), m, preferred_element_type=jnp.float32)
        y_ref[t] = x1_ref[t] + g2_ref[0] * (r[0:_ROW_CHUNKS] + r[_ROW_CHUNKS:])

    _pipelined_tokens(y_ref.shape[0], gather, compute)


def _expert_call(kern, name, idx, tab, ins, extra=(), extra_specs=()):
    T = idx.shape[0]
    ET = PEER_ET
    assert T % ET == 0
    blk = pl.BlockSpec((ET, _ROW_CHUNKS, 128), lambda i: (i, 0, 0))
    return pl.pallas_call(
        kern,
        grid=(T // ET,),
        in_specs=[
            pl.BlockSpec((ET, _EDGES), lambda i: (i, 0), memory_space=pltpu.SMEM),
            pl.BlockSpec(tab.shape, lambda i: (0, 0, 0), pipeline_mode=pl.Buffered(1)),
        ] + [blk] * len(ins) + list(extra_specs),
        out_specs=blk,
        out_shape=jax.ShapeDtypeStruct((T, _ROW_CHUNKS, 128), jnp.float32),
        scratch_shapes=[pltpu.VMEM((_EDGES * 4, 128), jnp.uint32)] * 2,
        compiler_params=pltpu.CompilerParams(
            dimension_semantics=("arbitrary",),
            vmem_limit_bytes=44 * 1024 * 1024),
        name=name,
    )(idx, tab, *ins, *extra)


def peer_experts(h2, eidx_t, gates_t, u, vtab, x1, g2, seq):
    T = h2.shape[0]
    ET = PEER_ET
    assert seq % ET == 0
    idx = eidx_t.T
    gexp = jnp.repeat(gates_t.T.reshape(T, _EDGES // 16, 16), _ROW_CHUNKS, axis=-1)

    def chunked(z):
        return z.reshape(z.shape[0], _ROW_CHUNKS, 128)
    a = _expert_call(_peer_dot_kernel, "peer_dot", idx, pack_table(u), [chunked(h2)])
    g2_spec = pl.BlockSpec((1, _ROW_CHUNKS, 128), lambda i: (i * ET // seq, 0, 0))
    y = _expert_call(_peer_mix_kernel, "peer_mix", idx, pack_table(vtab), [a, gexp, chunked(x1)],
                     extra=[chunked(g2)], extra_specs=[g2_spec])
    return y.reshape(T, D_MODEL)


def kernel(x, c, w_ada, b_ada, norm1_g, norm2_g, w_in, hg_lb, hg_onorm_g, q_norm_g, k_norm_g,
           w_a, w_b, w_out, peer_wq, peer_subkeys, peer_u, peer_v):
    B, S, D = x.shape
    T = B * S
    for l in range(w_ada.shape[0]):
        mod = ada_modulation(c, w_ada[l], b_ada[l])
        sh1, sc1, g1, sh2, sc2, g2 = jnp.split(mod, 6, axis=-1)
        hg, q, k, iw, iq, ik, v, gate = in_projection(x, norm1_g[l], sc1, sh1, pack_w_in(w_in[l]))
        ya = hgrn2(hg, hg_lb, hg_onorm_g[l], l)
        yb = dsa_attention(q, iq, iw, k, v, ik, q_norm_g[l], k_norm_g[l])
        x1, h2, qry = out_projection(ya, yb, gate, x, g1, norm2_g[l], sc2, sh2,
                                     w_a[l], w_b[l], w_out[l], peer_wq[l])
        e_t, g_t = peer_route(qry.reshape(T, -1), peer_subkeys[l])
        x = peer_experts(h2.reshape(T, D), e_t, g_t, peer_u[l], peer_v[l],
                         x1.reshape(T, D), g2, S).reshape(B, S, D)
    return x
```

```python
import functools

import jax
import jax.numpy as jnp
import numpy as np
from jax import lax
from jax.experimental import pallas as pl
from jax.experimental.pallas import tpu as pltpu

D_MODEL = 1024
HG_HEADS = 4
HG_DK = 128
HG_WIDTH = HG_HEADS * HG_DK
HG_CHUNK = 64
ATT_HEADS = 8
ATT_GROUPS = 2
ATT_HEAD_DIM = 64
ATT_REP = ATT_HEADS // ATT_GROUPS
IDX_HEADS = 8
IDX_DIM = 64
IDX_TOPK_MAX = 256
PEER_HEADS = 8
PEER_NKEYS = 128
PEER_TOPK = 16
RMS_EPS = 1e-6
IN_SPLIT = (HG_WIDTH, HG_WIDTH, HG_WIDTH, HG_WIDTH,
            ATT_HEADS * ATT_HEAD_DIM, ATT_GROUPS * ATT_HEAD_DIM, ATT_GROUPS * ATT_HEAD_DIM,
            IDX_HEADS * IDX_DIM, IDX_DIM, IDX_HEADS,
            D_MODEL, D_MODEL)

_VMEM_LIMIT = 48 * 1024 * 1024


def _ada_kernel(c_ref, w_ref, b_ref, o_ref):
    c = c_ref[...]
    cs = c * (1.0 / (1.0 + jnp.exp(-c)))
    o_ref[...] = jnp.dot(cs, w_ref[...], preferred_element_type=jnp.float32,
                         precision=lax.Precision.HIGHEST) + b_ref[...]


def ada_modulation(c, w, b):
    B, D = c.shape
    N = w.shape[1]
    tn = 1536
    assert N % tn == 0
    return pl.pallas_call(
        _ada_kernel,
        grid=(N // tn,),
        in_specs=[pl.BlockSpec((B, D), lambda j: (0, 0)),
                  pl.BlockSpec((D, tn), lambda j: (0, j)),
                  pl.BlockSpec((1, tn), lambda j: (0, j))],
        out_specs=pl.BlockSpec((B, tn), lambda j: (0, j)),
        out_shape=jax.ShapeDtypeStruct((B, N), jnp.float32),
        compiler_params=pltpu.CompilerParams(dimension_semantics=("arbitrary",),
                                             vmem_limit_bytes=_VMEM_LIMIT),
        name="ada_modulation",
    )(c, w, b.reshape(1, N))


_PROJ_SEGS = (("hg", 2048, jnp.float32), ("q", 1024, jnp.float32), ("k", 256, jnp.float32),
              ("iw", 128, jnp.float32), ("iq", 1024, jnp.bfloat16), ("ik", 128, jnp.bfloat16),
              ("v", 512, jnp.bfloat16), ("gate", 2048, jnp.float32))
PROJ_TM = 256


def pack_w_in(w_in):
    D = w_in.shape[0]
    offs = np.cumsum((0,) + IN_SPLIT)
    hq, hf, hi, hg, aq, ak, av, iq, ik, iw, ga, gb = [w_in[:, offs[i]:offs[i + 1]] for i in range(12)]

    def pad_heads(w, nh, low=True):
        w = w.reshape(D, nh, 64)
        z = jnp.zeros_like(w)
        return jnp.concatenate([w, z] if low else [z, w], axis=-1).reshape(D, nh * 128)

    iwp = jnp.concatenate([iw, jnp.zeros((D, 128 - IDX_HEADS), w_in.dtype)], axis=1)
    cols = [hq, hf, hi, hg, pad_heads(aq, ATT_HEADS), pad_heads(ak, ATT_GROUPS), iwp,
            pad_heads(iq, IDX_HEADS), pad_heads(ik, 1),
            pad_heads(av, ATT_GROUPS, True), pad_heads(av, ATT_GROUPS, False), ga, gb]
    return jnp.concatenate(cols, axis=1).astype(jnp.bfloat16)


def _proj_kernel(x_ref, g_ref, sc_ref, sh_ref, w_ref, *o_refs):
    x = x_ref[0]
    ms = jnp.mean(x * x, axis=-1, keepdims=True)
    h = x * lax.rsqrt(ms + RMS_EPS) * g_ref[...] * (1.0 + sc_ref[0]) + sh_ref[0]
    hb = h.astype(jnp.bfloat16)
    c0 = 0
    for (_, width, _), o_ref in zip(_PROJ_SEGS, o_refs):
        for s in range(0, width, 512):
            e = min(s + 512, width)
            o_ref[0, :, s:e] = jnp.dot(hb, w_ref[:, c0 + s:c0 + e],
                                       preferred_element_type=jnp.float32).astype(o_ref.dtype)
        c0 += width


def in_projection(x, norm_g, sc, sh, w_packed):
    B, S, D = x.shape
    tm = PROJ_TM
    assert S % tm == 0
    ntot = sum(w for _, w, _ in _PROJ_SEGS)
    assert w_packed.shape == (D, ntot)
    return pl.pallas_call(
        _proj_kernel,
        grid=(B, S // tm),
        in_specs=[
            pl.BlockSpec((1, tm, D), lambda b, i: (b, i, 0)),
            pl.BlockSpec((1, D), lambda b, i: (0, 0)),
            pl.BlockSpec((1, 1, D), lambda b, i: (b, 0, 0)),
            pl.BlockSpec((1, 1, D), lambda b, i: (b, 0, 0)),
            pl.BlockSpec((D, ntot), lambda b, i: (0, 0), pipeline_mode=pl.Buffered(1)),
        ],
        out_specs=[pl.BlockSpec((1, tm, w), lambda b, i: (b, i, 0)) for _, w, _ in _PROJ_SEGS],
        out_shape=[jax.ShapeDtypeStruct((B, S, w), dt) for _, w, dt in _PROJ_SEGS],
        compiler_params=pltpu.CompilerParams(dimension_semantics=("arbitrary", "arbitrary"),
                                             vmem_limit_bytes=_VMEM_LIMIT),
        name="in_projection",
    )(x, norm_g.reshape(1, D), sc.reshape(B, 1, D), sh.reshape(B, 1, D), w_packed)


HG_LT = 512


def _split3_bf16(x):
    a = x.astype(jnp.bfloat16)
    r = x - a.astype(jnp.float32)
    b = r.astype(jnp.bfloat16)
    c = (r - b.astype(jnp.float32)).astype(jnp.bfloat16)
    return a, b, c


def _hgrn_kernel(q_ref, f_ref, i_ref, g_ref, lb_ref, og_ref, o_ref, st_ref, *, layer):
    C = HG_CHUNK
    bf = jnp.bfloat16

    @pl.when(pl.program_id(1) == 0)
    def _():
        st_ref[...] = jnp.zeros(st_ref.shape, jnp.float32)

    lbr = lb_ref[...]
    e = jnp.exp(lbr - jnp.max(lbr, axis=0, keepdims=True))
    sm = e / jnp.sum(e, axis=0, keepdims=True)
    lb_all = jnp.sum(sm[0:layer + 1], axis=0, keepdims=True)

    r_i = lax.broadcasted_iota(jnp.int32, (C, C), 0)
    c_i = lax.broadcasted_iota(jnp.int32, (C, C), 1)
    causal = c_i <= r_i
    tri = jnp.where(causal, 1.0, 0.0).astype(bf)
    og = og_ref[...]

    def chunk(c, carry):
        r0 = pl.multiple_of(c * C, C)
        for h in range(HG_HEADS):
            cs = slice(h * HG_DK, (h + 1) * HG_DK)
            lb = lb_all[:, cs]
            qr = q_ref[0, pl.ds(r0, C), cs]
            q = qr * (1.0 / (1.0 + jnp.exp(-qr)))
            f = lb + (1.0 - lb) * (1.0 / (1.0 + jnp.exp(-f_ref[0, pl.ds(r0, C), cs])))
            k = 1.0 - f
            v = i_ref[0, pl.ds(r0, C), cs]
            l1, l2, l3 = _split3_bf16(jnp.log(f))
            b = (jnp.dot(tri, l1, preferred_element_type=jnp.float32)
                 + jnp.dot(tri, l2, preferred_element_type=jnp.float32)
                 + jnp.dot(tri, l3, preferred_element_type=jnp.float32))
            bm = b[C // 2 - 1:C // 2, :]
            bl = b[C - 1:C, :]
            vb = v.astype(bf)
            a = lax.dot_general((q * jnp.exp(b - bm)).astype(bf), (k * jnp.exp(bm - b)).astype(bf),
                                (((1,), (1,)), ((), ())), preferred_element_type=jnp.float32)
            a = jnp.where(causal, a, 0.0)
            o = jnp.dot(a.astype(bf), vb, preferred_element_type=jnp.float32)
            st = st_ref[h]
            o = o + lax.dot_general((q * jnp.exp(b)).astype(bf), st.astype(bf),
                                    (((1,), (1,)), ((), ())), preferred_element_type=jnp.float32)
            kd = (k * jnp.exp(bl - b)).astype(bf)
            ds = lax.dot_general(vb, kd, (((0,), (0,)), ((), ())), preferred_element_type=jnp.float32)
            st_ref[h] = st * jnp.exp(bl) + ds
            o = o * lax.rsqrt(jnp.mean(o * o, axis=-1, keepdims=True) + RMS_EPS) * og
            gr = g_ref[0, pl.ds(r0, C), cs]
            o_ref[0, pl.ds(r0, C), cs] = (o * (gr * (1.0 / (1.0 + jnp.exp(-gr))))).astype(o_ref.dtype)
        return carry
    lax.fori_loop(0, q_ref.shape[1] // C, chunk, 0)


def hgrn2(hg, hg_lb, onorm_g, layer):
    B, S, _ = hg.shape
    LT = HG_LT
    assert S % LT == 0

    def piece(p):
        return pl.BlockSpec((1, LT, HG_WIDTH), lambda b, i, p=p: (b, i, p))
    return pl.pallas_call(
        functools.partial(_hgrn_kernel, layer=layer),
        grid=(B, S // LT),
        in_specs=[piece(0), piece(1), piece(2), piece(3),
                  pl.BlockSpec(hg_lb.shape, lambda b, i: (0, 0)),
                  pl.BlockSpec((1, HG_DK), lambda b, i: (0, 0))],
        out_specs=pl.BlockSpec((1, LT, HG_WIDTH), lambda b, i: (b, i, 0)),
        out_shape=jax.ShapeDtypeStruct((B, S, HG_WIDTH), jnp.bfloat16),
        scratch_shapes=[pltpu.VMEM((HG_HEADS, HG_DK, HG_DK), jnp.float32)],
        compiler_params=pltpu.CompilerParams(dimension_semantics=("arbitrary", "arbitrary"),
                                             vmem_limit_bytes=_VMEM_LIMIT),
        name="hgrn2",
    )(hg, hg, hg, hg, hg_lb, onorm_g.reshape(1, HG_DK))


OUT_TM = 256


def _out_kernel(ya_ref, yb_ref, ga_ref, gb_ref, x_ref, g1_ref, n2_ref, sc_ref, sh_ref,
                wa_ref, wb_ref, wo_ref, wq_ref, x1_ref, h2_ref, qry_ref):
    bf = jnp.bfloat16

    def sig(z):
        return 1.0 / (1.0 + jnp.exp(-z))
    ma = jnp.dot(ya_ref[0], wa_ref[...], preferred_element_type=jnp.float32)
    mb = jnp.dot(yb_ref[0], wb_ref[...], preferred_element_type=jnp.float32)
    merged = sig(ga_ref[0]) * ma + sig(gb_ref[0]) * mb
    x1 = x_ref[0] + g1_ref[0] * jnp.dot(merged.astype(bf), wo_ref[...], preferred_element_type=jnp.float32)
    x1_ref[0] = x1
    ms = jnp.mean(x1 * x1, axis=-1, keepdims=True)
    h2 = x1 * lax.rsqrt(ms + RMS_EPS) * n2_ref[...] * (1.0 + sc_ref[0]) + sh_ref[0]
    h2_ref[0] = h2
    qry_ref[0] = jnp.dot(h2.astype(bf), wq_ref[...], preferred_element_type=jnp.float32)


def out_projection(ya, yb, gate, x, g1, norm2_g, sc2, sh2, w_a, w_b, w_out, wq):
    B, S, D = x.shape
    tm = OUT_TM
    bf = jnp.bfloat16
    NQ = wq.shape[1]

    def tok(width, col=0):
        return pl.BlockSpec((1, tm, width), lambda b, i, col=col: (b, i, col))

    def per_batch():
        return pl.BlockSpec((1, 1, D), lambda b, i: (b, 0, 0))

    def const(shape):
        return pl.BlockSpec(shape, lambda b, i: (0,) * len(shape), pipeline_mode=pl.Buffered(1))
    return pl.pallas_call(
        _out_kernel,
        grid=(B, S // tm),
        in_specs=[tok(HG_WIDTH), tok(512), tok(D, 0), tok(D, 1), tok(D), per_batch(),
                  const((1, D)), per_batch(), per_batch(),
                  const(w_a.shape), const(w_b.shape), const(w_out.shape), const(wq.shape)],
        out_specs=[tok(D), tok(D), tok(NQ)],
        out_shape=[jax.ShapeDtypeStruct((B, S, D), jnp.float32),
                   jax.ShapeDtypeStruct((B, S, D), jnp.float32),
                   jax.ShapeDtypeStruct((B, S, NQ), jnp.float32)],
        compiler_params=pltpu.CompilerParams(dimension_semantics=("arbitrary", "arbitrary"),
                                             vmem_limit_bytes=_VMEM_LIMIT),
        name="out_projection",
    )(ya, yb, gate, gate, x, g1.reshape(B, 1, D), norm2_g.reshape(1, D), sc2.reshape(B, 1, D),
      sh2.reshape(B, 1, D), w_a.astype(bf), w_b.astype(bf), w_out.astype(bf), wq.astype(bf))


DSA_TQ = 128
DSA_KC = 512
_INT_MIN = -2 ** 31
_FILL_KEY = -2139095041
_NEG_BIG = -1e30


def _sortable_key(x):
    b = lax.bitcast_convert_type(x, jnp.int32)
    return b ^ ((b >> 31) & jnp.int32(0x7FFFFFFF))


def _dsa_kernel(q_ref, iq_ref, iw_ref, k_ref, v_ref, ik_ref, qg_ref, kg_ref, o_ref,
                kn_ref, qs_ref, keys_ref, lim_ref, m_ref, l_ref, acc_ref, *, seq, topk):
    TQ, KC = DSA_TQ, DSA_KC
    j = pl.program_id(1)
    n_kc = (j * TQ + TQ + KC - 1) // KC
    idx_scale = (IDX_DIM ** -0.5) * (IDX_HEADS ** -0.5)
    att_scale = ATT_HEAD_DIM ** -0.5

    @pl.when(j == 0)
    def _():
        def body(c, carry):
            r0 = pl.multiple_of(c * KC, KC)
            for g in range(ATT_GROUPS):
                kk = k_ref[0, pl.ds(r0, KC), g * 128:(g + 1) * 128]
                ms = jnp.sum(kk * kk, axis=-1, keepdims=True) * (1.0 / ATT_HEAD_DIM)
                kn_ref[g, pl.ds(r0, KC), :] = (kk * lax.rsqrt(ms + RMS_EPS) * kg_ref[...]).astype(jnp.bfloat16)
            return carry
        lax.fori_loop(0, seq // KC, body, 0)

    for h in range(ATT_HEADS):
        qq = q_ref[0, :, h * 128:(h + 1) * 128]
        ms = jnp.sum(qq * qq, axis=-1, keepdims=True) * (1.0 / ATT_HEAD_DIM)
        qs_ref[h] = (qq * lax.rsqrt(ms + RMS_EPS) * qg_ref[...] * att_scale).astype(jnp.bfloat16)

    rows = j * TQ + lax.broadcasted_iota(jnp.int32, (TQ, 1), 0)
    lane_cols = lax.broadcasted_iota(jnp.int32, (1, KC), 1)

    def idx_body(c, carry):
        r0 = pl.multiple_of(c * KC, KC)
        ikc = ik_ref[0, pl.ds(r0, KC), :]
        acc = jnp.zeros((TQ, KC), jnp.float32)
        for h in range(IDX_HEADS):
            s = lax.dot_general(iq_ref[0, :, h * 128:(h + 1) * 128], ikc, (((1,), (1,)), ((), ())),
                                preferred_element_type=jnp.float32)
            acc = acc + iw_ref[0, :, h:h + 1] * jnp.maximum(s, 0.0)
        score = acc * idx_scale
        cols = c * KC + lane_cols
        score = jnp.where(cols <= rows, score, -jnp.inf)
        keys_ref[c] = _sortable_key(score)
        return carry
    lax.fori_loop(0, n_kc, idx_body, 0)

    def lane_fold(m):
        part = m[:, 0:128]
        for i in range(1, KC // 128):
            part = part + m[:, i * 128:(i + 1) * 128]
        return part

    def count_ge(cand):
        def body(c, acc):
            return acc + lane_fold(jnp.where(keys_ref[c] >= cand, 1.0, 0.0))
        acc = lax.fori_loop(0, n_kc, body, jnp.zeros((TQ, 128), jnp.float32))
        return jnp.sum(acc, axis=1, keepdims=True)

    kf = float(topk)
    ans0 = jnp.where(count_ge(jnp.zeros((TQ, 1), jnp.int32)) >= kf,
                     jnp.int32(0), jnp.int32(_INT_MIN))

    def bit_body(i, ans):
        cand = ans | jnp.left_shift(jnp.int32(1), 30 - i)
        return jnp.where(count_ge(cand) >= kf, cand, ans)
    thr = lax.fori_loop(0, 31, bit_body, ans0)

    n_gt = count_ge(thr + 1)
    n_ge = count_ge(thr)
    need = kf - n_gt
    tie = jnp.logical_and(n_ge - n_gt > need, thr != _FILL_KEY)
    lim_ref[...] = jnp.full((TQ, 1), seq, jnp.int32)

    @pl.when(jnp.max(jnp.where(tie, 1.0, 0.0)) > 0.0)
    def _():
        def count_eq_lt(cand):
            def body(c, acc):
                cols = c * KC + lane_cols
                hit = jnp.logical_and(keys_ref[c] == thr, cols < cand)
                return acc + lane_fold(jnp.where(hit, 1.0, 0.0))
            acc = lax.fori_loop(0, n_kc, body, jnp.zeros((TQ, 128), jnp.float32))
            return jnp.sum(acc, axis=1, keepdims=True)

        nbits = max(1, int(np.ceil(np.log2(seq))))

        def bit2(i, lo):
            cand = lo | jnp.left_shift(jnp.int32(1), nbits - 1 - i)
            return jnp.where(count_eq_lt(cand) < need, cand, lo)
        lo = lax.fori_loop(0, nbits, bit2, jnp.zeros((TQ, 1), jnp.int32))
        lim_ref[...] = jnp.where(tie, lo, jnp.int32(seq))

    lim = lim_ref[...]

    m_ref[...] = jnp.full(m_ref.shape, _NEG_BIG, jnp.float32)
    l_ref[...] = jnp.zeros(l_ref.shape, jnp.float32)
    acc_ref[...] = jnp.zeros(acc_ref.shape, jnp.float32)
    left = lax.broadcasted_iota(jnp.int32, (1, 128), 1) < ATT_HEAD_DIM

    def att_body(c, carry):
        r0 = pl.multiple_of(c * KC, KC)
        cols = c * KC + lane_cols
        key = keys_ref[c]
        sel = jnp.logical_or(key > thr, jnp.logical_and(key == thr, cols <= lim))
        sel = jnp.logical_and(sel, cols <= rows)
        dist = (rows - cols).astype(jnp.float32)
        for g in range(ATT_GROUPS):
            kc = kn_ref[g, pl.ds(r0, KC), :]
            for r in range(ATT_REP):
                h = g * ATT_REP + r
                slope = float(2.0 ** (-8.0 * (h + 1) / ATT_HEADS))
                s = lax.dot_general(qs_ref[h], kc, (((1,), (1,)), ((), ())),
                                    preferred_element_type=jnp.float32)
                sc = jnp.where(sel, s - slope * dist, _NEG_BIG)
                m_old = m_ref[h]
                m_new = jnp.maximum(m_old, jnp.max(sc, axis=1, keepdims=True))
                p = jnp.where(sel, jnp.exp(sc - m_new), 0.0)
                alpha = jnp.exp(m_old - m_new)
                l_ref[h] = alpha * l_ref[h] + jnp.sum(p, axis=1, keepdims=True)
                m_ref[h] = m_new
                vt = (h % 2) * ATT_GROUPS + g
                pv = jnp.dot(p.astype(jnp.bfloat16), v_ref[0, pl.ds(r0, KC), vt * 128:(vt + 1) * 128],
                             preferred_element_type=jnp.float32)
                mine = left if h % 2 == 0 else jnp.logical_not(left)
                acc_ref[h // 2] = acc_ref[h // 2] * jnp.where(mine, alpha, 1.0) + pv
        return carry
    lax.fori_loop(0, n_kc, att_body, 0)

    for pair in range(ATT_HEADS // 2):
        den = jnp.where(left, l_ref[2 * pair], l_ref[2 * pair + 1])
        o_ref[0, :, pair * 128:(pair + 1) * 128] = (acc_ref[pair] / den).astype(o_ref.dtype)


def dsa_attention(q, iq, iw, k, v, ik, qg, kg):
    B, S, _ = q.shape
    TQ, KC = DSA_TQ, DSA_KC
    assert S % KC == 0 and S % TQ == 0
    topk = min(IDX_TOPK_MAX, S // 4)
    kern = functools.partial(_dsa_kernel, seq=S, topk=topk)

    def pad_gain(g):
        return jnp.concatenate([g, jnp.zeros_like(g)]).reshape(1, 128)
    return pl.pallas_call(
        kern,
        grid=(B, S // TQ),
        in_specs=[
            pl.BlockSpec((1, TQ, ATT_HEADS * 128), lambda b, j: (b, j, 0)),
            pl.BlockSpec((1, TQ, IDX_HEADS * 128), lambda b, j: (b, j, 0)),
            pl.BlockSpec((1, TQ, 128), lambda b, j: (b, j, 0)),
            pl.BlockSpec((1, S, ATT_GROUPS * 128), lambda b, j: (b, 0, 0)),
            pl.BlockSpec((1, S, 2 * ATT_GROUPS * 128), lambda b, j: (b, 0, 0)),
            pl.BlockSpec((1, S, 128), lambda b, j: (b, 0, 0)),
            pl.BlockSpec((1, 128), lambda b, j: (0, 0)),
            pl.BlockSpec((1, 128), lambda b, j: (0, 0)),
        ],
        out_specs=pl.BlockSpec((1, TQ, ATT_HEADS * 64), lambda b, j: (b, j, 0)),
        out_shape=jax.ShapeDtypeStruct((B, S, ATT_HEADS * 64), jnp.bfloat16),
        scratch_shapes=[
            pltpu.VMEM((ATT_GROUPS, S, 128), jnp.bfloat16),
            pltpu.VMEM((ATT_HEADS, TQ, 128), jnp.bfloat16),
            pltpu.VMEM((S // KC, TQ, KC), jnp.int32),
            pltpu.VMEM((TQ, 1), jnp.int32),
            pltpu.VMEM((ATT_HEADS, TQ, 1), jnp.float32),
            pltpu.VMEM((ATT_HEADS, TQ, 1), jnp.float32),
            pltpu.VMEM((ATT_HEADS // 2, TQ, 128), jnp.float32),
        ],
        compiler_params=pltpu.CompilerParams(
            dimension_semantics=("arbitrary", "arbitrary"),
            vmem_limit_bytes=48 * 1024 * 1024),
        name="dsa_attention",
    )(q, iq, iw, k, v, ik, pad_gain(qg), pad_gain(kg))


PEER_RT = 256
PEER_ET = 128
_ROW_CHUNKS = D_MODEL // 128
_TAB_ROWS = _ROW_CHUNKS // 2
_EDGES = PEER_HEADS * PEER_TOPK


def _extract_topk_rows(work, n_out, val_ref, payload=None, pay_ref=None):
    nrows = work.shape[0]
    rowi = lax.broadcasted_iota(jnp.int32, work.shape, 0)
    for it in range(n_out):
        m = jnp.max(work, axis=0, keepdims=True)
        am = jnp.min(jnp.where(work == m, rowi, nrows), axis=0, keepdims=True)
        hit = rowi == am
        val_ref[it:it + 1, :] = m
        if payload is None:
            pay_ref[it:it + 1, :] = am
        else:
            pay_ref[it:it + 1, :] = jnp.max(jnp.where(hit, payload, -1), axis=0, keepdims=True)
        work = jnp.where(hit, -jnp.inf, work)


def _peer_route_kernel(q_ref, sk_ref, e_ref, g_ref, s1_ref, i1_ref, s2_ref, i2_ref, ts_ref, te_ref):
    K = PEER_TOPK
    for h in range(PEER_HEADS):
        for p, (s_ref, i_ref) in enumerate(((s1_ref, i1_ref), (s2_ref, i2_ref))):
            col = (h * 2 + p) * 128
            qs = q_ref[:, col:col + 128].astype(jnp.bfloat16)
            sk = sk_ref[h, p].astype(jnp.bfloat16)
            sc = lax.dot_general(sk, qs, (((1,), (1,)), ((), ())),
                                 preferred_element_type=jnp.float32)
            _extract_topk_rows(sc, K, s_ref, None, i_ref)
        s2 = s2_ref[...]
        i2 = i2_ref[...]
        cand = jnp.concatenate([s1_ref[a:a + 1, :] + s2 for a in range(K)], axis=0)
        cidx = jnp.concatenate([(i1_ref[a:a + 1, :] * PEER_NKEYS + i2) * _TAB_ROWS for a in range(K)], axis=0)
        _extract_topk_rows(cand, K, ts_ref, cidx, te_ref)
        ts = ts_ref[...]
        ex = jnp.exp(ts - jnp.max(ts, axis=0, keepdims=True))
        g_ref[h * K:(h + 1) * K, :] = ex / jnp.sum(ex, axis=0, keepdims=True)
        e_ref[h * K:(h + 1) * K, :] = te_ref[...]


def peer_route(qry, subkeys):
    T = qry.shape[0]
    RT = PEER_RT
    assert T % RT == 0
    K = PEER_TOPK
    return pl.pallas_call(
        _peer_route_kernel,
        grid=(T // RT,),
        in_specs=[
            pl.BlockSpec((RT, qry.shape[1]), lambda i: (i, 0)),
            pl.BlockSpec(subkeys.shape, lambda i: (0, 0, 0, 0)),
        ],
        out_specs=[pl.BlockSpec((_EDGES, RT), lambda i: (0, i)),
                   pl.BlockSpec((_EDGES, RT), lambda i: (0, i))],
        out_shape=[jax.ShapeDtypeStruct((_EDGES, T), jnp.int32),
                   jax.ShapeDtypeStruct((_EDGES, T), jnp.float32)],
        scratch_shapes=[
            pltpu.VMEM((K, RT), jnp.float32), pltpu.VMEM((K, RT), jnp.int32),
            pltpu.VMEM((K, RT), jnp.float32), pltpu.VMEM((K, RT), jnp.int32),
            pltpu.VMEM((K, RT), jnp.float32), pltpu.VMEM((K, RT), jnp.int32),
        ],
        compiler_params=pltpu.CompilerParams(dimension_semantics=("arbitrary",)),
        name="peer_route",
    )(qry, subkeys)


def pack_table(tab):
    n = tab.shape[0]
    tb = lax.bitcast_convert_type(tab.astype(jnp.bfloat16), jnp.uint16).astype(jnp.uint32)
    tb = tb.reshape(n, _TAB_ROWS, 2, 128)
    return (tb[:, :, 0, :] | (tb[:, :, 1, :] << 16)).reshape(n * _TAB_ROWS, 128)


def _gather_rows(idx_ref, tab_ref, stage_ref, t):
    R = _TAB_ROWS
    for i in range(_EDGES):
        row = pl.multiple_of(idx_ref[t, i], R)
        stage_ref[R * i:R * i + R, :] = tab_ref[pl.ds(row, R), :]


def _split_bf16(x):
    hi = x.astype(jnp.bfloat16)
    lo = (x - hi.astype(jnp.float32)).astype(jnp.bfloat16)
    return jnp.concatenate([hi, lo], axis=0)


_SLOTS = 8
_AHEAD = 2


def _pipelined_tokens(n_tok, gather, compute):
    for t in range(_AHEAD):
        gather(t, t)

    def body(tt, carry):
        t0 = _SLOTS * tt
        for u in range(_SLOTS):
            gather(jnp.minimum(t0 + u + _AHEAD, n_tok - 1), (u + _AHEAD) % _SLOTS)
            compute(t0 + u, u)
        return carry
    lax.fori_loop(0, n_tok // _SLOTS, body, 0)


def _peer_dot_kernel(idx_ref, tab_ref, x_ref, a_ref, *stages):
    n = _EDGES * _ROW_CHUNKS
    sub = lax.broadcasted_iota(jnp.int32, (2 * _ROW_CHUNKS, n), 0) % _ROW_CHUNKS
    lane = lax.broadcasted_iota(jnp.int32, (2 * _ROW_CHUNKS, n), 1)
    diag = sub == lane % _ROW_CHUNKS

    def gather(t, slot):
        _gather_rows(idx_ref, tab_ref, stages[slot], t)

    def compute(t, slot):
        m = pltpu.bitcast(stages[slot][...], jnp.bfloat16)
        x16 = _split_bf16(x_ref[t])
        r = lax.dot_general(x16, m, (((1,), (1,)), ((), ())),
                            preferred_element_type=jnp.float32)
        tot = jnp.sum(jnp.where(diag, r, 0.0), axis=0, keepdims=True)
        for g in range(n // 128):
            a_ref[t, g:g + 1, :] = tot[:, g * 128:(g + 1) * 128]

    _pipelined_tokens(a_ref.shape[0], gather, compute)


def _group_sum_kernel(p_ref, a_ref):
    a = p_ref[...]
    lane = lax.broadcasted_iota(jnp.int32, a.shape, 1)
    for sh in (1, 2, 4):
        a = a + jnp.where((lane & sh) == 0, pltpu.roll(a, 128 - sh, 1), pltpu.roll(a, sh, 1))
    a_ref[...] = a


def _group_sum(part):
    rows = part.shape[0]
    tr = 4096
    assert rows % tr == 0
    return pl.pallas_call(
        _group_sum_kernel,
        grid=(rows // tr,),
        in_specs=[pl.BlockSpec((tr, 128), lambda i: (i, 0))],
        out_specs=pl.BlockSpec((tr, 128), lambda i: (i, 0)),
        out_shape=jax.ShapeDtypeStruct(part.shape, part.dtype),
        compiler_params=pltpu.CompilerParams(dimension_semantics=("arbitrary",)),
        name="peer_group_sum",
    )(part)


def _peer_mix_kernel(idx_ref, tab_ref, a_ref, g_ref, x1_ref, g2_ref, y_ref, *stages):
    sub = lax.broadcasted_iota(jnp.int32, (_ROW_CHUNKS, 128), 0)
    lane = lax.broadcasted_iota(jnp.int32, (_ROW_CHUNKS, 128), 1)
    diag = sub == lane % _ROW_CHUNKS
    ngroups = _EDGES * _ROW_CHUNKS // 128

    def gather(t, slot):
        _gather_rows(idx_ref, tab_ref, stages[slot], t)

    def compute(t, slot):
        m = pltpu.bitcast(stages[slot][...], jnp.bfloat16)
        a = a_ref[t]
        c = g_ref[t] * (0.5 * a * (1.0 + lax.erf(a * (2.0 ** -0.5))))
        lhs = jnp.concatenate(
            [jnp.where(diag, jnp.broadcast_to(c[g:g + 1, :], (_ROW_CHUNKS, 128)), 0.0) for g in range(ngroups)],
            axis=1)
        r = jnp.dot(_split_bf16(lhs), m, preferred_element_type=jnp.float32)
        y_ref[t] = x1_ref[t] + g2_ref[0] * (r[0:_ROW_CHUNKS] + r[_ROW_CHUNKS:])

    _pipelined_tokens(y_ref.shape[0], gather, compute)


def _expert_call(kern, name, idx, tab, ins, extra=(), extra_specs=()):
    T = idx.shape[0]
    ET = PEER_ET
    assert T % ET == 0
    blk = pl.BlockSpec((ET, _ROW_CHUNKS, 128), lambda i: (i, 0, 0))
    return pl.pallas_call(
        kern,
        grid=(T // ET,),
        in_specs=[
            pl.BlockSpec((ET, _EDGES), lambda i: (i, 0), memory_space=pltpu.SMEM),
            pl.BlockSpec(tab.shape, lambda i: (0, 0), pipeline_mode=pl.Buffered(1)),
        ] + [blk] * len(ins) + list(extra_specs),
        out_specs=blk,
        out_shape=jax.ShapeDtypeStruct((T, _ROW_CHUNKS, 128), jnp.float32),
        scratch_shapes=[pltpu.VMEM((_EDGES * 4, 128), jnp.uint32)] * _SLOTS,
        compiler_params=pltpu.CompilerParams(
            dimension_semantics=("arbitrary",),
            vmem_limit_bytes=44 * 1024 * 1024),
        name=name,
    )(idx, tab, *ins, *extra)


def peer_experts(h2, eidx_t, gates_t, u, vtab, x1, g2, seq):
    T = h2.shape[0]
    ET = PEER_ET
    assert seq % ET == 0
    idx = eidx_t.T
    gexp = jnp.repeat(gates_t.T.reshape(T, _EDGES // 16, 16), _ROW_CHUNKS, axis=-1)

    def chunked(z):
        return z.reshape(z.shape[0], _ROW_CHUNKS, 128)
    part = _expert_call(_peer_dot_kernel, "peer_dot", idx, pack_table(u), [chunked(h2)])
    a = _group_sum(part.reshape(T * _ROW_CHUNKS, 128)).reshape(T, _ROW_CHUNKS, 128)
    g2_spec = pl.BlockSpec((1, _ROW_CHUNKS, 128), lambda i: (i * ET // seq, 0, 0))
    y = _expert_call(_peer_mix_kernel, "peer_mix", idx, pack_table(vtab), [a, gexp, chunked(x1)],
                     extra=[chunked(g2)], extra_specs=[g2_spec])
    return y.reshape(T, D_MODEL)


def kernel(x, c, w_ada, b_ada, norm1_g, norm2_g, w_in, hg_lb, hg_onorm_g, q_norm_g, k_norm_g,
           w_a, w_b, w_out, peer_wq, peer_subkeys, peer_u, peer_v):
    B, S, D = x.shape
    T = B * S
    for l in range(w_ada.shape[0]):
        mod = ada_modulation(c, w_ada[l], b_ada[l])
        sh1, sc1, g1, sh2, sc2, g2 = jnp.split(mod, 6, axis=-1)
        hg, q, k, iw, iq, ik, v, gate = in_projection(x, norm1_g[l], sc1, sh1, pack_w_in(w_in[l]))
        ya = hgrn2(hg, hg_lb, hg_onorm_g[l], l)
        yb = dsa_attention(q, iq, iw, k, v, ik, q_norm_g[l], k_norm_g[l])
        x1, h2, qry = out_projection(ya, yb, gate, x, g1, norm2_g[l], sc2, sh2,
                                     w_a[l], w_b[l], w_out[l], peer_wq[l])
        e_t, g_t = peer_route(qry.reshape(T, -1), peer_subkeys[l])
        x = peer_experts(h2.reshape(T, D), e_t, g_t, peer_u[l], peer_v[l],
                         x1.reshape(T, D), g2, S).reshape(B, S, D)
    return x
```

```python
import functools

import jax
import jax.numpy as jnp
import numpy as np
from jax import lax
from jax.experimental import pallas as pl
from jax.experimental.pallas import tpu as pltpu

D_MODEL = 1024
HG_HEADS = 4
HG_DK = 128
HG_WIDTH = HG_HEADS * HG_DK
HG_CHUNK = 64
ATT_HEADS = 8
ATT_GROUPS = 2
ATT_HEAD_DIM = 64
ATT_REP = ATT_HEADS // ATT_GROUPS
IDX_HEADS = 8
IDX_DIM = 64
IDX_TOPK_MAX = 256
PEER_HEADS = 8
PEER_NKEYS = 128
PEER_TOPK = 16
RMS_EPS = 1e-6
IN_SPLIT = (HG_WIDTH, HG_WIDTH, HG_WIDTH, HG_WIDTH,
            ATT_HEADS * ATT_HEAD_DIM, ATT_GROUPS * ATT_HEAD_DIM, ATT_GROUPS * ATT_HEAD_DIM,
            IDX_HEADS * IDX_DIM, IDX_DIM, IDX_HEADS,
            D_MODEL, D_MODEL)

_VMEM_LIMIT = 48 * 1024 * 1024


def _ada_kernel(c_ref, w_ref, b_ref, o_ref):
    c = c_ref[...]
    cs = c * (1.0 / (1.0 + jnp.exp(-c)))
    o_ref[...] = jnp.dot(cs, w_ref[...], preferred_element_type=jnp.float32,
                         precision=lax.Precision.HIGHEST) + b_ref[...]


def ada_modulation(c, w, b):
    B, D = c.shape
    N = w.shape[1]
    tn = 1536
    assert N % tn == 0
    return pl.pallas_call(
        _ada_kernel,
        grid=(N // tn,),
        in_specs=[pl.BlockSpec((B, D), lambda j: (0, 0)),
                  pl.BlockSpec((D, tn), lambda j: (0, j)),
                  pl.BlockSpec((1, tn), lambda j: (0, j))],
        out_specs=pl.BlockSpec((B, tn), lambda j: (0, j)),
        out_shape=jax.ShapeDtypeStruct((B, N), jnp.float32),
        compiler_params=pltpu.CompilerParams(dimension_semantics=("arbitrary",),
                                             vmem_limit_bytes=_VMEM_LIMIT),
        name="ada_modulation",
    )(c, w, b.reshape(1, N))


_PROJ_SEGS = (("hg", 2048, jnp.float32), ("q", 1024, jnp.float32), ("k", 256, jnp.float32),
              ("iw", 128, jnp.float32), ("iq", 1024, jnp.bfloat16), ("ik", 128, jnp.bfloat16),
              ("v", 128, jnp.bfloat16), ("gate", 2048, jnp.float32))
PROJ_TM = 256


def pack_w_in(w_in):
    D = w_in.shape[0]
    offs = np.cumsum((0,) + IN_SPLIT)
    hq, hf, hi, hg, aq, ak, av, iq, ik, iw, ga, gb = [w_in[:, offs[i]:offs[i + 1]] for i in range(12)]

    def pad_heads(w, nh):
        w = w.reshape(D, nh, 64)
        return jnp.concatenate([w, jnp.zeros_like(w)], axis=-1).reshape(D, nh * 128)

    iwp = jnp.concatenate([iw, jnp.zeros((D, 128 - IDX_HEADS), w_in.dtype)], axis=1)
    cols = [hq, hf, hi, hg, pad_heads(aq, ATT_HEADS), pad_heads(ak, ATT_GROUPS), iwp,
            pad_heads(iq, IDX_HEADS), pad_heads(ik, 1),
            av, ga, gb]
    return jnp.concatenate(cols, axis=1).astype(jnp.bfloat16)


def _proj_kernel(x_ref, g_ref, sc_ref, sh_ref, w_ref, *o_refs):
    x = x_ref[0]
    ms = jnp.mean(x * x, axis=-1, keepdims=True)
    h = x * lax.rsqrt(ms + RMS_EPS) * g_ref[...] * (1.0 + sc_ref[0]) + sh_ref[0]
    hb = h.astype(jnp.bfloat16)
    c0 = 0
    for (_, width, _), o_ref in zip(_PROJ_SEGS, o_refs):
        for s in range(0, width, 512):
            e = min(s + 512, width)
            o_ref[0, :, s:e] = jnp.dot(hb, w_ref[:, c0 + s:c0 + e],
                                       preferred_element_type=jnp.float32).astype(o_ref.dtype)
        c0 += width


def in_projection(x, norm_g, sc, sh, w_packed):
    B, S, D = x.shape
    tm = PROJ_TM
    assert S % tm == 0
    ntot = sum(w for _, w, _ in _PROJ_SEGS)
    assert w_packed.shape == (D, ntot)
    return pl.pallas_call(
        _proj_kernel,
        grid=(B, S // tm),
        in_specs=[
            pl.BlockSpec((1, tm, D), lambda b, i: (b, i, 0)),
            pl.BlockSpec((1, D), lambda b, i: (0, 0)),
            pl.BlockSpec((1, 1, D), lambda b, i: (b, 0, 0)),
            pl.BlockSpec((1, 1, D), lambda b, i: (b, 0, 0)),
            pl.BlockSpec((D, ntot), lambda b, i: (0, 0), pipeline_mode=pl.Buffered(1)),
        ],
        out_specs=[pl.BlockSpec((1, tm, w), lambda b, i: (b, i, 0)) for _, w, _ in _PROJ_SEGS],
        out_shape=[jax.ShapeDtypeStruct((B, S, w), dt) for _, w, dt in _PROJ_SEGS],
        compiler_params=pltpu.CompilerParams(dimension_semantics=("arbitrary", "arbitrary"),
                                             vmem_limit_bytes=_VMEM_LIMIT),
        name="in_projection",
    )(x, norm_g.reshape(1, D), sc.reshape(B, 1, D), sh.reshape(B, 1, D), w_packed)


HG_LT = 512


def _split3_bf16(x):
    a = x.astype(jnp.bfloat16)
    r = x - a.astype(jnp.float32)
    b = r.astype(jnp.bfloat16)
    c = (r - b.astype(jnp.float32)).astype(jnp.bfloat16)
    return a, b, c


def _hgrn_kernel(q_ref, f_ref, i_ref, g_ref, lb_ref, og_ref, o_ref, st_ref, *, layer):
    C = HG_CHUNK
    bf = jnp.bfloat16

    @pl.when(pl.program_id(1) == 0)
    def _():
        st_ref[...] = jnp.zeros(st_ref.shape, jnp.float32)

    lbr = lb_ref[...]
    e = jnp.exp(lbr - jnp.max(lbr, axis=0, keepdims=True))
    sm = e / jnp.sum(e, axis=0, keepdims=True)
    lb_all = jnp.sum(sm[0:layer + 1], axis=0, keepdims=True)

    r_i = lax.broadcasted_iota(jnp.int32, (C, C), 0)
    c_i = lax.broadcasted_iota(jnp.int32, (C, C), 1)
    causal = c_i <= r_i
    tri = jnp.where(causal, 1.0, 0.0).astype(bf)
    og = og_ref[...]

    def chunk(c, carry):
        r0 = pl.multiple_of(c * C, C)
        for h in range(HG_HEADS):
            cs = slice(h * HG_DK, (h + 1) * HG_DK)
            lb = lb_all[:, cs]
            qr = q_ref[0, pl.ds(r0, C), cs]
            q = qr * (1.0 / (1.0 + jnp.exp(-qr)))
            f = lb + (1.0 - lb) * (1.0 / (1.0 + jnp.exp(-f_ref[0, pl.ds(r0, C), cs])))
            k = 1.0 - f
            v = i_ref[0, pl.ds(r0, C), cs]
            l1, l2, l3 = _split3_bf16(jnp.log(f))
            b = (jnp.dot(tri, l1, preferred_element_type=jnp.float32)
                 + jnp.dot(tri, l2, preferred_element_type=jnp.float32)
                 + jnp.dot(tri, l3, preferred_element_type=jnp.float32))
            bm = b[C // 2 - 1:C // 2, :]
            bl = b[C - 1:C, :]
            vb = v.astype(bf)
            a = lax.dot_general((q * jnp.exp(b - bm)).astype(bf), (k * jnp.exp(bm - b)).astype(bf),
                                (((1,), (1,)), ((), ())), preferred_element_type=jnp.float32)
            a = jnp.where(causal, a, 0.0)
            o = jnp.dot(a.astype(bf), vb, preferred_element_type=jnp.float32)
            st = st_ref[h]
            o = o + lax.dot_general((q * jnp.exp(b)).astype(bf), st.astype(bf),
                                    (((1,), (1,)), ((), ())), preferred_element_type=jnp.float32)
            kd = (k * jnp.exp(bl - b)).astype(bf)
            ds = lax.dot_general(vb, kd, (((0,), (0,)), ((), ())), preferred_element_type=jnp.float32)
            st_ref[h] = st * jnp.exp(bl) + ds
            o = o * lax.rsqrt(jnp.mean(o * o, axis=-1, keepdims=True) + RMS_EPS) * og
            gr = g_ref[0, pl.ds(r0, C), cs]
            o_ref[0, pl.ds(r0, C), cs] = (o * (gr * (1.0 / (1.0 + jnp.exp(-gr))))).astype(o_ref.dtype)
        return carry
    lax.fori_loop(0, q_ref.shape[1] // C, chunk, 0)


def hgrn2(hg, hg_lb, onorm_g, layer):
    B, S, _ = hg.shape
    LT = HG_LT
    assert S % LT == 0

    def piece(p):
        return pl.BlockSpec((1, LT, HG_WIDTH), lambda b, i, p=p: (b, i, p))
    return pl.pallas_call(
        functools.partial(_hgrn_kernel, layer=layer),
        grid=(B, S // LT),
        in_specs=[piece(0), piece(1), piece(2), piece(3),
                  pl.BlockSpec(hg_lb.shape, lambda b, i: (0, 0)),
                  pl.BlockSpec((1, HG_DK), lambda b, i: (0, 0))],
        out_specs=pl.BlockSpec((1, LT, HG_WIDTH), lambda b, i: (b, i, 0)),
        out_shape=jax.ShapeDtypeStruct((B, S, HG_WIDTH), jnp.bfloat16),
        scratch_shapes=[pltpu.VMEM((HG_HEADS, HG_DK, HG_DK), jnp.float32)],
        compiler_params=pltpu.CompilerParams(dimension_semantics=("arbitrary", "arbitrary"),
                                             vmem_limit_bytes=_VMEM_LIMIT),
        name="hgrn2",
    )(hg, hg, hg, hg, hg_lb, onorm_g.reshape(1, HG_DK))


OUT_TM = 256


def _out_kernel(ya_ref, yb_ref, ga_ref, gb_ref, x_ref, g1_ref, n2_ref, sc_ref, sh_ref,
                wa_ref, wb_ref, wo_ref, wq_ref, x1_ref, h2_ref, qry_ref):
    bf = jnp.bfloat16

    def sig(z):
        return 1.0 / (1.0 + jnp.exp(-z))
    ma = jnp.dot(ya_ref[0], wa_ref[...], preferred_element_type=jnp.float32)
    mb = jnp.dot(yb_ref[0], wb_ref[...], preferred_element_type=jnp.float32)
    merged = sig(ga_ref[0]) * ma + sig(gb_ref[0]) * mb
    x1 = x_ref[0] + g1_ref[0] * jnp.dot(merged.astype(bf), wo_ref[...], preferred_element_type=jnp.float32)
    x1_ref[0] = x1
    ms = jnp.mean(x1 * x1, axis=-1, keepdims=True)
    h2 = x1 * lax.rsqrt(ms + RMS_EPS) * n2_ref[...] * (1.0 + sc_ref[0]) + sh_ref[0]
    h2_ref[0] = h2
    qry_ref[0] = jnp.dot(h2.astype(bf), wq_ref[...], preferred_element_type=jnp.float32)


def out_projection(ya, yb, gate, x, g1, norm2_g, sc2, sh2, w_a, w_b, w_out, wq):
    B, S, D = x.shape
    tm = OUT_TM
    bf = jnp.bfloat16
    NQ = wq.shape[1]

    def tok(width, col=0):
        return pl.BlockSpec((1, tm, width), lambda b, i, col=col: (b, i, col))

    def per_batch():
        return pl.BlockSpec((1, 1, D), lambda b, i: (b, 0, 0))

    def const(shape):
        return pl.BlockSpec(shape, lambda b, i: (0,) * len(shape), pipeline_mode=pl.Buffered(1))
    return pl.pallas_call(
        _out_kernel,
        grid=(B, S // tm),
        in_specs=[tok(HG_WIDTH), tok(512), tok(D, 0), tok(D, 1), tok(D), per_batch(),
                  const((1, D)), per_batch(), per_batch(),
                  const(w_a.shape), const(w_b.shape), const(w_out.shape), const(wq.shape)],
        out_specs=[tok(D), tok(D), tok(NQ)],
        out_shape=[jax.ShapeDtypeStruct((B, S, D), jnp.float32),
                   jax.ShapeDtypeStruct((B, S, D), jnp.float32),
                   jax.ShapeDtypeStruct((B, S, NQ), jnp.float32)],
        compiler_params=pltpu.CompilerParams(dimension_semantics=("arbitrary", "arbitrary"),
                                             vmem_limit_bytes=_VMEM_LIMIT),
        name="out_projection",
    )(ya, yb, gate, gate, x, g1.reshape(B, 1, D), norm2_g.reshape(1, D), sc2.reshape(B, 1, D),
      sh2.reshape(B, 1, D), w_a.astype(bf), w_b.astype(bf), w_out.astype(bf), wq.astype(bf))


DSA_TQ = 128
DSA_KC = 512
DSA_KA = 256
_INT_MIN = -2 ** 31
_FILL_KEY = -2139095041
_NEG_BIG = -1e30


def _sortable_key(x):
    b = lax.bitcast_convert_type(x, jnp.int32)
    return b ^ ((b >> 31) & jnp.int32(0x7FFFFFFF))


def _dsa_kernel(q_ref, iq_ref, iw_ref, k_ref, v_ref, ik_ref, qg_ref, kg_ref, o_ref,
                kn_ref, vt_ref, qs_ref, keys_ref, lim_ref, m_ref, l_ref, acc_ref, sc_ref, *, seq, topk):
    TQ, KC, KA = DSA_TQ, DSA_KC, DSA_KA
    bf = jnp.bfloat16
    j = pl.program_id(1)
    n_kc = (j * TQ + TQ + KC - 1) // KC
    idx_scale = (IDX_DIM ** -0.5) * (IDX_HEADS ** -0.5)
    att_scale = ATT_HEAD_DIM ** -0.5
    nt = (((1,), (1,)), ((), ()))

    lane = lax.broadcasted_iota(jnp.int32, (1, 128), 1)
    krow = lax.broadcasted_iota(jnp.int32, (KC, 1), 0)

    def kfeat(kpos):
        a = (kpos // 64).astype(jnp.float32)
        b = (kpos % 64).astype(jnp.float32)
        return jnp.where(lane == ATT_HEAD_DIM, a, jnp.where(lane == ATT_HEAD_DIM + 1, b, 0.0))

    def qfeat(slope):
        return jnp.where(lane == ATT_HEAD_DIM, 64.0 * slope, jnp.where(lane == ATT_HEAD_DIM + 1, slope, 0.0))

    @pl.when(j == 0)
    def _():
        def body(c, carry):
            r0 = pl.multiple_of(c * KC, KC)
            for g in range(ATT_GROUPS):
                kk = k_ref[0, pl.ds(r0, KC), g * 128:(g + 1) * 128]
                ms = jnp.sum(kk * kk, axis=-1, keepdims=True) * (1.0 / ATT_HEAD_DIM)
                kn = kk * lax.rsqrt(ms + RMS_EPS) * kg_ref[...]
                kn_ref[g, pl.ds(r0, KC), :] = (kn + kfeat(r0 + krow)).astype(bf)
            vt_ref[c] = v_ref[0, pl.ds(r0, KC), :].astype(jnp.float32).T.astype(bf)
            return carry
        lax.fori_loop(0, seq // KC, body, 0)

    for h in range(ATT_HEADS):
        qq = q_ref[0, :, h * 128:(h + 1) * 128]
        ms = jnp.sum(qq * qq, axis=-1, keepdims=True) * (1.0 / ATT_HEAD_DIM)
        slope = float(2.0 ** (-8.0 * (h + 1) / ATT_HEADS))
        qs_ref[h] = (qq * lax.rsqrt(ms + RMS_EPS) * qg_ref[...] * att_scale + qfeat(slope)).astype(bf)
    iwt = iw_ref[0].T

    qpos = j * TQ + lax.broadcasted_iota(jnp.int32, (1, TQ), 1)

    def idx_body(c, carry):
        r0 = pl.multiple_of(c * KC, KC)
        ikc = ik_ref[0, pl.ds(r0, KC), :]
        acc = jnp.zeros((KC, TQ), jnp.float32)
        for h in range(IDX_HEADS):
            s = lax.dot_general(ikc, iq_ref[0, :, h * 128:(h + 1) * 128], nt,
                                preferred_element_type=jnp.float32)
            acc = acc + iwt[h:h + 1, :] * jnp.maximum(s, 0.0)
        score = jnp.where(c * KC + krow <= qpos, acc * idx_scale, -jnp.inf)
        keys_ref[c] = _sortable_key(score)
        return carry
    lax.fori_loop(0, n_kc, idx_body, 0)

    def row_fold(m):
        parts = [m[8 * i:8 * i + 8] for i in range(m.shape[0] // 8)]
        while len(parts) > 1:
            parts = [parts[i] + parts[i + 1] for i in range(0, len(parts), 2)]
        return parts[0]

    def count(pred):
        def body(c, acc):
            return acc + row_fold(jnp.where(pred(c, keys_ref[c]), 1.0, 0.0))
        acc = lax.fori_loop(0, n_kc, body, jnp.zeros((8, TQ), jnp.float32))
        return jnp.sum(acc, axis=0, keepdims=True)

    def count_ge(cand):
        return count(lambda c, key: key >= cand)

    kf = float(topk)
    ans0 = jnp.where(count_ge(jnp.zeros((1, TQ), jnp.int32)) >= kf, jnp.int32(0), jnp.int32(_INT_MIN))

    def bit_body(i, ans):
        cand = ans | jnp.left_shift(jnp.int32(1), 30 - i)
        return jnp.where(count_ge(cand) >= kf, cand, ans)
    thr = lax.fori_loop(0, 31, bit_body, ans0)

    n_gt = count_ge(thr + 1)
    n_ge = count_ge(thr)
    need = kf - n_gt
    tie = jnp.logical_and(n_ge - n_gt > need, thr != _FILL_KEY)
    lim_ref[...] = jnp.full((1, TQ), seq, jnp.int32)

    @pl.when(jnp.max(jnp.where(tie, 1.0, 0.0)) > 0.0)
    def _():
        nbits = max(1, int(np.ceil(np.log2(seq))))

        def bit2(i, lo):
            cand = lo | jnp.left_shift(jnp.int32(1), nbits - 1 - i)
            n_before = count(lambda c, key: jnp.logical_and(key == thr, c * KC + krow < cand))
            return jnp.where(n_before < need, cand, lo)
        lo = lax.fori_loop(0, nbits, bit2, jnp.zeros((1, TQ), jnp.int32))
        lim_ref[...] = jnp.where(tie, lo, jnp.int32(seq))

    lim = lim_ref[...]

    m_ref[...] = jnp.full(m_ref.shape, _NEG_BIG, jnp.float32)
    l_ref[...] = jnp.zeros(l_ref.shape, jnp.float32)
    acc_ref[...] = jnp.zeros(acc_ref.shape, jnp.float32)
    hd = ATT_HEAD_DIM

    def att_body(c, carry):
        vtc = vt_ref[c]
        for half in range(KC // KA):
            h0 = half * KA
            r0 = pl.multiple_of(c * KC, KC) + h0
            kpos = c * KC + h0 + krow[0:KA]
            key = keys_ref[c, h0:h0 + KA, :]
            sel = jnp.logical_or(key > thr, jnp.logical_and(key == thr, kpos <= lim))
            sel = jnp.logical_and(sel, kpos <= qpos)
            madd = jnp.where(sel, 0.0, _NEG_BIG)
            mx = []
            for h in range(ATT_HEADS):
                kc = kn_ref[h // ATT_REP, pl.ds(r0, KA), :]
                sc = lax.dot_general(kc, qs_ref[h], nt, preferred_element_type=jnp.float32) + madd
                sc_ref[h] = sc
                mx.append(jnp.max(sc, axis=0, keepdims=True))
            for h in range(ATT_HEADS):
                g = h // ATT_REP
                m_old = m_ref[h]
                m_new = jnp.maximum(m_old, mx[h])
                p = jnp.exp(sc_ref[h] - m_new)
                alpha = jnp.exp(m_old - m_new)
                l_ref[h] = alpha * l_ref[h] + jnp.sum(p, axis=0, keepdims=True)
                m_ref[h] = m_new
                pv = jnp.dot(vtc[g * hd:(g + 1) * hd, h0:h0 + KA], p.astype(bf),
                             preferred_element_type=jnp.float32)
                rs = slice((h % 2) * hd, (h % 2 + 1) * hd)
                acc_ref[h // 2, rs, :] = acc_ref[h // 2, rs, :] * alpha + pv
        return carry
    lax.fori_loop(0, n_kc, att_body, 0)

    for pair in range(ATT_HEADS // 2):
        den = jnp.concatenate([jnp.broadcast_to(l_ref[2 * pair], (hd, TQ)),
                               jnp.broadcast_to(l_ref[2 * pair + 1], (hd, TQ))], axis=0)
        o_ref[0, :, pair * 128:(pair + 1) * 128] = (acc_ref[pair] / den).T.astype(o_ref.dtype)


def dsa_attention(q, iq, iw, k, v, ik, qg, kg):
    B, S, _ = q.shape
    TQ, KC = DSA_TQ, DSA_KC
    assert S % KC == 0 and S % TQ == 0
    topk = min(IDX_TOPK_MAX, S // 4)
    kern = functools.partial(_dsa_kernel, seq=S, topk=topk)

    def pad_gain(g):
        return jnp.concatenate([g, jnp.zeros_like(g)]).reshape(1, 128)
    return pl.pallas_call(
        kern,
        grid=(B, S // TQ),
        in_specs=[
            pl.BlockSpec((1, TQ, ATT_HEADS * 128), lambda b, j: (b, j, 0)),
            pl.BlockSpec((1, TQ, IDX_HEADS * 128), lambda b, j: (b, j, 0)),
            pl.BlockSpec((1, TQ, 128), lambda b, j: (b, j, 0)),
            pl.BlockSpec((1, S, ATT_GROUPS * 128), lambda b, j: (b, 0, 0)),
            pl.BlockSpec((1, S, ATT_GROUPS * ATT_HEAD_DIM), lambda b, j: (b, 0, 0)),
            pl.BlockSpec((1, S, 128), lambda b, j: (b, 0, 0)),
            pl.BlockSpec((1, 128), lambda b, j: (0, 0)),
            pl.BlockSpec((1, 128), lambda b, j: (0, 0)),
        ],
        out_specs=pl.BlockSpec((1, TQ, ATT_HEADS * ATT_HEAD_DIM), lambda b, j: (b, j, 0)),
        out_shape=jax.ShapeDtypeStruct((B, S, ATT_HEADS * ATT_HEAD_DIM), jnp.bfloat16),
        scratch_shapes=[
            pltpu.VMEM((ATT_GROUPS, S, 128), jnp.bfloat16),
            pltpu.VMEM((S // KC, ATT_GROUPS * ATT_HEAD_DIM, KC), jnp.bfloat16),
            pltpu.VMEM((ATT_HEADS, TQ, 128), jnp.bfloat16),
            pltpu.VMEM((S // KC, KC, TQ), jnp.int32),
            pltpu.VMEM((1, TQ), jnp.int32),
            pltpu.VMEM((ATT_HEADS, 1, TQ), jnp.float32),
            pltpu.VMEM((ATT_HEADS, 1, TQ), jnp.float32),
            pltpu.VMEM((ATT_HEADS // 2, 2 * ATT_HEAD_DIM, TQ), jnp.float32),
            pltpu.VMEM((ATT_HEADS, DSA_KA, TQ), jnp.float32),
        ],
        compiler_params=pltpu.CompilerParams(
            dimension_semantics=("arbitrary", "arbitrary"),
            vmem_limit_bytes=_VMEM_LIMIT),
        name="dsa_attention",
    )(q, iq, iw, k, v, ik, pad_gain(qg), pad_gain(kg))


PEER_RT = 256
PEER_ET = 128
_ROW_CHUNKS = D_MODEL // 128
_TAB_ROWS = _ROW_CHUNKS // 2
_EDGES = PEER_HEADS * PEER_TOPK


def _extract_topk_rows(work, n_out, val_ref, pay_ref, order=None, payload=None):
    if order is None:
        order = lax.broadcasted_iota(jnp.int32, work.shape, 0)
    big = jnp.int32(2 ** 30)
    for it in range(n_out):
        m = jnp.max(work, axis=0, keepdims=True)
        am = jnp.min(jnp.where(work == m, order, big), axis=0, keepdims=True)
        hit = order == am
        val_ref[it:it + 1, :] = m
        if payload is None:
            pay_ref[it:it + 1, :] = am
        else:
            pay_ref[it:it + 1, :] = jnp.max(jnp.where(hit, payload, -1), axis=0, keepdims=True)
        work = jnp.where(hit, -jnp.inf, work)


def _peer_route_kernel(q_ref, sk_ref, e_ref, g_ref, s1_ref, i1_ref, s2_ref, i2_ref, ts_ref, te_ref):
    K = PEER_TOPK
    RT = q_ref.shape[0]
    b8 = lax.broadcasted_iota(jnp.int32, (8, RT), 0)
    b16 = lax.broadcasted_iota(jnp.int32, (K, RT), 0)
    order = jnp.concatenate([b16] + [a * K + b8 for a in range(1, 8)] + [(8 + b8) * K], axis=0)
    for h in range(PEER_HEADS):
        for p, (s_ref, i_ref) in enumerate(((s1_ref, i1_ref), (s2_ref, i2_ref))):
            col = (h * 2 + p) * 128
            qs = q_ref[:, col:col + 128].astype(jnp.bfloat16)
            sk = sk_ref[h, p].astype(jnp.bfloat16)
            sc = lax.dot_general(sk, qs, (((1,), (1,)), ((), ())),
                                 preferred_element_type=jnp.float32)
            _extract_topk_rows(sc, K, s_ref, i_ref)
        s2 = s2_ref[...]
        i2 = i2_ref[...]
        cand = [s1_ref[0:1, :] + s2]
        cidx = [i1_ref[0:1, :] * PEER_NKEYS + i2]
        for a in range(1, 8):
            cand.append(jnp.where(b8 < K // (a + 1), s1_ref[a:a + 1, :] + s2[0:8], -jnp.inf))
            cidx.append(i1_ref[a:a + 1, :] * PEER_NKEYS + i2[0:8])
        cand.append(s1_ref[8:K, :] + s2[0:1])
        cidx.append(i1_ref[8:K, :] * PEER_NKEYS + i2[0:1])
        _extract_topk_rows(jnp.concatenate(cand, axis=0), K, ts_ref, te_ref, order,
                           jnp.concatenate(cidx, axis=0) * _TAB_ROWS)
        ts = ts_ref[...]
        ex = jnp.exp(ts - jnp.max(ts, axis=0, keepdims=True))
        g_ref[h * K:(h + 1) * K, :] = ex / jnp.sum(ex, axis=0, keepdims=True)
        e_ref[h * K:(h + 1) * K, :] = te_ref[...]


def peer_route(qry, subkeys):
    T = qry.shape[0]
    RT = PEER_RT
    assert T % RT == 0
    K = PEER_TOPK
    return pl.pallas_call(
        _peer_route_kernel,
        grid=(T // RT,),
        in_specs=[
            pl.BlockSpec((RT, qry.shape[1]), lambda i: (i, 0)),
            pl.BlockSpec(subkeys.shape, lambda i: (0, 0, 0, 0)),
        ],
        out_specs=[pl.BlockSpec((_EDGES, RT), lambda i: (0, i)),
                   pl.BlockSpec((_EDGES, RT), lambda i: (0, i))],
        out_shape=[jax.ShapeDtypeStruct((_EDGES, T), jnp.int32),
                   jax.ShapeDtypeStruct((_EDGES, T), jnp.float32)],
        scratch_shapes=[
            pltpu.VMEM((K, RT), jnp.float32), pltpu.VMEM((K, RT), jnp.int32),
            pltpu.VMEM((K, RT), jnp.float32), pltpu.VMEM((K, RT), jnp.int32),
            pltpu.VMEM((K, RT), jnp.float32), pltpu.VMEM((K, RT), jnp.int32),
        ],
        compiler_params=pltpu.CompilerParams(dimension_semantics=("arbitrary",)),
        name="peer_route",
    )(qry, subkeys)


def pack_table(tab):
    n = tab.shape[0]
    tb = lax.bitcast_convert_type(tab.astype(jnp.bfloat16), jnp.uint16).astype(jnp.uint32)
    tb = tb.reshape(n, _TAB_ROWS, 2, 128)
    return (tb[:, :, 0, :] | (tb[:, :, 1, :] << 16)).reshape(n * _TAB_ROWS, 128)


def _gather_rows(idx_ref, tab_ref, stage_ref, t):
    R = _TAB_ROWS
    for i in range(_EDGES):
        row = pl.multiple_of(idx_ref[t, i], R)
        stage_ref[R * i:R * i + R, :] = tab_ref[pl.ds(row, R), :]


def _split_bf16(x):
    hi = x.astype(jnp.bfloat16)
    lo = (x - hi.astype(jnp.float32)).astype(jnp.bfloat16)
    return jnp.concatenate([hi, lo], axis=0)


_SLOTS = 8
_AHEAD = 2


def _pipelined_tokens(n_tok, gather, compute):
    for t in range(_AHEAD):
        gather(t, t)

    def body(tt, carry):
        t0 = _SLOTS * tt
        for u in range(_SLOTS):
            gather(jnp.minimum(t0 + u + _AHEAD, n_tok - 1), (u + _AHEAD) % _SLOTS)
            compute(t0 + u, u)
        return carry
    lax.fori_loop(0, n_tok // _SLOTS, body, 0)


def _peer_dot_kernel(idx_ref, tab_ref, x_ref, a_ref, *stages):
    n = _EDGES * _ROW_CHUNKS
    sub = lax.broadcasted_iota(jnp.int32, (2 * _ROW_CHUNKS, n), 0) % _ROW_CHUNKS
    lane = lax.broadcasted_iota(jnp.int32, (2 * _ROW_CHUNKS, n), 1)
    diag = sub == lane % _ROW_CHUNKS

    def gather(t, slot):
        _gather_rows(idx_ref, tab_ref, stages[slot], t)

    def compute(t, slot):
        m = pltpu.bitcast(stages[slot][...], jnp.bfloat16)
        x16 = _split_bf16(x_ref[t])
        r = lax.dot_general(x16, m, (((1,), (1,)), ((), ())),
                            preferred_element_type=jnp.float32)
        tot = jnp.sum(jnp.where(diag, r, 0.0), axis=0, keepdims=True)
        for g in range(n // 128):
            a_ref[t, g:g + 1, :] = tot[:, g * 128:(g + 1) * 128]

    _pipelined_tokens(a_ref.shape[0], gather, compute)


def _group_sum_kernel(p_ref, a_ref):
    a = p_ref[...]
    lane = lax.broadcasted_iota(jnp.int32, a.shape, 1)
    for sh in (1, 2, 4):
        a = a + jnp.where((lane & sh) == 0, pltpu.roll(a, 128 - sh, 1), pltpu.roll(a, sh, 1))
    a_ref[...] = a


def _group_sum(part):
    rows = part.shape[0]
    tr = 4096
    assert rows % tr == 0
    return pl.pallas_call(
        _group_sum_kernel,
        grid=(rows // tr,),
        in_specs=[pl.BlockSpec((tr, 128), lambda i: (i, 0))],
        out_specs=pl.BlockSpec((tr, 128), lambda i: (i, 0)),
        out_shape=jax.ShapeDtypeStruct(part.shape, part.dtype),
        compiler_params=pltpu.CompilerParams(dimension_semantics=("arbitrary",)),
        name="peer_group_sum",
    )(part)


def _peer_mix_kernel(idx_ref, tab_ref, a_ref, g_ref, x1_ref, g2_ref, y_ref, *stages):
    sub = lax.broadcasted_iota(jnp.int32, (_ROW_CHUNKS, 128), 0)
    lane = lax.broadcasted_iota(jnp.int32, (_ROW_CHUNKS, 128), 1)
    diag = sub == lane % _ROW_CHUNKS
    ngroups = _EDGES * _ROW_CHUNKS // 128

    def gather(t, slot):
        _gather_rows(idx_ref, tab_ref, stages[slot], t)

    def compute(t, slot):
        m = pltpu.bitcast(stages[slot][...], jnp.bfloat16)
        a = a_ref[t]
        c = g_ref[t] * (0.5 * a * (1.0 + lax.erf(a * (2.0 ** -0.5))))
        lhs = jnp.concatenate(
            [jnp.where(diag, jnp.broadcast_to(c[g:g + 1, :], (_ROW_CHUNKS, 128)), 0.0) for g in range(ngroups)],
            axis=1)
        r = jnp.dot(_split_bf16(lhs), m, preferred_element_type=jnp.float32)
        y_ref[t] = x1_ref[t] + g2_ref[0] * (r[0:_ROW_CHUNKS] + r[_ROW_CHUNKS:])

    _pipelined_tokens(y_ref.shape[0], gather, compute)


def _expert_call(kern, name, idx, tab, ins, extra=(), extra_specs=()):
    T = idx.shape[0]
    ET = PEER_ET
    assert T % ET == 0
    blk = pl.BlockSpec((ET, _ROW_CHUNKS, 128), lambda i: (i, 0, 0))
    return pl.pallas_call(
        kern,
        grid=(T // ET,),
        in_specs=[
            pl.BlockSpec((ET, _EDGES), lambda i: (i, 0), memory_space=pltpu.SMEM),
            pl.BlockSpec(tab.shape, lambda i: (0, 0), pipeline_mode=pl.Buffered(1)),
        ] + [blk] * len(ins) + list(extra_specs),
        out_specs=blk,
        out_shape=jax.ShapeDtypeStruct((T, _ROW_CHUNKS, 128), jnp.float32),
        scratch_shapes=[pltpu.VMEM((_EDGES * 4, 128), jnp.uint32)] * _SLOTS,
        compiler_params=pltpu.CompilerParams(
            dimension_semantics=("arbitrary",),
            vmem_limit_bytes=44 * 1024 * 1024),
        name=name,
    )(idx, tab, *ins, *extra)


def peer_experts(h2, eidx_t, gates_t, u, vtab, x1, g2, seq):
    T = h2.shape[0]
    ET = PEER_ET
    assert seq % ET == 0
    idx = eidx_t.T
    gexp = jnp.repeat(gates_t.T.reshape(T, _EDGES // 16, 16), _ROW_CHUNKS, axis=-1)

    def chunked(z):
        return z.reshape(z.shape[0], _ROW_CHUNKS, 128)
    part = _expert_call(_peer_dot_kernel, "peer_dot", idx, pack_table(u), [chunked(h2)])
    a = _group_sum(part.reshape(T * _ROW_CHUNKS, 128)).reshape(T, _ROW_CHUNKS, 128)
    g2_spec = pl.BlockSpec((1, _ROW_CHUNKS, 128), lambda i: (i * ET // seq, 0, 0))
    y = _expert_call(_peer_mix_kernel, "peer_mix", idx, pack_table(vtab), [a, gexp, chunked(x1)],
                     extra=[chunked(g2)], extra_specs=[g2_spec])
    return y.reshape(T, D_MODEL)


def kernel(x, c, w_ada, b_ada, norm1_g, norm2_g, w_in, hg_lb, hg_onorm_g, q_norm_g, k_norm_g,
           w_a, w_b, w_out, peer_wq, peer_subkeys, peer_u, peer_v):
    B, S, D = x.shape
    T = B * S
    for l in range(w_ada.shape[0]):
        mod = ada_modulation(c, w_ada[l], b_ada[l])
        sh1, sc1, g1, sh2, sc2, g2 = jnp.split(mod, 6, axis=-1)
        hg, q, k, iw, iq, ik, v, gate = in_projection(x, norm1_g[l], sc1, sh1, pack_w_in(w_in[l]))
        ya = hgrn2(hg, hg_lb, hg_onorm_g[l], l)
        yb = dsa_attention(q, iq, iw, k, v, ik, q_norm_g[l], k_norm_g[l])
        x1, h2, qry = out_projection(ya, yb, gate, x, g1, norm2_g[l], sc2, sh2,
                                     w_a[l], w_b[l], w_out[l], peer_wq[l])
        e_t, g_t = peer_route(qry.reshape(T, -1), peer_subkeys[l])
        x = peer_experts(h2.reshape(T, D), e_t, g_t, peer_u[l], peer_v[l],
                         x1.reshape(T, D), g2, S).reshape(B, S, D)
    return x
```

```python
import functools

import jax
import jax.numpy as jnp
import numpy as np
from jax import lax
from jax.experimental import pallas as pl
from jax.experimental.pallas import tpu as pltpu

D_MODEL = 1024
HG_HEADS = 4
HG_DK = 128
HG_WIDTH = HG_HEADS * HG_DK
HG_CHUNK = 64
ATT_HEADS = 8
ATT_GROUPS = 2
ATT_HEAD_DIM = 64
ATT_REP = ATT_HEADS // ATT_GROUPS
IDX_HEADS = 8
IDX_DIM = 64
IDX_TOPK_MAX = 256
PEER_HEADS = 8
PEER_NKEYS = 128
PEER_TOPK = 16
RMS_EPS = 1e-6
IN_SPLIT = (HG_WIDTH, HG_WIDTH, HG_WIDTH, HG_WIDTH,
            ATT_HEADS * ATT_HEAD_DIM, ATT_GROUPS * ATT_HEAD_DIM, ATT_GROUPS * ATT_HEAD_DIM,
            IDX_HEADS * IDX_DIM, IDX_DIM, IDX_HEADS,
            D_MODEL, D_MODEL)

_VMEM_LIMIT = 48 * 1024 * 1024


def _ada_kernel(c_ref, w_ref, b_ref, o_ref):
    c = c_ref[...]
    cs = c * (1.0 / (1.0 + jnp.exp(-c)))
    o_ref[...] = jnp.dot(cs, w_ref[...], preferred_element_type=jnp.float32,
                         precision=lax.Precision.HIGHEST) + b_ref[...]


def ada_modulation(c, w, b):
    B, D = c.shape
    N = w.shape[1]
    tn = 1536
    assert N % tn == 0
    return pl.pallas_call(
        _ada_kernel,
        grid=(N // tn,),
        in_specs=[pl.BlockSpec((B, D), lambda j: (0, 0)),
                  pl.BlockSpec((D, tn), lambda j: (0, j)),
                  pl.BlockSpec((1, tn), lambda j: (0, j))],
        out_specs=pl.BlockSpec((B, tn), lambda j: (0, j)),
        out_shape=jax.ShapeDtypeStruct((B, N), jnp.float32),
        compiler_params=pltpu.CompilerParams(dimension_semantics=("arbitrary",),
                                             vmem_limit_bytes=_VMEM_LIMIT),
        name="ada_modulation",
    )(c, w, b.reshape(1, N))


_PROJ_SEGS = (("hg", 2048, jnp.float32), ("q", 1024, jnp.float32), ("k", 256, jnp.float32),
              ("iw", 128, jnp.float32), ("iq", 1024, jnp.bfloat16), ("ik", 128, jnp.bfloat16),
              ("v", 128, jnp.bfloat16), ("gate", 2048, jnp.float32))
PROJ_TM = 256


def pack_w_in(w_in):
    D = w_in.shape[0]
    offs = np.cumsum((0,) + IN_SPLIT)
    hq, hf, hi, hg, aq, ak, av, iq, ik, iw, ga, gb = [w_in[:, offs[i]:offs[i + 1]] for i in range(12)]

    def pad_heads(w, nh):
        w = w.reshape(D, nh, 64)
        return jnp.concatenate([w, jnp.zeros_like(w)], axis=-1).reshape(D, nh * 128)

    iwp = jnp.concatenate([iw, jnp.zeros((D, 128 - IDX_HEADS), w_in.dtype)], axis=1)
    cols = [hq, hf, hi, hg, pad_heads(aq, ATT_HEADS), pad_heads(ak, ATT_GROUPS), iwp,
            pad_heads(iq, IDX_HEADS), pad_heads(ik, 1),
            av, ga, gb]
    return jnp.concatenate(cols, axis=1).astype(jnp.bfloat16)


def _proj_kernel(x_ref, g_ref, sc_ref, sh_ref, w_ref, *o_refs):
    x = x_ref[0]
    ms = jnp.mean(x * x, axis=-1, keepdims=True)
    h = x * lax.rsqrt(ms + RMS_EPS) * g_ref[...] * (1.0 + sc_ref[0]) + sh_ref[0]
    hb = h.astype(jnp.bfloat16)
    c0 = 0
    for (_, width, _), o_ref in zip(_PROJ_SEGS, o_refs):
        for s in range(0, width, 512):
            e = min(s + 512, width)
            o_ref[0, :, s:e] = jnp.dot(hb, w_ref[:, c0 + s:c0 + e],
                                       preferred_element_type=jnp.float32).astype(o_ref.dtype)
        c0 += width


def in_projection(x, norm_g, sc, sh, w_packed):
    B, S, D = x.shape
    tm = PROJ_TM
    assert S % tm == 0
    ntot = sum(w for _, w, _ in _PROJ_SEGS)
    assert w_packed.shape == (D, ntot)
    return pl.pallas_call(
        _proj_kernel,
        grid=(B, S // tm),
        in_specs=[
            pl.BlockSpec((1, tm, D), lambda b, i: (b, i, 0)),
            pl.BlockSpec((1, D), lambda b, i: (0, 0)),
            pl.BlockSpec((1, 1, D), lambda b, i: (b, 0, 0)),
            pl.BlockSpec((1, 1, D), lambda b, i: (b, 0, 0)),
            pl.BlockSpec((D, ntot), lambda b, i: (0, 0), pipeline_mode=pl.Buffered(1)),
        ],
        out_specs=[pl.BlockSpec((1, tm, w), lambda b, i: (b, i, 0)) for _, w, _ in _PROJ_SEGS],
        out_shape=[jax.ShapeDtypeStruct((B, S, w), dt) for _, w, dt in _PROJ_SEGS],
        compiler_params=pltpu.CompilerParams(dimension_semantics=("arbitrary", "arbitrary"),
                                             vmem_limit_bytes=_VMEM_LIMIT),
        name="in_projection",
    )(x, norm_g.reshape(1, D), sc.reshape(B, 1, D), sh.reshape(B, 1, D), w_packed)


HG_LT = 512


def _split3_bf16(x):
    a = x.astype(jnp.bfloat16)
    r = x - a.astype(jnp.float32)
    b = r.astype(jnp.bfloat16)
    c = (r - b.astype(jnp.float32)).astype(jnp.bfloat16)
    return a, b, c


def _hgrn_kernel(q_ref, f_ref, i_ref, g_ref, lb_ref, og_ref, o_ref, st_ref, *, layer):
    C = HG_CHUNK
    bf = jnp.bfloat16

    @pl.when(pl.program_id(1) == 0)
    def _():
        st_ref[...] = jnp.zeros(st_ref.shape, jnp.float32)

    lbr = lb_ref[...]
    e = jnp.exp(lbr - jnp.max(lbr, axis=0, keepdims=True))
    sm = e / jnp.sum(e, axis=0, keepdims=True)
    lb_all = jnp.sum(sm[0:layer + 1], axis=0, keepdims=True)

    r_i = lax.broadcasted_iota(jnp.int32, (C, C), 0)
    c_i = lax.broadcasted_iota(jnp.int32, (C, C), 1)
    causal = c_i <= r_i
    tri = jnp.where(causal, 1.0, 0.0).astype(bf)
    og = og_ref[...]

    def chunk(c, carry):
        r0 = pl.multiple_of(c * C, C)
        H = range(HG_HEADS)
        cols = [slice(h * HG_DK, (h + 1) * HG_DK) for h in H]
        dot = functools.partial(jnp.dot, preferred_element_type=jnp.float32)
        nt = (((1,), (1,)), ((), ()))
        tn = (((0,), (0,)), ((), ()))
        qr = [q_ref[0, pl.ds(r0, C), cs] for cs in cols]
        q = [x * (1.0 / (1.0 + jnp.exp(-x))) for x in qr]
        f = [lb_all[:, cs] + (1.0 - lb_all[:, cs]) * (1.0 / (1.0 + jnp.exp(-f_ref[0, pl.ds(r0, C), cs])))
             for cs in cols]
        k = [1.0 - x for x in f]
        vb = [i_ref[0, pl.ds(r0, C), cs].astype(bf) for cs in cols]
        lg = [_split3_bf16(jnp.log(x)) for x in f]
        b = [dot(tri, l1) + dot(tri, l2) + dot(tri, l3) for l1, l2, l3 in lg]
        bm = [x[C // 2 - 1:C // 2, :] for x in b]
        bl = [x[C - 1:C, :] for x in b]
        a = [lax.dot_general((q[h] * jnp.exp(b[h] - bm[h])).astype(bf), (k[h] * jnp.exp(bm[h] - b[h])).astype(bf),
                             nt, preferred_element_type=jnp.float32) for h in H]
        st = [st_ref[h] for h in H]
        o_inter = [lax.dot_general((q[h] * jnp.exp(b[h])).astype(bf), st[h].astype(bf), nt,
                                   preferred_element_type=jnp.float32) for h in H]
        ds = [lax.dot_general(vb[h], (k[h] * jnp.exp(bl[h] - b[h])).astype(bf), tn,
                              preferred_element_type=jnp.float32) for h in H]
        for h in H:
            st_ref[h] = st[h] * jnp.exp(bl[h]) + ds[h]
        o = [dot(jnp.where(causal, a[h], 0.0).astype(bf), vb[h]) + o_inter[h] for h in H]
        for h in H:
            on = o[h] * lax.rsqrt(jnp.mean(o[h] * o[h], axis=-1, keepdims=True) + RMS_EPS) * og
            gr = g_ref[0, pl.ds(r0, C), cols[h]]
            o_ref[0, pl.ds(r0, C), cols[h]] = (on * (gr * (1.0 / (1.0 + jnp.exp(-gr))))).astype(o_ref.dtype)
        return carry
    lax.fori_loop(0, q_ref.shape[1] // C, chunk, 0)


def hgrn2(hg, hg_lb, onorm_g, layer):
    B, S, _ = hg.shape
    LT = HG_LT
    assert S % LT == 0

    def piece(p):
        return pl.BlockSpec((1, LT, HG_WIDTH), lambda b, i, p=p: (b, i, p))
    return pl.pallas_call(
        functools.partial(_hgrn_kernel, layer=layer),
        grid=(B, S // LT),
        in_specs=[piece(0), piece(1), piece(2), piece(3),
                  pl.BlockSpec(hg_lb.shape, lambda b, i: (0, 0)),
                  pl.BlockSpec((1, HG_DK), lambda b, i: (0, 0))],
        out_specs=pl.BlockSpec((1, LT, HG_WIDTH), lambda b, i: (b, i, 0)),
        out_shape=jax.ShapeDtypeStruct((B, S, HG_WIDTH), jnp.bfloat16),
        scratch_shapes=[pltpu.VMEM((HG_HEADS, HG_DK, HG_DK), jnp.float32)],
        compiler_params=pltpu.CompilerParams(dimension_semantics=("arbitrary", "arbitrary"),
                                             vmem_limit_bytes=_VMEM_LIMIT),
        name="hgrn2",
    )(hg, hg, hg, hg, hg_lb, onorm_g.reshape(1, HG_DK))


OUT_TM = 256


def _out_kernel(ya_ref, yb_ref, ga_ref, gb_ref, x_ref, g1_ref, n2_ref, sc_ref, sh_ref,
                wa_ref, wb_ref, wo_ref, wq_ref, x1_ref, h2_ref, qry_ref):
    bf = jnp.bfloat16

    def sig(z):
        return 1.0 / (1.0 + jnp.exp(-z))
    ma = jnp.dot(ya_ref[0], wa_ref[...], preferred_element_type=jnp.float32)
    mb = jnp.dot(yb_ref[0], wb_ref[...], preferred_element_type=jnp.float32)
    merged = sig(ga_ref[0]) * ma + sig(gb_ref[0]) * mb
    x1 = x_ref[0] + g1_ref[0] * jnp.dot(merged.astype(bf), wo_ref[...], preferred_element_type=jnp.float32)
    x1_ref[0] = x1
    ms = jnp.mean(x1 * x1, axis=-1, keepdims=True)
    h2 = x1 * lax.rsqrt(ms + RMS_EPS) * n2_ref[...] * (1.0 + sc_ref[0]) + sh_ref[0]
    h2_ref[0] = h2
    qry_ref[0] = jnp.dot(h2.astype(bf), wq_ref[...], preferred_element_type=jnp.float32)


def out_projection(ya, yb, gate, x, g1, norm2_g, sc2, sh2, w_a, w_b, w_out, wq):
    B, S, D = x.shape
    tm = OUT_TM
    bf = jnp.bfloat16
    NQ = wq.shape[1]

    def tok(width, col=0):
        return pl.BlockSpec((1, tm, width), lambda b, i, col=col: (b, i, col))

    def per_batch():
        return pl.BlockSpec((1, 1, D), lambda b, i: (b, 0, 0))

    def const(shape):
        return pl.BlockSpec(shape, lambda b, i: (0,) * len(shape), pipeline_mode=pl.Buffered(1))
    return pl.pallas_call(
        _out_kernel,
        grid=(B, S // tm),
        in_specs=[tok(HG_WIDTH), tok(512), tok(D, 0), tok(D, 1), tok(D), per_batch(),
                  const((1, D)), per_batch(), per_batch(),
                  const(w_a.shape), const(w_b.shape), const(w_out.shape), const(wq.shape)],
        out_specs=[tok(D), tok(D), tok(NQ)],
        out_shape=[jax.ShapeDtypeStruct((B, S, D), jnp.float32),
                   jax.ShapeDtypeStruct((B, S, D), jnp.float32),
                   jax.ShapeDtypeStruct((B, S, NQ), jnp.float32)],
        compiler_params=pltpu.CompilerParams(dimension_semantics=("arbitrary", "arbitrary"),
                                             vmem_limit_bytes=_VMEM_LIMIT),
        name="out_projection",
    )(ya, yb, gate, gate, x, g1.reshape(B, 1, D), norm2_g.reshape(1, D), sc2.reshape(B, 1, D),
      sh2.reshape(B, 1, D), w_a.astype(bf), w_b.astype(bf), w_out.astype(bf), wq.astype(bf))


DSA_TQ = 128
DSA_KC = 512
DSA_KA = 256
_INT_MIN = -2 ** 31
_FILL_KEY = -2139095041
_NEG_BIG = -1e30


def _sortable_key(x):
    b = lax.bitcast_convert_type(x, jnp.int32)
    return b ^ ((b >> 31) & jnp.int32(0x7FFFFFFF))


def _dsa_kernel(q_ref, iq_ref, iw_ref, k_ref, v_ref, ik_ref, qg_ref, kg_ref, o_ref,
                kn_ref, vt_ref, qs_ref, keys_ref, lim_ref, m_ref, l_ref, acc_ref, sc_ref, *, seq, topk):
    TQ, KC, KA = DSA_TQ, DSA_KC, DSA_KA
    bf = jnp.bfloat16
    j = pl.program_id(1)
    n_kc = (j * TQ + TQ + KC - 1) // KC
    idx_scale = (IDX_DIM ** -0.5) * (IDX_HEADS ** -0.5)
    att_scale = ATT_HEAD_DIM ** -0.5
    nt = (((1,), (1,)), ((), ()))

    lane = lax.broadcasted_iota(jnp.int32, (1, 128), 1)
    krow = lax.broadcasted_iota(jnp.int32, (KC, 1), 0)

    def kfeat(kpos):
        a = (kpos // 64).astype(jnp.float32)
        b = (kpos % 64).astype(jnp.float32)
        return jnp.where(lane == ATT_HEAD_DIM, a, jnp.where(lane == ATT_HEAD_DIM + 1, b, 0.0))

    def qfeat(slope):
        return jnp.where(lane == ATT_HEAD_DIM, 64.0 * slope, jnp.where(lane == ATT_HEAD_DIM + 1, slope, 0.0))

    @pl.when(j == 0)
    def _():
        def body(c, carry):
            r0 = pl.multiple_of(c * KC, KC)
            for g in range(ATT_GROUPS):
                kk = k_ref[0, pl.ds(r0, KC), g * 128:(g + 1) * 128]
                ms = jnp.sum(kk * kk, axis=-1, keepdims=True) * (1.0 / ATT_HEAD_DIM)
                kn = kk * lax.rsqrt(ms + RMS_EPS) * kg_ref[...]
                kn_ref[g, pl.ds(r0, KC), :] = (kn + kfeat(r0 + krow)).astype(bf)
            vt_ref[c] = v_ref[0, pl.ds(r0, KC), :].astype(jnp.float32).T.astype(bf)
            return carry
        lax.fori_loop(0, seq // KC, body, 0)

    for h in range(ATT_HEADS):
        qq = q_ref[0, :, h * 128:(h + 1) * 128]
        ms = jnp.sum(qq * qq, axis=-1, keepdims=True) * (1.0 / ATT_HEAD_DIM)
        slope = float(2.0 ** (-8.0 * (h + 1) / ATT_HEADS))
        qs_ref[h] = (qq * lax.rsqrt(ms + RMS_EPS) * qg_ref[...] * att_scale + qfeat(slope)).astype(bf)
    iwt = iw_ref[0].T

    qpos = j * TQ + lax.broadcasted_iota(jnp.int32, (1, TQ), 1)

    def idx_body(c, carry):
        r0 = pl.multiple_of(c * KC, KC)
        ikc = ik_ref[0, pl.ds(r0, KC), :]
        acc = jnp.zeros((KC, TQ), jnp.float32)
        for h in range(IDX_HEADS):
            s = lax.dot_general(ikc, iq_ref[0, :, h * 128:(h + 1) * 128], nt,
                                preferred_element_type=jnp.float32)
            acc = acc + iwt[h:h + 1, :] * jnp.maximum(s, 0.0)
        score = jnp.where(c * KC + krow <= qpos, acc * idx_scale, -jnp.inf)
        keys_ref[c] = _sortable_key(score)
        return carry
    lax.fori_loop(0, n_kc, idx_body, 0)

    def row_fold(m):
        parts = [m[8 * i:8 * i + 8] for i in range(m.shape[0] // 8)]
        while len(parts) > 1:
            parts = [parts[i] + parts[i + 1] for i in range(0, len(parts), 2)]
        return parts[0]

    def count(pred):
        def body(c, acc):
            return acc + row_fold(jnp.where(pred(c, keys_ref[c]), 1.0, 0.0))
        acc = lax.fori_loop(0, n_kc, body, jnp.zeros((8, TQ), jnp.float32))
        return jnp.sum(acc, axis=0, keepdims=True)

    def count_ge(cand):
        return count(lambda c, key: key >= cand)

    kf = float(topk)
    ans0 = jnp.where(count_ge(jnp.zeros((1, TQ), jnp.int32)) >= kf, jnp.int32(0), jnp.int32(_INT_MIN))

    def bit_body(i, ans):
        cand = ans | jnp.left_shift(jnp.int32(1), 30 - i)
        return jnp.where(count_ge(cand) >= kf, cand, ans)
    thr = lax.fori_loop(0, 31, bit_body, ans0)

    n_gt = count_ge(thr + 1)
    n_ge = count_ge(thr)
    need = kf - n_gt
    tie = jnp.logical_and(n_ge - n_gt > need, thr != _FILL_KEY)
    lim_ref[...] = jnp.full((1, TQ), seq, jnp.int32)

    @pl.when(jnp.max(jnp.where(tie, 1.0, 0.0)) > 0.0)
    def _():
        nbits = max(1, int(np.ceil(np.log2(seq))))

        def bit2(i, lo):
            cand = lo | jnp.left_shift(jnp.int32(1), nbits - 1 - i)
            n_before = count(lambda c, key: jnp.logical_and(key == thr, c * KC + krow < cand))
            return jnp.where(n_before < need, cand, lo)
        lo = lax.fori_loop(0, nbits, bit2, jnp.zeros((1, TQ), jnp.int32))
        lim_ref[...] = jnp.where(tie, lo, jnp.int32(seq))

    lim = lim_ref[...]

    m_ref[...] = jnp.full(m_ref.shape, _NEG_BIG, jnp.float32)
    l_ref[...] = jnp.zeros(l_ref.shape, jnp.float32)
    acc_ref[...] = jnp.zeros(acc_ref.shape, jnp.float32)
    hd = ATT_HEAD_DIM

    def att_body(c, carry):
        vtc = vt_ref[c]
        for half in range(KC // KA):
            h0 = half * KA
            r0 = pl.multiple_of(c * KC, KC) + h0
            kpos = c * KC + h0 + krow[0:KA]
            key = keys_ref[c, h0:h0 + KA, :]
            sel = jnp.logical_or(key > thr, jnp.logical_and(key == thr, kpos <= lim))
            sel = jnp.logical_and(sel, kpos <= qpos)
            madd = jnp.where(sel, 0.0, _NEG_BIG)
            mx = []
            for h in range(ATT_HEADS):
                kc = kn_ref[h // ATT_REP, pl.ds(r0, KA), :]
                sc = lax.dot_general(kc, qs_ref[h], nt, preferred_element_type=jnp.float32) + madd
                sc_ref[h] = sc
                mx.append(jnp.max(sc, axis=0, keepdims=True))
            for h in range(ATT_HEADS):
                g = h // ATT_REP
                m_old = m_ref[h]
                m_new = jnp.maximum(m_old, mx[h])
                p = jnp.exp(sc_ref[h] - m_new)
                alpha = jnp.exp(m_old - m_new)
                l_ref[h] = alpha * l_ref[h] + jnp.sum(p, axis=0, keepdims=True)
                m_ref[h] = m_new
                pv = jnp.dot(vtc[g * hd:(g + 1) * hd, h0:h0 + KA], p.astype(bf),
                             preferred_element_type=jnp.float32)
                rs = slice((h % 2) * hd, (h % 2 + 1) * hd)
                acc_ref[h // 2, rs, :] = acc_ref[h // 2, rs, :] * alpha + pv
        return carry
    lax.fori_loop(0, n_kc, att_body, 0)

    for pair in range(ATT_HEADS // 2):
        den = jnp.concatenate([jnp.broadcast_to(l_ref[2 * pair], (hd, TQ)),
                               jnp.broadcast_to(l_ref[2 * pair + 1], (hd, TQ))], axis=0)
        o_ref[0, :, pair * 128:(pair + 1) * 128] = (acc_ref[pair] / den).T.astype(o_ref.dtype)


def dsa_attention(q, iq, iw, k, v, ik, qg, kg):
    B, S, _ = q.shape
    TQ, KC = DSA_TQ, DSA_KC
    assert S % KC == 0 and S % TQ == 0
    topk = min(IDX_TOPK_MAX, S // 4)
    kern = functools.partial(_dsa_kernel, seq=S, topk=topk)

    def pad_gain(g):
        return jnp.concatenate([g, jnp.zeros_like(g)]).reshape(1, 128)
    return pl.pallas_call(
        kern,
        grid=(B, S // TQ),
        in_specs=[
            pl.BlockSpec((1, TQ, ATT_HEADS * 128), lambda b, j: (b, j, 0)),
            pl.BlockSpec((1, TQ, IDX_HEADS * 128), lambda b, j: (b, j, 0)),
            pl.BlockSpec((1, TQ, 128), lambda b, j: (b, j, 0)),
            pl.BlockSpec((1, S, ATT_GROUPS * 128), lambda b, j: (b, 0, 0)),
            pl.BlockSpec((1, S, ATT_GROUPS * ATT_HEAD_DIM), lambda b, j: (b, 0, 0)),
            pl.BlockSpec((1, S, 128), lambda b, j: (b, 0, 0)),
            pl.BlockSpec((1, 128), lambda b, j: (0, 0)),
            pl.BlockSpec((1, 128), lambda b, j: (0, 0)),
        ],
        out_specs=pl.BlockSpec((1, TQ, ATT_HEADS * ATT_HEAD_DIM), lambda b, j: (b, j, 0)),
        out_shape=jax.ShapeDtypeStruct((B, S, ATT_HEADS * ATT_HEAD_DIM), jnp.bfloat16),
        scratch_shapes=[
            pltpu.VMEM((ATT_GROUPS, S, 128), jnp.bfloat16),
            pltpu.VMEM((S // KC, ATT_GROUPS * ATT_HEAD_DIM, KC), jnp.bfloat16),
            pltpu.VMEM((ATT_HEADS, TQ, 128), jnp.bfloat16),
            pltpu.VMEM((S // KC, KC, TQ), jnp.int32),
            pltpu.VMEM((1, TQ), jnp.int32),
            pltpu.VMEM((ATT_HEADS, 1, TQ), jnp.float32),
            pltpu.VMEM((ATT_HEADS, 1, TQ), jnp.float32),
            pltpu.VMEM((ATT_HEADS // 2, 2 * ATT_HEAD_DIM, TQ), jnp.float32),
            pltpu.VMEM((ATT_HEADS, DSA_KA, TQ), jnp.float32),
        ],
        compiler_params=pltpu.CompilerParams(
            dimension_semantics=("arbitrary", "arbitrary"),
            vmem_limit_bytes=_VMEM_LIMIT),
        name="dsa_attention",
    )(q, iq, iw, k, v, ik, pad_gain(qg), pad_gain(kg))


PEER_RT = 256
PEER_ET = 128
_ROW_CHUNKS = D_MODEL // 128
_TAB_ROWS = _ROW_CHUNKS // 2
_EDGES = PEER_HEADS * PEER_TOPK


def _extract_topk_rows(work, n_out, val_ref, pay_ref, order=None, payload=None):
    if order is None:
        order = lax.broadcasted_iota(jnp.int32, work.shape, 0)
    big = jnp.int32(2 ** 30)
    for it in range(n_out):
        m = jnp.max(work, axis=0, keepdims=True)
        am = jnp.min(jnp.where(work == m, order, big), axis=0, keepdims=True)
        hit = order == am
        val_ref[it:it + 1, :] = m
        if payload is None:
            pay_ref[it:it + 1, :] = am
        else:
            pay_ref[it:it + 1, :] = jnp.max(jnp.where(hit, payload, -1), axis=0, keepdims=True)
        work = jnp.where(hit, -jnp.inf, work)


def _peer_route_kernel(q_ref, sk_ref, e_ref, g_ref, s1_ref, i1_ref, s2_ref, i2_ref, ts_ref, te_ref, et_ref):
    K = PEER_TOPK
    RT = q_ref.shape[0]
    b8 = lax.broadcasted_iota(jnp.int32, (8, RT), 0)
    b16 = lax.broadcasted_iota(jnp.int32, (K, RT), 0)
    order = jnp.concatenate([b16] + [a * K + b8 for a in range(1, 8)] + [(8 + b8) * K], axis=0)
    for h in range(PEER_HEADS):
        for p, (s_ref, i_ref) in enumerate(((s1_ref, i1_ref), (s2_ref, i2_ref))):
            col = (h * 2 + p) * 128
            qs = q_ref[:, col:col + 128].astype(jnp.bfloat16)
            sk = sk_ref[h, p].astype(jnp.bfloat16)
            sc = lax.dot_general(sk, qs, (((1,), (1,)), ((), ())),
                                 preferred_element_type=jnp.float32)
            _extract_topk_rows(sc, K, s_ref, i_ref)
        s2 = s2_ref[...]
        i2 = i2_ref[...]
        cand = [s1_ref[0:1, :] + s2]
        cidx = [i1_ref[0:1, :] * PEER_NKEYS + i2]
        for a in range(1, 8):
            cand.append(jnp.where(b8 < K // (a + 1), s1_ref[a:a + 1, :] + s2[0:8], -jnp.inf))
            cidx.append(i1_ref[a:a + 1, :] * PEER_NKEYS + i2[0:8])
        cand.append(s1_ref[8:K, :] + s2[0:1])
        cidx.append(i1_ref[8:K, :] * PEER_NKEYS + i2[0:1])
        _extract_topk_rows(jnp.concatenate(cand, axis=0), K, ts_ref, te_ref, order,
                           jnp.concatenate(cidx, axis=0) * _TAB_ROWS)
        ts = ts_ref[...]
        ex = jnp.exp(ts - jnp.max(ts, axis=0, keepdims=True))
        gate = ex / jnp.sum(ex, axis=0, keepdims=True)
        rep = jnp.concatenate([jnp.broadcast_to(gate[k:k + 1, :], (_ROW_CHUNKS, RT)) for k in range(K)], axis=0)
        g_ref[:, h * 128:(h + 1) * 128] = rep.T
        et_ref[h * K:(h + 1) * K, :] = te_ref[...]
    e_ref[...] = et_ref[...].T


def peer_route(qry, subkeys):
    T = qry.shape[0]
    RT = PEER_RT
    assert T % RT == 0
    K = PEER_TOPK
    return pl.pallas_call(
        _peer_route_kernel,
        grid=(T // RT,),
        in_specs=[
            pl.BlockSpec((RT, qry.shape[1]), lambda i: (i, 0)),
            pl.BlockSpec(subkeys.shape, lambda i: (0, 0, 0, 0)),
        ],
        out_specs=[pl.BlockSpec((RT, _EDGES), lambda i: (i, 0)),
                   pl.BlockSpec((RT, PEER_HEADS * 128), lambda i: (i, 0))],
        out_shape=[jax.ShapeDtypeStruct((T, _EDGES), jnp.int32),
                   jax.ShapeDtypeStruct((T, PEER_HEADS * 128), jnp.float32)],
        scratch_shapes=[
            pltpu.VMEM((K, RT), jnp.float32), pltpu.VMEM((K, RT), jnp.int32),
            pltpu.VMEM((K, RT), jnp.float32), pltpu.VMEM((K, RT), jnp.int32),
            pltpu.VMEM((K, RT), jnp.float32), pltpu.VMEM((K, RT), jnp.int32),
            pltpu.VMEM((_EDGES, RT), jnp.int32),
        ],
        compiler_params=pltpu.CompilerParams(dimension_semantics=("arbitrary",)),
        name="peer_route",
    )(qry, subkeys)


def pack_table(tab):
    n = tab.shape[0]
    tb = lax.bitcast_convert_type(tab.astype(jnp.bfloat16), jnp.uint16).astype(jnp.uint32)
    tb = tb.reshape(n, _TAB_ROWS, 2, 128)
    return (tb[:, :, 0, :] | (tb[:, :, 1, :] << 16)).reshape(n * _TAB_ROWS, 128)


def _gather_rows(idx_ref, tab_ref, stage_ref, t):
    R = _TAB_ROWS
    for i in range(_EDGES):
        row = pl.multiple_of(idx_ref[t, i], R)
        stage_ref[R * i:R * i + R, :] = tab_ref[pl.ds(row, R), :]


def _split_bf16(x):
    hi = x.astype(jnp.bfloat16)
    lo = (x - hi.astype(jnp.float32)).astype(jnp.bfloat16)
    return jnp.concatenate([hi, lo], axis=0)


_SLOTS = 8
_AHEAD = 2


def _pipelined_tokens(n_tok, gather, compute):
    for t in range(_AHEAD):
        gather(t, t)

    def body(tt, carry):
        t0 = _SLOTS * tt
        for u in range(_SLOTS):
            gather(jnp.minimum(t0 + u + _AHEAD, n_tok - 1), (u + _AHEAD) % _SLOTS)
            compute(t0 + u, u)
        return carry
    lax.fori_loop(0, n_tok // _SLOTS, body, 0)


def _peer_dot_kernel(idx_ref, tab_ref, x_ref, a_ref, *stages):
    n = _EDGES * _ROW_CHUNKS
    sub = lax.broadcasted_iota(jnp.int32, (2 * _ROW_CHUNKS, n), 0) % _ROW_CHUNKS
    lane = lax.broadcasted_iota(jnp.int32, (2 * _ROW_CHUNKS, n), 1)
    diag = sub == lane % _ROW_CHUNKS

    def gather(t, slot):
        _gather_rows(idx_ref, tab_ref, stages[slot], t)

    def compute(t, slot):
        m = pltpu.bitcast(stages[slot][...], jnp.bfloat16)
        x16 = _split_bf16(x_ref[t])
        r = lax.dot_general(x16, m, (((1,), (1,)), ((), ())),
                            preferred_element_type=jnp.float32)
        tot = jnp.sum(jnp.where(diag, r, 0.0), axis=0, keepdims=True)
        for g in range(n // 128):
            a_ref[t, g:g + 1, :] = tot[:, g * 128:(g + 1) * 128]

    _pipelined_tokens(a_ref.shape[0], gather, compute)


def _group_sum_kernel(p_ref, a_ref):
    a = p_ref[...]
    lane = lax.broadcasted_iota(jnp.int32, a.shape, 1)
    for sh in (1, 2, 4):
        a = a + jnp.where((lane & sh) == 0, pltpu.roll(a, 128 - sh, 1), pltpu.roll(a, sh, 1))
    a_ref[...] = a


def _group_sum(part):
    rows = part.shape[0]
    tr = 4096
    assert rows % tr == 0
    return pl.pallas_call(
        _group_sum_kernel,
        grid=(rows // tr,),
        in_specs=[pl.BlockSpec((tr, 128), lambda i: (i, 0))],
        out_specs=pl.BlockSpec((tr, 128), lambda i: (i, 0)),
        out_shape=jax.ShapeDtypeStruct(part.shape, part.dtype),
        compiler_params=pltpu.CompilerParams(dimension_semantics=("arbitrary",)),
        name="peer_group_sum",
    )(part)


def _peer_mix_kernel(idx_ref, tab_ref, a_ref, g_ref, x1_ref, g2_ref, y_ref, *stages):
    sub = lax.broadcasted_iota(jnp.int32, (_ROW_CHUNKS, 128), 0)
    lane = lax.broadcasted_iota(jnp.int32, (_ROW_CHUNKS, 128), 1)
    diag = sub == lane % _ROW_CHUNKS
    ngroups = _EDGES * _ROW_CHUNKS // 128

    def gather(t, slot):
        _gather_rows(idx_ref, tab_ref, stages[slot], t)

    def compute(t, slot):
        m = pltpu.bitcast(stages[slot][...], jnp.bfloat16)
        a = a_ref[t]
        c = g_ref[t] * (0.5 * a * (1.0 + lax.erf(a * (2.0 ** -0.5))))
        lhs = jnp.concatenate(
            [jnp.where(diag, jnp.broadcast_to(c[g:g + 1, :], (_ROW_CHUNKS, 128)), 0.0) for g in range(ngroups)],
            axis=1)
        r = jnp.dot(_split_bf16(lhs), m, preferred_element_type=jnp.float32)
        y_ref[t] = x1_ref[t] + g2_ref[0] * (r[0:_ROW_CHUNKS] + r[_ROW_CHUNKS:])

    _pipelined_tokens(y_ref.shape[0], gather, compute)


def _expert_call(kern, name, idx, tab, ins, extra=(), extra_specs=()):
    T = idx.shape[0]
    ET = PEER_ET
    assert T % ET == 0
    blk = pl.BlockSpec((ET, _ROW_CHUNKS, 128), lambda i: (i, 0, 0))
    return pl.pallas_call(
        kern,
        grid=(T // ET,),
        in_specs=[
            pl.BlockSpec((ET, _EDGES), lambda i: (i, 0), memory_space=pltpu.SMEM),
            pl.BlockSpec(tab.shape, lambda i: (0, 0), pipeline_mode=pl.Buffered(1)),
        ] + [blk] * len(ins) + list(extra_specs),
        out_specs=blk,
        out_shape=jax.ShapeDtypeStruct((T, _ROW_CHUNKS, 128), jnp.float32),
        scratch_shapes=[pltpu.VMEM((_EDGES * 4, 128), jnp.uint32)] * _SLOTS,
        compiler_params=pltpu.CompilerParams(
            dimension_semantics=("arbitrary",),
            vmem_limit_bytes=44 * 1024 * 1024),
        name=name,
    )(idx, tab, *ins, *extra)


def peer_experts(h2, idx, gexp, u, vtab, x1, g2, seq):
    T = h2.shape[0]
    ET = PEER_ET
    assert seq % ET == 0

    def chunked(z):
        return z.reshape(z.shape[0], _ROW_CHUNKS, 128)
    part = _expert_call(_peer_dot_kernel, "peer_dot", idx, pack_table(u), [chunked(h2)])
    a = _group_sum(part.reshape(T * _ROW_CHUNKS, 128)).reshape(T, _ROW_CHUNKS, 128)
    g2_spec = pl.BlockSpec((1, _ROW_CHUNKS, 128), lambda i: (i * ET // seq, 0, 0))
    y = _expert_call(_peer_mix_kernel, "peer_mix", idx, pack_table(vtab), [a, chunked(gexp), chunked(x1)],
                     extra=[chunked(g2)], extra_specs=[g2_spec])
    return y.reshape(T, D_MODEL)


def kernel(x, c, w_ada, b_ada, norm1_g, norm2_g, w_in, hg_lb, hg_onorm_g, q_norm_g, k_norm_g,
           w_a, w_b, w_out, peer_wq, peer_subkeys, peer_u, peer_v):
    B, S, D = x.shape
    T = B * S
    for l in range(w_ada.shape[0]):
        mod = ada_modulation(c, w_ada[l], b_ada[l])
        sh1, sc1, g1, sh2, sc2, g2 = jnp.split(mod, 6, axis=-1)
        hg, q, k, iw, iq, ik, v, gate = in_projection(x, norm1_g[l], sc1, sh1, pack_w_in(w_in[l]))
        ya = hgrn2(hg, hg_lb, hg_onorm_g[l], l)
        yb = dsa_attention(q, iq, iw, k, v, ik, q_norm_g[l], k_norm_g[l])
        x1, h2, qry = out_projection(ya, yb, gate, x, g1, norm2_g[l], sc2, sh2,
                                     w_a[l], w_b[l], w_out[l], peer_wq[l])
        idx, gexp = peer_route(qry.reshape(T, -1), peer_subkeys[l])
        x = peer_experts(h2.reshape(T, D), idx, gexp, peer_u[l], peer_v[l],
                         x1.reshape(T, D), g2, S).reshape(B, S, D)
    return x
```

```python
import functools

import jax
import jax.numpy as jnp
import numpy as np
from jax import lax
from jax.experimental import pallas as pl
from jax.experimental.pallas import tpu as pltpu

D_MODEL = 1024
HG_HEADS = 4
HG_DK = 128
HG_WIDTH = HG_HEADS * HG_DK
HG_CHUNK = 64
ATT_HEADS = 8
ATT_GROUPS = 2
ATT_HEAD_DIM = 64
ATT_REP = ATT_HEADS // ATT_GROUPS
IDX_HEADS = 8
IDX_DIM = 64
IDX_TOPK_MAX = 256
PEER_HEADS = 8
PEER_NKEYS = 128
PEER_TOPK = 16
RMS_EPS = 1e-6
IN_SPLIT = (HG_WIDTH, HG_WIDTH, HG_WIDTH, HG_WIDTH,
            ATT_HEADS * ATT_HEAD_DIM, ATT_GROUPS * ATT_HEAD_DIM, ATT_GROUPS * ATT_HEAD_DIM,
            IDX_HEADS * IDX_DIM, IDX_DIM, IDX_HEADS,
            D_MODEL, D_MODEL)

_VMEM_LIMIT = 48 * 1024 * 1024


def _ada_kernel(c_ref, w_ref, b_ref, o_ref):
    c = c_ref[...]
    cs = c * (1.0 / (1.0 + jnp.exp(-c)))
    o_ref[...] = jnp.dot(cs, w_ref[...], preferred_element_type=jnp.float32,
                         precision=lax.Precision.HIGHEST) + b_ref[...]


def ada_modulation(c, w, b):
    B, D = c.shape
    N = w.shape[1]
    tn = 1536
    assert N % tn == 0
    return pl.pallas_call(
        _ada_kernel,
        grid=(N // tn,),
        in_specs=[pl.BlockSpec((B, D), lambda j: (0, 0)),
                  pl.BlockSpec((D, tn), lambda j: (0, j)),
                  pl.BlockSpec((1, tn), lambda j: (0, j))],
        out_specs=pl.BlockSpec((B, tn), lambda j: (0, j)),
        out_shape=jax.ShapeDtypeStruct((B, N), jnp.float32),
        compiler_params=pltpu.CompilerParams(dimension_semantics=("arbitrary",),
                                             vmem_limit_bytes=_VMEM_LIMIT),
        name="ada_modulation",
    )(c, w, b.reshape(1, N))


_PROJ_SEGS = (("hg", 2048, jnp.float32), ("q", 1024, jnp.float32), ("k", 256, jnp.float32),
              ("iw", 128, jnp.float32), ("iq", 1024, jnp.bfloat16), ("ik", 128, jnp.bfloat16),
              ("v", 128, jnp.bfloat16), ("gate", 2048, jnp.float32))
PROJ_TM = 256


def pack_w_in(w_in):
    D = w_in.shape[0]
    offs = np.cumsum((0,) + IN_SPLIT)
    hq, hf, hi, hg, aq, ak, av, iq, ik, iw, ga, gb = [w_in[:, offs[i]:offs[i + 1]] for i in range(12)]

    def pad_heads(w, nh):
        w = w.reshape(D, nh, 64)
        return jnp.concatenate([w, jnp.zeros_like(w)], axis=-1).reshape(D, nh * 128)

    iwp = jnp.concatenate([iw, jnp.zeros((D, 128 - IDX_HEADS), w_in.dtype)], axis=1)
    cols = [hq, hf, hi, hg, pad_heads(aq, ATT_HEADS), pad_heads(ak, ATT_GROUPS), iwp,
            pad_heads(iq, IDX_HEADS), pad_heads(ik, 1),
            av, ga, gb]
    return jnp.concatenate(cols, axis=1).astype(jnp.bfloat16)


def _proj_kernel(x_ref, g_ref, sc_ref, sh_ref, w_ref, *o_refs):
    x = x_ref[0]
    ms = jnp.mean(x * x, axis=-1, keepdims=True)
    h = x * lax.rsqrt(ms + RMS_EPS) * g_ref[...] * (1.0 + sc_ref[0]) + sh_ref[0]
    hb = h.astype(jnp.bfloat16)
    c0 = 0
    for (_, width, _), o_ref in zip(_PROJ_SEGS, o_refs):
        for s in range(0, width, 512):
            e = min(s + 512, width)
            o_ref[0, :, s:e] = jnp.dot(hb, w_ref[:, c0 + s:c0 + e],
                                       preferred_element_type=jnp.float32).astype(o_ref.dtype)
        c0 += width


def in_projection(x, norm_g, sc, sh, w_packed):
    B, S, D = x.shape
    tm = PROJ_TM
    assert S % tm == 0
    ntot = sum(w for _, w, _ in _PROJ_SEGS)
    assert w_packed.shape == (D, ntot)
    return pl.pallas_call(
        _proj_kernel,
        grid=(B, S // tm),
        in_specs=[
            pl.BlockSpec((1, tm, D), lambda b, i: (b, i, 0)),
            pl.BlockSpec((1, D), lambda b, i: (0, 0)),
            pl.BlockSpec((1, 1, D), lambda b, i: (b, 0, 0)),
            pl.BlockSpec((1, 1, D), lambda b, i: (b, 0, 0)),
            pl.BlockSpec((D, ntot), lambda b, i: (0, 0), pipeline_mode=pl.Buffered(1)),
        ],
        out_specs=[pl.BlockSpec((1, tm, w), lambda b, i: (b, i, 0)) for _, w, _ in _PROJ_SEGS],
        out_shape=[jax.ShapeDtypeStruct((B, S, w), dt) for _, w, dt in _PROJ_SEGS],
        compiler_params=pltpu.CompilerParams(dimension_semantics=("arbitrary", "arbitrary"),
                                             vmem_limit_bytes=_VMEM_LIMIT),
        name="in_projection",
    )(x, norm_g.reshape(1, D), sc.reshape(B, 1, D), sh.reshape(B, 1, D), w_packed)


HG_LT = 512


def _split3_bf16(x):
    a = x.astype(jnp.bfloat16)
    r = x - a.astype(jnp.float32)
    b = r.astype(jnp.bfloat16)
    c = (r - b.astype(jnp.float32)).astype(jnp.bfloat16)
    return a, b, c


def _hgrn_kernel(q_ref, f_ref, i_ref, g_ref, lb_ref, og_ref, o_ref, st_ref, *, layer):
    C = HG_CHUNK
    bf = jnp.bfloat16

    @pl.when(pl.program_id(1) == 0)
    def _():
        st_ref[...] = jnp.zeros(st_ref.shape, jnp.float32)

    lbr = lb_ref[...]
    e = jnp.exp(lbr - jnp.max(lbr, axis=0, keepdims=True))
    sm = e / jnp.sum(e, axis=0, keepdims=True)
    lb_all = jnp.sum(sm[0:layer + 1], axis=0, keepdims=True)

    r_i = lax.broadcasted_iota(jnp.int32, (C, C), 0)
    c_i = lax.broadcasted_iota(jnp.int32, (C, C), 1)
    causal = c_i <= r_i
    tri = jnp.where(causal, 1.0, 0.0).astype(bf)
    og = og_ref[...]

    def chunk(c, carry):
        r0 = pl.multiple_of(c * C, C)
        H = range(HG_HEADS)
        cols = [slice(h * HG_DK, (h + 1) * HG_DK) for h in H]
        dot = functools.partial(jnp.dot, preferred_element_type=jnp.float32)
        nt = (((1,), (1,)), ((), ()))
        tn = (((0,), (0,)), ((), ()))
        qr = [q_ref[0, pl.ds(r0, C), cs] for cs in cols]
        q = [x * (1.0 / (1.0 + jnp.exp(-x))) for x in qr]
        f = [lb_all[:, cs] + (1.0 - lb_all[:, cs]) * (1.0 / (1.0 + jnp.exp(-f_ref[0, pl.ds(r0, C), cs])))
             for cs in cols]
        k = [1.0 - x for x in f]
        vb = [i_ref[0, pl.ds(r0, C), cs].astype(bf) for cs in cols]
        lg = [_split3_bf16(jnp.log(x)) for x in f]
        b = [dot(tri, l1) + dot(tri, l2) + dot(tri, l3) for l1, l2, l3 in lg]
        bm = [x[C // 2 - 1:C // 2, :] for x in b]
        bl = [x[C - 1:C, :] for x in b]
        a = [lax.dot_general((q[h] * jnp.exp(b[h] - bm[h])).astype(bf), (k[h] * jnp.exp(bm[h] - b[h])).astype(bf),
                             nt, preferred_element_type=jnp.float32) for h in H]
        st = [st_ref[h] for h in H]
        o_inter = [lax.dot_general((q[h] * jnp.exp(b[h])).astype(bf), st[h].astype(bf), nt,
                                   preferred_element_type=jnp.float32) for h in H]
        ds = [lax.dot_general(vb[h], (k[h] * jnp.exp(bl[h] - b[h])).astype(bf), tn,
                              preferred_element_type=jnp.float32) for h in H]
        for h in H:
            st_ref[h] = st[h] * jnp.exp(bl[h]) + ds[h]
        o = [dot(jnp.where(causal, a[h], 0.0).astype(bf), vb[h]) + o_inter[h] for h in H]
        for h in H:
            on = o[h] * lax.rsqrt(jnp.mean(o[h] * o[h], axis=-1, keepdims=True) + RMS_EPS) * og
            gr = g_ref[0, pl.ds(r0, C), cols[h]]
            o_ref[0, pl.ds(r0, C), cols[h]] = (on * (gr * (1.0 / (1.0 + jnp.exp(-gr))))).astype(o_ref.dtype)
        return carry
    lax.fori_loop(0, q_ref.shape[1] // C, chunk, 0)


def hgrn2(hg, hg_lb, onorm_g, layer):
    B, S, _ = hg.shape
    LT = HG_LT
    assert S % LT == 0

    def piece(p):
        return pl.BlockSpec((1, LT, HG_WIDTH), lambda b, i, p=p: (b, i, p))
    return pl.pallas_call(
        functools.partial(_hgrn_kernel, layer=layer),
        grid=(B, S // LT),
        in_specs=[piece(0), piece(1), piece(2), piece(3),
                  pl.BlockSpec(hg_lb.shape, lambda b, i: (0, 0)),
                  pl.BlockSpec((1, HG_DK), lambda b, i: (0, 0))],
        out_specs=pl.BlockSpec((1, LT, HG_WIDTH), lambda b, i: (b, i, 0)),
        out_shape=jax.ShapeDtypeStruct((B, S, HG_WIDTH), jnp.bfloat16),
        scratch_shapes=[pltpu.VMEM((HG_HEADS, HG_DK, HG_DK), jnp.float32)],
        compiler_params=pltpu.CompilerParams(dimension_semantics=("arbitrary", "arbitrary"),
                                             vmem_limit_bytes=_VMEM_LIMIT),
        name="hgrn2",
    )(hg, hg, hg, hg, hg_lb, onorm_g.reshape(1, HG_DK))


OUT_TM = 256


def _out_kernel(ya_ref, yb_ref, ga_ref, gb_ref, x_ref, g1_ref, n2_ref, sc_ref, sh_ref,
                wa_ref, wb_ref, wo_ref, wq_ref, x1_ref, h2_ref, qry_ref):
    bf = jnp.bfloat16

    def sig(z):
        return 1.0 / (1.0 + jnp.exp(-z))
    ma = jnp.dot(ya_ref[0], wa_ref[...], preferred_element_type=jnp.float32)
    mb = jnp.dot(yb_ref[0], wb_ref[...], preferred_element_type=jnp.float32)
    merged = sig(ga_ref[0]) * ma + sig(gb_ref[0]) * mb
    x1 = x_ref[0] + g1_ref[0] * jnp.dot(merged.astype(bf), wo_ref[...], preferred_element_type=jnp.float32)
    tm, d = x1.shape
    x1_ref[0] = x1.reshape(tm, d // 128, 128)
    ms = jnp.mean(x1 * x1, axis=-1, keepdims=True)
    h2 = x1 * lax.rsqrt(ms + RMS_EPS) * n2_ref[...] * (1.0 + sc_ref[0]) + sh_ref[0]
    h2_ref[0] = h2.reshape(tm, d // 128, 128)
    qry_ref[0] = jnp.dot(h2.astype(bf), wq_ref[...], preferred_element_type=jnp.float32)


def out_projection(ya, yb, gate, x, g1, norm2_g, sc2, sh2, w_a, w_b, w_out, wq):
    B, S, D = x.shape
    tm = OUT_TM
    bf = jnp.bfloat16
    NQ = wq.shape[1]

    def tok(width, col=0):
        return pl.BlockSpec((1, tm, width), lambda b, i, col=col: (b, i, col))

    def chunked_tok():
        return pl.BlockSpec((1, tm, D // 128, 128), lambda b, i: (b, i, 0, 0))

    def per_batch():
        return pl.BlockSpec((1, 1, D), lambda b, i: (b, 0, 0))

    def const(shape):
        return pl.BlockSpec(shape, lambda b, i: (0,) * len(shape), pipeline_mode=pl.Buffered(1))
    return pl.pallas_call(
        _out_kernel,
        grid=(B, S // tm),
        in_specs=[tok(HG_WIDTH), tok(512), tok(D, 0), tok(D, 1), tok(D), per_batch(),
                  const((1, D)), per_batch(), per_batch(),
                  const(w_a.shape), const(w_b.shape), const(w_out.shape), const(wq.shape)],
        out_specs=[chunked_tok(), chunked_tok(), tok(NQ)],
        out_shape=[jax.ShapeDtypeStruct((B, S, D // 128, 128), jnp.float32),
                   jax.ShapeDtypeStruct((B, S, D // 128, 128), jnp.float32),
                   jax.ShapeDtypeStruct((B, S, NQ), jnp.float32)],
        compiler_params=pltpu.CompilerParams(dimension_semantics=("arbitrary", "arbitrary"),
                                             vmem_limit_bytes=_VMEM_LIMIT),
        name="out_projection",
    )(ya, yb, gate, gate, x, g1.reshape(B, 1, D), norm2_g.reshape(1, D), sc2.reshape(B, 1, D),
      sh2.reshape(B, 1, D), w_a.astype(bf), w_b.astype(bf), w_out.astype(bf), wq.astype(bf))


DSA_TQ = 128
DSA_KC = 512
DSA_KA = 256
_INT_MIN = -2 ** 31
_FILL_KEY = -2139095041
_NEG_BIG = -1e30


def _sortable_key(x):
    b = lax.bitcast_convert_type(x, jnp.int32)
    return b ^ ((b >> 31) & jnp.int32(0x7FFFFFFF))


def _dsa_kernel(q_ref, iq_ref, iw_ref, k_ref, v_ref, ik_ref, qg_ref, kg_ref, o_ref,
                kn_ref, vt_ref, qs_ref, keys_ref, lim_ref, m_ref, l_ref, acc_ref, sc_ref, *, seq, topk):
    TQ, KC, KA = DSA_TQ, DSA_KC, DSA_KA
    bf = jnp.bfloat16
    j = pl.program_id(1)
    n_kc = (j * TQ + TQ + KC - 1) // KC
    idx_scale = (IDX_DIM ** -0.5) * (IDX_HEADS ** -0.5)
    att_scale = ATT_HEAD_DIM ** -0.5
    nt = (((1,), (1,)), ((), ()))

    lane = lax.broadcasted_iota(jnp.int32, (1, 128), 1)
    krow = lax.broadcasted_iota(jnp.int32, (KC, 1), 0)

    def kfeat(kpos):
        a = (kpos // 64).astype(jnp.float32)
        b = (kpos % 64).astype(jnp.float32)
        return jnp.where(lane == ATT_HEAD_DIM, a, jnp.where(lane == ATT_HEAD_DIM + 1, b, 0.0))

    def qfeat(slope):
        return jnp.where(lane == ATT_HEAD_DIM, 64.0 * slope, jnp.where(lane == ATT_HEAD_DIM + 1, slope, 0.0))

    @pl.when(j == 0)
    def _():
        def body(c, carry):
            r0 = pl.multiple_of(c * KC, KC)
            for g in range(ATT_GROUPS):
                kk = k_ref[0, pl.ds(r0, KC), g * 128:(g + 1) * 128]
                ms = jnp.sum(kk * kk, axis=-1, keepdims=True) * (1.0 / ATT_HEAD_DIM)
                kn = kk * lax.rsqrt(ms + RMS_EPS) * kg_ref[...]
                kn_ref[g, pl.ds(r0, KC), :] = (kn + kfeat(r0 + krow)).astype(bf)
            vt_ref[c] = v_ref[0, pl.ds(r0, KC), :].astype(jnp.float32).T.astype(bf)
            return carry
        lax.fori_loop(0, seq // KC, body, 0)

    for h in range(ATT_HEADS):
        qq = q_ref[0, :, h * 128:(h + 1) * 128]
        ms = jnp.sum(qq * qq, axis=-1, keepdims=True) * (1.0 / ATT_HEAD_DIM)
        slope = float(2.0 ** (-8.0 * (h + 1) / ATT_HEADS))
        qs_ref[h] = (qq * lax.rsqrt(ms + RMS_EPS) * qg_ref[...] * att_scale + qfeat(slope)).astype(bf)
    iwt = iw_ref[0].T

    qpos = j * TQ + lax.broadcasted_iota(jnp.int32, (1, TQ), 1)

    def idx_body(c, carry):
        r0 = pl.multiple_of(c * KC, KC)
        ikc = ik_ref[0, pl.ds(r0, KC), :]
        acc = jnp.zeros((KC, TQ), jnp.float32)
        for h in range(IDX_HEADS):
            s = lax.dot_general(ikc, iq_ref[0, :, h * 128:(h + 1) * 128], nt,
                                preferred_element_type=jnp.float32)
            acc = acc + iwt[h:h + 1, :] * jnp.maximum(s, 0.0)
        score = jnp.where(c * KC + krow <= qpos, acc * idx_scale, -jnp.inf)
        keys_ref[c] = _sortable_key(score)
        return carry
    lax.fori_loop(0, n_kc, idx_body, 0)

    def row_fold(m):
        parts = [m[8 * i:8 * i + 8] for i in range(m.shape[0] // 8)]
        while len(parts) > 1:
            parts = [parts[i] + parts[i + 1] for i in range(0, len(parts), 2)]
        return parts[0]

    def count(pred):
        def body(c, acc):
            return acc + row_fold(jnp.where(pred(c, keys_ref[c]), 1.0, 0.0))
        acc = lax.fori_loop(0, n_kc, body, jnp.zeros((8, TQ), jnp.float32))
        return jnp.sum(acc, axis=0, keepdims=True)

    def count_ge(cand):
        return count(lambda c, key: key >= cand)

    kf = float(topk)
    ans0 = jnp.where(count_ge(jnp.zeros((1, TQ), jnp.int32)) >= kf, jnp.int32(0), jnp.int32(_INT_MIN))

    def bit_body(i, ans):
        cand = ans | jnp.left_shift(jnp.int32(1), 30 - i)
        return jnp.where(count_ge(cand) >= kf, cand, ans)
    thr = lax.fori_loop(0, 31, bit_body, ans0)

    n_gt = count_ge(thr + 1)
    n_ge = count_ge(thr)
    need = kf - n_gt
    tie = jnp.logical_and(n_ge - n_gt > need, thr != _FILL_KEY)
    lim_ref[...] = jnp.full((1, TQ), seq, jnp.int32)

    @pl.when(jnp.max(jnp.where(tie, 1.0, 0.0)) > 0.0)
    def _():
        nbits = max(1, int(np.ceil(np.log2(seq))))

        def bit2(i, lo):
            cand = lo | jnp.left_shift(jnp.int32(1), nbits - 1 - i)
            n_before = count(lambda c, key: jnp.logical_and(key == thr, c * KC + krow < cand))
            return jnp.where(n_before < need, cand, lo)
        lo = lax.fori_loop(0, nbits, bit2, jnp.zeros((1, TQ), jnp.int32))
        lim_ref[...] = jnp.where(tie, lo, jnp.int32(seq))

    lim = lim_ref[...]

    m_ref[...] = jnp.full(m_ref.shape, _NEG_BIG, jnp.float32)
    l_ref[...] = jnp.zeros(l_ref.shape, jnp.float32)
    acc_ref[...] = jnp.zeros(acc_ref.shape, jnp.float32)
    hd = ATT_HEAD_DIM

    def att_body(c, carry):
        vtc = vt_ref[c]
        for half in range(KC // KA):
            h0 = half * KA
            r0 = pl.multiple_of(c * KC, KC) + h0
            kpos = c * KC + h0 + krow[0:KA]
            key = keys_ref[c, h0:h0 + KA, :]
            sel = jnp.logical_or(key > thr, jnp.logical_and(key == thr, kpos <= lim))
            sel = jnp.logical_and(sel, kpos <= qpos)
            madd = jnp.where(sel, 0.0, _NEG_BIG)
            mx = []
            for h in range(ATT_HEADS):
                kc = kn_ref[h // ATT_REP, pl.ds(r0, KA), :]
                sc = lax.dot_general(kc, qs_ref[h], nt, preferred_element_type=jnp.float32) + madd
                sc_ref[h] = sc
                mx.append(jnp.max(sc, axis=0, keepdims=True))
            for h in range(ATT_HEADS):
                g = h // ATT_REP
                m_old = m_ref[h]
                m_new = jnp.maximum(m_old, mx[h])
                p = jnp.exp(sc_ref[h] - m_new)
                alpha = jnp.exp(m_old - m_new)
                l_ref[h] = alpha * l_ref[h] + jnp.sum(p, axis=0, keepdims=True)
                m_ref[h] = m_new
                pv = jnp.dot(vtc[g * hd:(g + 1) * hd, h0:h0 + KA], p.astype(bf),
                             preferred_element_type=jnp.float32)
                rs = slice((h % 2) * hd, (h % 2 + 1) * hd)
                acc_ref[h // 2, rs, :] = acc_ref[h // 2, rs, :] * alpha + pv
        return carry
    lax.fori_loop(0, n_kc, att_body, 0)

    for pair in range(ATT_HEADS // 2):
        den = jnp.concatenate([jnp.broadcast_to(l_ref[2 * pair], (hd, TQ)),
                               jnp.broadcast_to(l_ref[2 * pair + 1], (hd, TQ))], axis=0)
        o_ref[0, :, pair * 128:(pair + 1) * 128] = (acc_ref[pair] / den).T.astype(o_ref.dtype)


def dsa_attention(q, iq, iw, k, v, ik, qg, kg):
    B, S, _ = q.shape
    TQ, KC = DSA_TQ, DSA_KC
    assert S % KC == 0 and S % TQ == 0
    topk = min(IDX_TOPK_MAX, S // 4)
    kern = functools.partial(_dsa_kernel, seq=S, topk=topk)

    def pad_gain(g):
        return jnp.concatenate([g, jnp.zeros_like(g)]).reshape(1, 128)
    return pl.pallas_call(
        kern,
        grid=(B, S // TQ),
        in_specs=[
            pl.BlockSpec((1, TQ, ATT_HEADS * 128), lambda b, j: (b, j, 0)),
            pl.BlockSpec((1, TQ, IDX_HEADS * 128), lambda b, j: (b, j, 0)),
            pl.BlockSpec((1, TQ, 128), lambda b, j: (b, j, 0)),
            pl.BlockSpec((1, S, ATT_GROUPS * 128), lambda b, j: (b, 0, 0)),
            pl.BlockSpec((1, S, ATT_GROUPS * ATT_HEAD_DIM), lambda b, j: (b, 0, 0)),
            pl.BlockSpec((1, S, 128), lambda b, j: (b, 0, 0)),
            pl.BlockSpec((1, 128), lambda b, j: (0, 0)),
            pl.BlockSpec((1, 128), lambda b, j: (0, 0)),
        ],
        out_specs=pl.BlockSpec((1, TQ, ATT_HEADS * ATT_HEAD_DIM), lambda b, j: (b, j, 0)),
        out_shape=jax.ShapeDtypeStruct((B, S, ATT_HEADS * ATT_HEAD_DIM), jnp.bfloat16),
        scratch_shapes=[
            pltpu.VMEM((ATT_GROUPS, S, 128), jnp.bfloat16),
            pltpu.VMEM((S // KC, ATT_GROUPS * ATT_HEAD_DIM, KC), jnp.bfloat16),
            pltpu.VMEM((ATT_HEADS, TQ, 128), jnp.bfloat16),
            pltpu.VMEM((S // KC, KC, TQ), jnp.int32),
            pltpu.VMEM((1, TQ), jnp.int32),
            pltpu.VMEM((ATT_HEADS, 1, TQ), jnp.float32),
            pltpu.VMEM((ATT_HEADS, 1, TQ), jnp.float32),
            pltpu.VMEM((ATT_HEADS // 2, 2 * ATT_HEAD_DIM, TQ), jnp.float32),
            pltpu.VMEM((ATT_HEADS, DSA_KA, TQ), jnp.float32),
        ],
        compiler_params=pltpu.CompilerParams(
            dimension_semantics=("arbitrary", "arbitrary"),
            vmem_limit_bytes=_VMEM_LIMIT),
        name="dsa_attention",
    )(q, iq, iw, k, v, ik, pad_gain(qg), pad_gain(kg))


PEER_RT = 256
PEER_ET = 128
_ROW_CHUNKS = D_MODEL // 128
_TAB_ROWS = _ROW_CHUNKS // 2
_EDGES = PEER_HEADS * PEER_TOPK


def _extract_topk_rows(work, n_out, val_ref, pay_ref, order=None, payload=None):
    if order is None:
        order = lax.broadcasted_iota(jnp.int32, work.shape, 0)
    big = jnp.int32(2 ** 30)
    for it in range(n_out):
        m = jnp.max(work, axis=0, keepdims=True)
        am = jnp.min(jnp.where(work == m, order, big), axis=0, keepdims=True)
        hit = order == am
        val_ref[it:it + 1, :] = m
        if payload is None:
            pay_ref[it:it + 1, :] = am
        else:
            pay_ref[it:it + 1, :] = jnp.max(jnp.where(hit, payload, -1), axis=0, keepdims=True)
        work = jnp.where(hit, -jnp.inf, work)


def _peer_route_kernel(q_ref, sk_ref, e_ref, g_ref, s1_ref, i1_ref, s2_ref, i2_ref, ts_ref, te_ref, et_ref,
                       gt_ref):
    K = PEER_TOPK
    RT = q_ref.shape[0]
    b8 = lax.broadcasted_iota(jnp.int32, (8, RT), 0)
    b16 = lax.broadcasted_iota(jnp.int32, (K, RT), 0)
    order = jnp.concatenate([b16] + [a * K + b8 for a in range(1, 8)] + [(8 + b8) * K], axis=0)
    for h in range(PEER_HEADS):
        for p, (s_ref, i_ref) in enumerate(((s1_ref, i1_ref), (s2_ref, i2_ref))):
            col = (h * 2 + p) * 128
            qs = q_ref[:, col:col + 128].astype(jnp.bfloat16)
            sk = sk_ref[h, p].astype(jnp.bfloat16)
            sc = lax.dot_general(sk, qs, (((1,), (1,)), ((), ())),
                                 preferred_element_type=jnp.float32)
            _extract_topk_rows(sc, K, s_ref, i_ref)
        s2 = s2_ref[...]
        i2 = i2_ref[...]
        cand = [s1_ref[0:1, :] + s2]
        cidx = [i1_ref[0:1, :] * PEER_NKEYS + i2]
        for a in range(1, 8):
            cand.append(jnp.where(b8 < K // (a + 1), s1_ref[a:a + 1, :] + s2[0:8], -jnp.inf))
            cidx.append(i1_ref[a:a + 1, :] * PEER_NKEYS + i2[0:8])
        cand.append(s1_ref[8:K, :] + s2[0:1])
        cidx.append(i1_ref[8:K, :] * PEER_NKEYS + i2[0:1])
        _extract_topk_rows(jnp.concatenate(cand, axis=0), K, ts_ref, te_ref, order,
                           jnp.concatenate(cidx, axis=0) * _TAB_ROWS)
        ts = ts_ref[...]
        ex = jnp.exp(ts - jnp.max(ts, axis=0, keepdims=True))
        gate = ex / jnp.sum(ex, axis=0, keepdims=True)
        rep = jnp.concatenate([jnp.broadcast_to(gate[k:k + 1, :], (_ROW_CHUNKS, RT)) for k in range(K)], axis=0)
        gt_ref[:, h * 128:(h + 1) * 128] = rep.T
        et_ref[h * K:(h + 1) * K, :] = te_ref[...]
    e_ref[...] = et_ref[...].T
    g_ref[...] = gt_ref[...].reshape(RT, PEER_HEADS, 128)


def peer_route(qry, subkeys):
    T = qry.shape[0]
    RT = PEER_RT
    assert T % RT == 0
    K = PEER_TOPK
    return pl.pallas_call(
        _peer_route_kernel,
        grid=(T // RT,),
        in_specs=[
            pl.BlockSpec((RT, qry.shape[1]), lambda i: (i, 0)),
            pl.BlockSpec(subkeys.shape, lambda i: (0, 0, 0, 0)),
        ],
        out_specs=[pl.BlockSpec((RT, _EDGES), lambda i: (i, 0)),
                   pl.BlockSpec((RT, PEER_HEADS, 128), lambda i: (i, 0, 0))],
        out_shape=[jax.ShapeDtypeStruct((T, _EDGES), jnp.int32),
                   jax.ShapeDtypeStruct((T, PEER_HEADS, 128), jnp.float32)],
        scratch_shapes=[
            pltpu.VMEM((K, RT), jnp.float32), pltpu.VMEM((K, RT), jnp.int32),
            pltpu.VMEM((K, RT), jnp.float32), pltpu.VMEM((K, RT), jnp.int32),
            pltpu.VMEM((K, RT), jnp.float32), pltpu.VMEM((K, RT), jnp.int32),
            pltpu.VMEM((_EDGES, RT), jnp.int32),
            pltpu.VMEM((RT, PEER_HEADS * 128), jnp.float32),
        ],
        compiler_params=pltpu.CompilerParams(dimension_semantics=("arbitrary",)),
        name="peer_route",
    )(qry, subkeys)


def _pack_kernel(t_ref, o_ref):
    b = lax.bitcast_convert_type(t_ref[...], jnp.uint32)
    r = b + jnp.uint32(0x7FFF) + ((b >> 16) & jnp.uint32(1))
    words = [(r[:, (2 * s) * 128:(2 * s + 1) * 128] >> 16)
             | (r[:, (2 * s + 1) * 128:(2 * s + 2) * 128] & jnp.uint32(0xFFFF0000)) for s in range(_TAB_ROWS)]
    w = jnp.concatenate(words, axis=1)
    o_ref[...] = w.reshape(w.shape[0] * _TAB_ROWS, 128)


def pack_table(tab):
    n, d = tab.shape
    tr = 256
    assert n % tr == 0 and d == D_MODEL
    return pl.pallas_call(
        _pack_kernel,
        grid=(n // tr,),
        in_specs=[pl.BlockSpec((tr, d), lambda i: (i, 0))],
        out_specs=pl.BlockSpec((tr * _TAB_ROWS, 128), lambda i: (i, 0)),
        out_shape=jax.ShapeDtypeStruct((n * _TAB_ROWS, 128), jnp.uint32),
        compiler_params=pltpu.CompilerParams(dimension_semantics=("arbitrary",)),
        name="pack_table",
    )(tab)


def _gather_rows(idx_ref, tab_ref, stage_ref, t):
    R = _TAB_ROWS
    for i in range(_EDGES):
        row = pl.multiple_of(idx_ref[t, i], R)
        stage_ref[R * i:R * i + R, :] = tab_ref[pl.ds(row, R), :]


def _split_bf16(x):
    hi = x.astype(jnp.bfloat16)
    lo = (x - hi.astype(jnp.float32)).astype(jnp.bfloat16)
    return jnp.concatenate([hi, lo], axis=0)


_SLOTS = 8
_AHEAD = 2


def _pipelined_tokens(n_tok, gather, compute):
    for t in range(_AHEAD):
        gather(t, t)

    def body(tt, carry):
        t0 = _SLOTS * tt
        for u in range(_SLOTS):
            gather(jnp.minimum(t0 + u + _AHEAD, n_tok - 1), (u + _AHEAD) % _SLOTS)
            compute(t0 + u, u)
        return carry
    lax.fori_loop(0, n_tok // _SLOTS, body, 0)


def _peer_dot_kernel(idx_ref, tab_ref, x_ref, a_ref, *stages):
    n = _EDGES * _ROW_CHUNKS
    sub = lax.broadcasted_iota(jnp.int32, (2 * _ROW_CHUNKS, n), 0) % _ROW_CHUNKS
    lane = lax.broadcasted_iota(jnp.int32, (2 * _ROW_CHUNKS, n), 1)
    diag = sub == lane % _ROW_CHUNKS

    def gather(t, slot):
        _gather_rows(idx_ref, tab_ref, stages[slot], t)

    def compute(t, slot):
        m = pltpu.bitcast(stages[slot][...], jnp.bfloat16)
        x16 = _split_bf16(x_ref[t])
        r = lax.dot_general(x16, m, (((1,), (1,)), ((), ())),
                            preferred_element_type=jnp.float32)
        tot = jnp.sum(jnp.where(diag, r, 0.0), axis=0, keepdims=True)
        for g in range(n // 128):
            a_ref[t, g:g + 1, :] = tot[:, g * 128:(g + 1) * 128]

    _pipelined_tokens(a_ref.shape[0], gather, compute)


def _group_sum_kernel(p_ref, a_ref):
    a = p_ref[...]
    lane = lax.broadcasted_iota(jnp.int32, a.shape, 1)
    for sh in (1, 2, 4):
        a = a + jnp.where((lane & sh) == 0, pltpu.roll(a, 128 - sh, 1), pltpu.roll(a, sh, 1))
    a_ref[...] = a


def _group_sum(part):
    rows = part.shape[0]
    tr = 4096
    assert rows % tr == 0
    return pl.pallas_call(
        _group_sum_kernel,
        grid=(rows // tr,),
        in_specs=[pl.BlockSpec((tr, 128), lambda i: (i, 0))],
        out_specs=pl.BlockSpec((tr, 128), lambda i: (i, 0)),
        out_shape=jax.ShapeDtypeStruct(part.shape, part.dtype),
        compiler_params=pltpu.CompilerParams(dimension_semantics=("arbitrary",)),
        name="peer_group_sum",
    )(part)


def _peer_mix_kernel(idx_ref, tab_ref, a_ref, g_ref, x1_ref, g2_ref, y_ref, yc_ref, *stages):
    sub = lax.broadcasted_iota(jnp.int32, (_ROW_CHUNKS, 128), 0)
    lane = lax.broadcasted_iota(jnp.int32, (_ROW_CHUNKS, 128), 1)
    diag = sub == lane % _ROW_CHUNKS
    ngroups = _EDGES * _ROW_CHUNKS // 128

    def gather(t, slot):
        _gather_rows(idx_ref, tab_ref, stages[slot], t)

    def compute(t, slot):
        m = pltpu.bitcast(stages[slot][...], jnp.bfloat16)
        a = a_ref[t]
        c = g_ref[t] * (0.5 * a * (1.0 + lax.erf(a * (2.0 ** -0.5))))
        lhs = jnp.concatenate(
            [jnp.where(diag, jnp.broadcast_to(c[g:g + 1, :], (_ROW_CHUNKS, 128)), 0.0) for g in range(ngroups)],
            axis=1)
        r = jnp.dot(_split_bf16(lhs), m, preferred_element_type=jnp.float32)
        yc_ref[t] = x1_ref[t] + g2_ref[0] * (r[0:_ROW_CHUNKS] + r[_ROW_CHUNKS:])

    _pipelined_tokens(y_ref.shape[0], gather, compute)
    y_ref[...] = yc_ref[...].reshape(y_ref.shape)


def _expert_call(kern, name, idx, tab, ins, extra=(), extra_specs=(), rows_out=False):
    T = idx.shape[0]
    ET = PEER_ET
    assert T % ET == 0
    blk = pl.BlockSpec((ET, _ROW_CHUNKS, 128), lambda i: (i, 0, 0))
    stage = [pltpu.VMEM((_EDGES * _TAB_ROWS, 128), jnp.uint32)] * _SLOTS
    if rows_out:
        out_spec = pl.BlockSpec((ET, D_MODEL), lambda i: (i, 0))
        out_shape = jax.ShapeDtypeStruct((T, D_MODEL), jnp.float32)
        scratch = [pltpu.VMEM((ET, _ROW_CHUNKS, 128), jnp.float32)] + stage
    else:
        out_spec, out_shape, scratch = blk, jax.ShapeDtypeStruct((T, _ROW_CHUNKS, 128), jnp.float32), stage
    return pl.pallas_call(
        kern,
        grid=(T // ET,),
        in_specs=[
            pl.BlockSpec((ET, _EDGES), lambda i: (i, 0), memory_space=pltpu.SMEM),
            pl.BlockSpec(tab.shape, lambda i: (0, 0), pipeline_mode=pl.Buffered(1)),
        ] + [blk] * len(ins) + list(extra_specs),
        out_specs=out_spec,
        out_shape=out_shape,
        scratch_shapes=scratch,
        compiler_params=pltpu.CompilerParams(
            dimension_semantics=("arbitrary",),
            vmem_limit_bytes=44 * 1024 * 1024),
        name=name,
    )(idx, tab, *ins, *extra)


def peer_experts(h2c, idx, gexp, u, vtab, x1c, g2, seq):
    T = h2c.shape[0]
    ET = PEER_ET
    assert seq % ET == 0
    part = _expert_call(_peer_dot_kernel, "peer_dot", idx, pack_table(u), [h2c])
    a = _group_sum(part.reshape(T * _ROW_CHUNKS, 128)).reshape(T, _ROW_CHUNKS, 128)
    g2_spec = pl.BlockSpec((1, _ROW_CHUNKS, 128), lambda i: (i * ET // seq, 0, 0))
    return _expert_call(_peer_mix_kernel, "peer_mix", idx, pack_table(vtab), [a, gexp, x1c],
                        extra=[g2.reshape(g2.shape[0], _ROW_CHUNKS, 128)], extra_specs=[g2_spec], rows_out=True)


def kernel(x, c, w_ada, b_ada, norm1_g, norm2_g, w_in, hg_lb, hg_onorm_g, q_norm_g, k_norm_g,
           w_a, w_b, w_out, peer_wq, peer_subkeys, peer_u, peer_v):
    B, S, D = x.shape
    T = B * S
    for l in range(w_ada.shape[0]):
        mod = ada_modulation(c, w_ada[l], b_ada[l])
        sh1, sc1, g1, sh2, sc2, g2 = jnp.split(mod, 6, axis=-1)
        hg, q, k, iw, iq, ik, v, gate = in_projection(x, norm1_g[l], sc1, sh1, pack_w_in(w_in[l]))
        ya = hgrn2(hg, hg_lb, hg_onorm_g[l], l)
        yb = dsa_attention(q, iq, iw, k, v, ik, q_norm_g[l], k_norm_g[l])
        x1, h2, qry = out_projection(ya, yb, gate, x, g1, norm2_g[l], sc2, sh2,
                                     w_a[l], w_b[l], w_out[l], peer_wq[l])
        idx, gexp = peer_route(qry.reshape(T, -1), peer_subkeys[l])
        x = peer_experts(h2.reshape(T, D // 128, 128), idx, gexp, peer_u[l], peer_v[l],
                         x1.reshape(T, D // 128, 128), g2, S).reshape(B, S, D)
    return x
```

```python
import functools

import jax
import jax.numpy as jnp
import numpy as np
from jax import lax
from jax.experimental import pallas as pl
from jax.experimental.pallas import tpu as pltpu

D_MODEL = 1024
HG_HEADS = 4
HG_DK = 128
HG_WIDTH = HG_HEADS * HG_DK
HG_CHUNK = 64
ATT_HEADS = 8
ATT_GROUPS = 2
ATT_HEAD_DIM = 64
ATT_REP = ATT_HEADS // ATT_GROUPS
IDX_HEADS = 8
IDX_DIM = 64
IDX_TOPK_MAX = 256
PEER_HEADS = 8
PEER_NKEYS = 128
PEER_TOPK = 16
RMS_EPS = 1e-6
IN_SPLIT = (HG_WIDTH, HG_WIDTH, HG_WIDTH, HG_WIDTH,
            ATT_HEADS * ATT_HEAD_DIM, ATT_GROUPS * ATT_HEAD_DIM, ATT_GROUPS * ATT_HEAD_DIM,
            IDX_HEADS * IDX_DIM, IDX_DIM, IDX_HEADS,
            D_MODEL, D_MODEL)

_VMEM_LIMIT = 48 * 1024 * 1024


def _ada_kernel(c_ref, w_ref, b_ref, o_ref):
    c = c_ref[...]
    cs = c * (1.0 / (1.0 + jnp.exp(-c)))
    o_ref[...] = jnp.dot(cs, w_ref[...], preferred_element_type=jnp.float32,
                         precision=lax.Precision.HIGHEST) + b_ref[...]


def ada_modulation(c, w, b):
    B, D = c.shape
    N = w.shape[1]
    tn = 1536
    assert N % tn == 0
    return pl.pallas_call(
        _ada_kernel,
        grid=(N // tn,),
        in_specs=[pl.BlockSpec((B, D), lambda j: (0, 0)),
                  pl.BlockSpec((D, tn), lambda j: (0, j)),
                  pl.BlockSpec((1, tn), lambda j: (0, j))],
        out_specs=pl.BlockSpec((B, tn), lambda j: (0, j)),
        out_shape=jax.ShapeDtypeStruct((B, N), jnp.float32),
        compiler_params=pltpu.CompilerParams(dimension_semantics=("arbitrary",),
                                             vmem_limit_bytes=_VMEM_LIMIT),
        name="ada_modulation",
    )(c, w, b.reshape(1, N))


_PROJ_SEGS = (("hg", 2048, jnp.float32), ("q", 1024, jnp.float32), ("k", 256, jnp.float32),
              ("iw", 128, jnp.float32), ("iq", 1024, jnp.bfloat16), ("ik", 128, jnp.bfloat16),
              ("v", 128, jnp.bfloat16), ("gate", 2048, jnp.float32))
PROJ_TM = 256


def pack_w_in(w_in):
    D = w_in.shape[0]
    offs = np.cumsum((0,) + IN_SPLIT)
    hq, hf, hi, hg, aq, ak, av, iq, ik, iw, ga, gb = [w_in[:, offs[i]:offs[i + 1]] for i in range(12)]

    def pad_heads(w, nh):
        w = w.reshape(D, nh, 64)
        return jnp.concatenate([w, jnp.zeros_like(w)], axis=-1).reshape(D, nh * 128)

    iwp = jnp.concatenate([iw, jnp.zeros((D, 128 - IDX_HEADS), w_in.dtype)], axis=1)
    cols = [hq, hf, hi, hg, pad_heads(aq, ATT_HEADS), pad_heads(ak, ATT_GROUPS), iwp,
            pad_heads(iq, IDX_HEADS), pad_heads(ik, 1),
            av, ga, gb]
    return jnp.concatenate(cols, axis=1).astype(jnp.bfloat16)


def _proj_kernel(x_ref, g_ref, sc_ref, sh_ref, w_ref, *o_refs):
    x = x_ref[0]
    ms = jnp.mean(x * x, axis=-1, keepdims=True)
    h = x * lax.rsqrt(ms + RMS_EPS) * g_ref[...] * (1.0 + sc_ref[0]) + sh_ref[0]
    hb = h.astype(jnp.bfloat16)
    c0 = 0
    for (_, width, _), o_ref in zip(_PROJ_SEGS, o_refs):
        for s in range(0, width, 512):
            e = min(s + 512, width)
            o_ref[0, :, s:e] = jnp.dot(hb, w_ref[:, c0 + s:c0 + e],
                                       preferred_element_type=jnp.float32).astype(o_ref.dtype)
        c0 += width


def in_projection(x, norm_g, sc, sh, w_packed):
    B, S, D = x.shape
    tm = PROJ_TM
    assert S % tm == 0
    ntot = sum(w for _, w, _ in _PROJ_SEGS)
    assert w_packed.shape == (D, ntot)
    return pl.pallas_call(
        _proj_kernel,
        grid=(B, S // tm),
        in_specs=[
            pl.BlockSpec((1, tm, D), lambda b, i: (b, i, 0)),
            pl.BlockSpec((1, D), lambda b, i: (0, 0)),
            pl.BlockSpec((1, 1, D), lambda b, i: (b, 0, 0)),
            pl.BlockSpec((1, 1, D), lambda b, i: (b, 0, 0)),
            pl.BlockSpec((D, ntot), lambda b, i: (0, 0), pipeline_mode=pl.Buffered(1)),
        ],
        out_specs=[pl.BlockSpec((1, tm, w), lambda b, i: (b, i, 0)) for _, w, _ in _PROJ_SEGS],
        out_shape=[jax.ShapeDtypeStruct((B, S, w), dt) for _, w, dt in _PROJ_SEGS],
        compiler_params=pltpu.CompilerParams(dimension_semantics=("arbitrary", "arbitrary"),
                                             vmem_limit_bytes=_VMEM_LIMIT),
        name="in_projection",
    )(x, norm_g.reshape(1, D), sc.reshape(B, 1, D), sh.reshape(B, 1, D), w_packed)


HG_LT = 512


def _split3_bf16(x):
    a = x.astype(jnp.bfloat16)
    r = x - a.astype(jnp.float32)
    b = r.astype(jnp.bfloat16)
    c = (r - b.astype(jnp.float32)).astype(jnp.bfloat16)
    return a, b, c


def _hgrn_kernel(q_ref, f_ref, i_ref, g_ref, lb_ref, og_ref, o_ref, st_ref, *, layer):
    C = HG_CHUNK
    bf = jnp.bfloat16

    @pl.when(pl.program_id(1) == 0)
    def _():
        st_ref[...] = jnp.zeros(st_ref.shape, jnp.float32)

    lbr = lb_ref[...]
    e = jnp.exp(lbr - jnp.max(lbr, axis=0, keepdims=True))
    sm = e / jnp.sum(e, axis=0, keepdims=True)
    lb_all = jnp.sum(sm[0:layer + 1], axis=0, keepdims=True)

    r_i = lax.broadcasted_iota(jnp.int32, (C, C), 0)
    c_i = lax.broadcasted_iota(jnp.int32, (C, C), 1)
    causal = c_i <= r_i
    tri = jnp.where(causal, 1.0, 0.0).astype(bf)
    og = og_ref[...]

    def chunk(c, carry):
        r0 = pl.multiple_of(c * C, C)
        H = range(HG_HEADS)
        cols = [slice(h * HG_DK, (h + 1) * HG_DK) for h in H]
        dot = functools.partial(jnp.dot, preferred_element_type=jnp.float32)
        nt = (((1,), (1,)), ((), ()))
        tn = (((0,), (0,)), ((), ()))
        qr = [q_ref[0, pl.ds(r0, C), cs] for cs in cols]
        q = [x * (1.0 / (1.0 + jnp.exp(-x))) for x in qr]
        f = [lb_all[:, cs] + (1.0 - lb_all[:, cs]) * (1.0 / (1.0 + jnp.exp(-f_ref[0, pl.ds(r0, C), cs])))
             for cs in cols]
        k = [1.0 - x for x in f]
        vb = [i_ref[0, pl.ds(r0, C), cs].astype(bf) for cs in cols]
        lg = [_split3_bf16(jnp.log(x)) for x in f]
        b = [dot(tri, l1) + dot(tri, l2) + dot(tri, l3) for l1, l2, l3 in lg]
        bm = [x[C // 2 - 1:C // 2, :] for x in b]
        bl = [x[C - 1:C, :] for x in b]
        a = [lax.dot_general((q[h] * jnp.exp(b[h] - bm[h])).astype(bf), (k[h] * jnp.exp(bm[h] - b[h])).astype(bf),
                             nt, preferred_element_type=jnp.float32) for h in H]
        st = [st_ref[h] for h in H]
        o_inter = [lax.dot_general((q[h] * jnp.exp(b[h])).astype(bf), st[h].astype(bf), nt,
                                   preferred_element_type=jnp.float32) for h in H]
        ds = [lax.dot_general(vb[h], (k[h] * jnp.exp(bl[h] - b[h])).astype(bf), tn,
                              preferred_element_type=jnp.float32) for h in H]
        for h in H:
            st_ref[h] = st[h] * jnp.exp(bl[h]) + ds[h]
        o = [dot(jnp.where(causal, a[h], 0.0).astype(bf), vb[h]) + o_inter[h] for h in H]
        for h in H:
            on = o[h] * lax.rsqrt(jnp.mean(o[h] * o[h], axis=-1, keepdims=True) + RMS_EPS) * og
            gr = g_ref[0, pl.ds(r0, C), cols[h]]
            o_ref[0, pl.ds(r0, C), cols[h]] = (on * (gr * (1.0 / (1.0 + jnp.exp(-gr))))).astype(o_ref.dtype)
        return carry
    lax.fori_loop(0, q_ref.shape[1] // C, chunk, 0)


def hgrn2(hg, hg_lb, onorm_g, layer):
    B, S, _ = hg.shape
    LT = HG_LT
    assert S % LT == 0

    def piece(p):
        return pl.BlockSpec((1, LT, HG_WIDTH), lambda b, i, p=p: (b, i, p))
    return pl.pallas_call(
        functools.partial(_hgrn_kernel, layer=layer),
        grid=(B, S // LT),
        in_specs=[piece(0), piece(1), piece(2), piece(3),
                  pl.BlockSpec(hg_lb.shape, lambda b, i: (0, 0)),
                  pl.BlockSpec((1, HG_DK), lambda b, i: (0, 0))],
        out_specs=pl.BlockSpec((1, LT, HG_WIDTH), lambda b, i: (b, i, 0)),
        out_shape=jax.ShapeDtypeStruct((B, S, HG_WIDTH), jnp.bfloat16),
        scratch_shapes=[pltpu.VMEM((HG_HEADS, HG_DK, HG_DK), jnp.float32)],
        compiler_params=pltpu.CompilerParams(dimension_semantics=("arbitrary", "arbitrary"),
                                             vmem_limit_bytes=_VMEM_LIMIT),
        name="hgrn2",
    )(hg, hg, hg, hg, hg_lb, onorm_g.reshape(1, HG_DK))


OUT_TM = 256


def _out_kernel(ya_ref, yb_ref, ga_ref, gb_ref, x_ref, g1_ref, n2_ref, sc_ref, sh_ref,
                wa_ref, wb_ref, wo_ref, wq_ref, x1_ref, h2_ref, qry_ref):
    bf = jnp.bfloat16

    def sig(z):
        return 1.0 / (1.0 + jnp.exp(-z))
    ma = jnp.dot(ya_ref[0], wa_ref[...], preferred_element_type=jnp.float32)
    mb = jnp.dot(yb_ref[0], wb_ref[...], preferred_element_type=jnp.float32)
    merged = sig(ga_ref[0]) * ma + sig(gb_ref[0]) * mb
    x1 = x_ref[0] + g1_ref[0] * jnp.dot(merged.astype(bf), wo_ref[...], preferred_element_type=jnp.float32)
    tm, d = x1.shape
    x1_ref[0] = x1.reshape(tm, d // 128, 128)
    ms = jnp.mean(x1 * x1, axis=-1, keepdims=True)
    h2 = x1 * lax.rsqrt(ms + RMS_EPS) * n2_ref[...] * (1.0 + sc_ref[0]) + sh_ref[0]
    h2_ref[0] = h2.reshape(tm, d // 128, 128)
    qry_ref[0] = jnp.dot(h2.astype(bf), wq_ref[...], preferred_element_type=jnp.float32)


def out_projection(ya, yb, gate, x, g1, norm2_g, sc2, sh2, w_a, w_b, w_out, wq):
    B, S, D = x.shape
    tm = OUT_TM
    bf = jnp.bfloat16
    NQ = wq.shape[1]

    def tok(width, col=0):
        return pl.BlockSpec((1, tm, width), lambda b, i, col=col: (b, i, col))

    def chunked_tok():
        return pl.BlockSpec((1, tm, D // 128, 128), lambda b, i: (b, i, 0, 0))

    def per_batch():
        return pl.BlockSpec((1, 1, D), lambda b, i: (b, 0, 0))

    def const(shape):
        return pl.BlockSpec(shape, lambda b, i: (0,) * len(shape), pipeline_mode=pl.Buffered(1))
    return pl.pallas_call(
        _out_kernel,
        grid=(B, S // tm),
        in_specs=[tok(HG_WIDTH), tok(512), tok(D, 0), tok(D, 1), tok(D), per_batch(),
                  const((1, D)), per_batch(), per_batch(),
                  const(w_a.shape), const(w_b.shape), const(w_out.shape), const(wq.shape)],
        out_specs=[chunked_tok(), chunked_tok(), tok(NQ)],
        out_shape=[jax.ShapeDtypeStruct((B, S, D // 128, 128), jnp.float32),
                   jax.ShapeDtypeStruct((B, S, D // 128, 128), jnp.float32),
                   jax.ShapeDtypeStruct((B, S, NQ), jnp.float32)],
        compiler_params=pltpu.CompilerParams(dimension_semantics=("arbitrary", "arbitrary"),
                                             vmem_limit_bytes=_VMEM_LIMIT),
        name="out_projection",
    )(ya, yb, gate, gate, x, g1.reshape(B, 1, D), norm2_g.reshape(1, D), sc2.reshape(B, 1, D),
      sh2.reshape(B, 1, D), w_a.astype(bf), w_b.astype(bf), w_out.astype(bf), wq.astype(bf))


DSA_TQ = 128
DSA_KC = 512
DSA_KA = 256
_INT_MIN = -2 ** 31
_FILL_KEY = -2139095041
_NEG_BIG = -1e30


def _sortable_key(x):
    b = lax.bitcast_convert_type(x, jnp.int32)
    return b ^ ((b >> 31) & jnp.int32(0x7FFFFFFF))


def _dsa_kernel(q_ref, iq_ref, iw_ref, k_ref, v_ref, ik_ref, qg_ref, kg_ref, o_ref,
                kn_ref, vt_ref, qs_ref, keys_ref, lim_ref, m_ref, l_ref, acc_ref, sc_ref, *, seq, topk):
    TQ, KC, KA = DSA_TQ, DSA_KC, DSA_KA
    bf = jnp.bfloat16
    j = pl.program_id(1)
    n_kc = (j * TQ + TQ + KC - 1) // KC
    idx_scale = (IDX_DIM ** -0.5) * (IDX_HEADS ** -0.5)
    att_scale = ATT_HEAD_DIM ** -0.5
    nt = (((1,), (1,)), ((), ()))

    lane = lax.broadcasted_iota(jnp.int32, (1, 128), 1)
    krow = lax.broadcasted_iota(jnp.int32, (KC, 1), 0)

    def kfeat(kpos):
        a = (kpos // 64).astype(jnp.float32)
        b = (kpos % 64).astype(jnp.float32)
        return jnp.where(lane == ATT_HEAD_DIM, a, jnp.where(lane == ATT_HEAD_DIM + 1, b, 0.0))

    def qfeat(slope):
        return jnp.where(lane == ATT_HEAD_DIM, 64.0 * slope, jnp.where(lane == ATT_HEAD_DIM + 1, slope, 0.0))

    @pl.when(j == 0)
    def _():
        def body(c, carry):
            r0 = pl.multiple_of(c * KC, KC)
            for g in range(ATT_GROUPS):
                kk = k_ref[0, pl.ds(r0, KC), g * 128:(g + 1) * 128]
                ms = jnp.sum(kk * kk, axis=-1, keepdims=True) * (1.0 / ATT_HEAD_DIM)
                kn = kk * lax.rsqrt(ms + RMS_EPS) * kg_ref[...]
                kn_ref[g, pl.ds(r0, KC), :] = (kn + kfeat(r0 + krow)).astype(bf)
            vt_ref[c] = v_ref[0, pl.ds(r0, KC), :].astype(jnp.float32).T.astype(bf)
            return carry
        lax.fori_loop(0, seq // KC, body, 0)

    for h in range(ATT_HEADS):
        qq = q_ref[0, :, h * 128:(h + 1) * 128]
        ms = jnp.sum(qq * qq, axis=-1, keepdims=True) * (1.0 / ATT_HEAD_DIM)
        slope = float(2.0 ** (-8.0 * (h + 1) / ATT_HEADS))
        qs_ref[h] = (qq * lax.rsqrt(ms + RMS_EPS) * qg_ref[...] * att_scale + qfeat(slope)).astype(bf)
    iwt = iw_ref[0].T

    qpos = j * TQ + lax.broadcasted_iota(jnp.int32, (1, TQ), 1)

    def idx_body(c, carry):
        r0 = pl.multiple_of(c * KC, KC)
        ikc = ik_ref[0, pl.ds(r0, KC), :]
        acc = jnp.zeros((KC, TQ), jnp.float32)
        for h in range(IDX_HEADS):
            s = lax.dot_general(ikc, iq_ref[0, :, h * 128:(h + 1) * 128], nt,
                                preferred_element_type=jnp.float32)
            acc = acc + iwt[h:h + 1, :] * jnp.maximum(s, 0.0)
        score = acc * idx_scale
        score = jnp.where(score == 0.0, 0.0, score)
        score = jnp.where(c * KC + krow <= qpos, score, -jnp.inf)
        keys_ref[c] = _sortable_key(score)
        return carry
    lax.fori_loop(0, n_kc, idx_body, 0)

    def row_fold(m):
        parts = [m[8 * i:8 * i + 8] for i in range(m.shape[0] // 8)]
        while len(parts) > 1:
            parts = [parts[i] + parts[i + 1] for i in range(0, len(parts), 2)]
        return parts[0]

    def count(pred):
        def body(c, acc):
            return acc + row_fold(jnp.where(pred(c, keys_ref[c]), 1.0, 0.0))
        acc = lax.fori_loop(0, n_kc, body, jnp.zeros((8, TQ), jnp.float32))
        return jnp.sum(acc, axis=0, keepdims=True)

    def count_ge(cand):
        return count(lambda c, key: key >= cand)

    kf = float(topk)
    ans0 = jnp.where(count_ge(jnp.zeros((1, TQ), jnp.int32)) >= kf, jnp.int32(0), jnp.int32(_INT_MIN))

    def bit_body(i, ans):
        cand = ans | jnp.left_shift(jnp.int32(1), 30 - i)
        return jnp.where(count_ge(cand) >= kf, cand, ans)
    thr = lax.fori_loop(0, 31, bit_body, ans0)

    n_gt = count_ge(thr + 1)
    n_ge = count_ge(thr)
    need = kf - n_gt
    tie = jnp.logical_and(n_ge - n_gt > need, thr != _FILL_KEY)
    lim_ref[...] = jnp.full((1, TQ), seq, jnp.int32)

    @pl.when(jnp.max(jnp.where(tie, 1.0, 0.0)) > 0.0)
    def _():
        nbits = max(1, int(np.ceil(np.log2(seq))))

        def bit2(i, lo):
            cand = lo | jnp.left_shift(jnp.int32(1), nbits - 1 - i)
            n_before = count(lambda c, key: jnp.logical_and(key == thr, c * KC + krow < cand))
            return jnp.where(n_before < need, cand, lo)
        lo = lax.fori_loop(0, nbits, bit2, jnp.zeros((1, TQ), jnp.int32))
        lim_ref[...] = jnp.where(tie, lo, jnp.int32(seq))

    lim = lim_ref[...]

    m_ref[...] = jnp.full(m_ref.shape, _NEG_BIG, jnp.float32)
    l_ref[...] = jnp.zeros(l_ref.shape, jnp.float32)
    acc_ref[...] = jnp.zeros(acc_ref.shape, jnp.float32)
    hd = ATT_HEAD_DIM

    def att_body(c, carry):
        vtc = vt_ref[c]
        for half in range(KC // KA):
            h0 = half * KA
            r0 = pl.multiple_of(c * KC, KC) + h0
            kpos = c * KC + h0 + krow[0:KA]
            key = keys_ref[c, h0:h0 + KA, :]
            sel = jnp.logical_or(key > thr, jnp.logical_and(key == thr, kpos <= lim))
            sel = jnp.logical_and(sel, kpos <= qpos)
            madd = jnp.where(sel, 0.0, _NEG_BIG)
            mx = []
            for h in range(ATT_HEADS):
                kc = kn_ref[h // ATT_REP, pl.ds(r0, KA), :]
                sc = lax.dot_general(kc, qs_ref[h], nt, preferred_element_type=jnp.float32) + madd
                sc_ref[h] = sc
                mx.append(jnp.max(sc, axis=0, keepdims=True))
            for h in range(ATT_HEADS):
                g = h // ATT_REP
                m_old = m_ref[h]
                m_new = jnp.maximum(m_old, mx[h])
                p = jnp.exp(sc_ref[h] - m_new)
                alpha = jnp.exp(m_old - m_new)
                l_ref[h] = alpha * l_ref[h] + jnp.sum(p, axis=0, keepdims=True)
                m_ref[h] = m_new
                pv = jnp.dot(vtc[g * hd:(g + 1) * hd, h0:h0 + KA], p.astype(bf),
                             preferred_element_type=jnp.float32)
                rs = slice((h % 2) * hd, (h % 2 + 1) * hd)
                acc_ref[h // 2, rs, :] = acc_ref[h // 2, rs, :] * alpha + pv
        return carry
    lax.fori_loop(0, n_kc, att_body, 0)

    for pair in range(ATT_HEADS // 2):
        den = jnp.concatenate([jnp.broadcast_to(l_ref[2 * pair], (hd, TQ)),
                               jnp.broadcast_to(l_ref[2 * pair + 1], (hd, TQ))], axis=0)
        o_ref[0, :, pair * 128:(pair + 1) * 128] = (acc_ref[pair] / den).T.astype(o_ref.dtype)


def dsa_attention(q, iq, iw, k, v, ik, qg, kg):
    B, S, _ = q.shape
    TQ, KC = DSA_TQ, DSA_KC
    assert S % KC == 0 and S % TQ == 0
    topk = min(IDX_TOPK_MAX, S // 4)
    kern = functools.partial(_dsa_kernel, seq=S, topk=topk)

    def pad_gain(g):
        return jnp.concatenate([g, jnp.zeros_like(g)]).reshape(1, 128)
    return pl.pallas_call(
        kern,
        grid=(B, S // TQ),
        in_specs=[
            pl.BlockSpec((1, TQ, ATT_HEADS * 128), lambda b, j: (b, j, 0)),
            pl.BlockSpec((1, TQ, IDX_HEADS * 128), lambda b, j: (b, j, 0)),
            pl.BlockSpec((1, TQ, 128), lambda b, j: (b, j, 0)),
            pl.BlockSpec((1, S, ATT_GROUPS * 128), lambda b, j: (b, 0, 0)),
            pl.BlockSpec((1, S, ATT_GROUPS * ATT_HEAD_DIM), lambda b, j: (b, 0, 0)),
            pl.BlockSpec((1, S, 128), lambda b, j: (b, 0, 0)),
            pl.BlockSpec((1, 128), lambda b, j: (0, 0)),
            pl.BlockSpec((1, 128), lambda b, j: (0, 0)),
        ],
        out_specs=pl.BlockSpec((1, TQ, ATT_HEADS * ATT_HEAD_DIM), lambda b, j: (b, j, 0)),
        out_shape=jax.ShapeDtypeStruct((B, S, ATT_HEADS * ATT_HEAD_DIM), jnp.bfloat16),
        scratch_shapes=[
            pltpu.VMEM((ATT_GROUPS, S, 128), jnp.bfloat16),
            pltpu.VMEM((S // KC, ATT_GROUPS * ATT_HEAD_DIM, KC), jnp.bfloat16),
            pltpu.VMEM((ATT_HEADS, TQ, 128), jnp.bfloat16),
            pltpu.VMEM((S // KC, KC, TQ), jnp.int32),
            pltpu.VMEM((1, TQ), jnp.int32),
            pltpu.VMEM((ATT_HEADS, 1, TQ), jnp.float32),
            pltpu.VMEM((ATT_HEADS, 1, TQ), jnp.float32),
            pltpu.VMEM((ATT_HEADS // 2, 2 * ATT_HEAD_DIM, TQ), jnp.float32),
            pltpu.VMEM((ATT_HEADS, DSA_KA, TQ), jnp.float32),
        ],
        compiler_params=pltpu.CompilerParams(
            dimension_semantics=("arbitrary", "arbitrary"),
            vmem_limit_bytes=_VMEM_LIMIT),
        name="dsa_attention",
    )(q, iq, iw, k, v, ik, pad_gain(qg), pad_gain(kg))


PEER_RT = 256
PEER_ET = 128
_ROW_CHUNKS = D_MODEL // 128
_TAB_ROWS = _ROW_CHUNKS // 2
_EDGES = PEER_HEADS * PEER_TOPK


def _extract_topk_rows(work, n_out, val_ref, pay_ref, order=None, payload=None):
    if order is None:
        order = lax.broadcasted_iota(jnp.int32, work.shape, 0)
    big = jnp.int32(2 ** 30)
    for it in range(n_out):
        m = jnp.max(work, axis=0, keepdims=True)
        am = jnp.min(jnp.where(work == m, order, big), axis=0, keepdims=True)
        hit = order == am
        val_ref[it:it + 1, :] = m
        if payload is None:
            pay_ref[it:it + 1, :] = am
        else:
            pay_ref[it:it + 1, :] = jnp.max(jnp.where(hit, payload, -1), axis=0, keepdims=True)
        work = jnp.where(hit, -jnp.inf, work)


def _peer_route_kernel(q_ref, sk_ref, e_ref, g_ref, s1_ref, i1_ref, s2_ref, i2_ref, ts_ref, te_ref, et_ref,
                       gt_ref):
    K = PEER_TOPK
    RT = q_ref.shape[0]
    b8 = lax.broadcasted_iota(jnp.int32, (8, RT), 0)
    b16 = lax.broadcasted_iota(jnp.int32, (K, RT), 0)
    order = jnp.concatenate([b16] + [a * K + b8 for a in range(1, 8)] + [(8 + b8) * K], axis=0)
    for h in range(PEER_HEADS):
        for p, (s_ref, i_ref) in enumerate(((s1_ref, i1_ref), (s2_ref, i2_ref))):
            col = (h * 2 + p) * 128
            qs = q_ref[:, col:col + 128].astype(jnp.bfloat16)
            sk = sk_ref[h, p].astype(jnp.bfloat16)
            sc = lax.dot_general(sk, qs, (((1,), (1,)), ((), ())),
                                 preferred_element_type=jnp.float32)
            _extract_topk_rows(sc, K, s_ref, i_ref)
        s2 = s2_ref[...]
        i2 = i2_ref[...]
        cand = [s1_ref[0:1, :] + s2]
        cidx = [i1_ref[0:1, :] * PEER_NKEYS + i2]
        for a in range(1, 8):
            cand.append(jnp.where(b8 < K // (a + 1), s1_ref[a:a + 1, :] + s2[0:8], -jnp.inf))
            cidx.append(i1_ref[a:a + 1, :] * PEER_NKEYS + i2[0:8])
        cand.append(s1_ref[8:K, :] + s2[0:1])
        cidx.append(i1_ref[8:K, :] * PEER_NKEYS + i2[0:1])
        _extract_topk_rows(jnp.concatenate(cand, axis=0), K, ts_ref, te_ref, order,
                           jnp.concatenate(cidx, axis=0) * _TAB_ROWS)
        ts = ts_ref[...]
        ex = jnp.exp(ts - jnp.max(ts, axis=0, keepdims=True))
        gate = ex / jnp.sum(ex, axis=0, keepdims=True)
        rep = jnp.concatenate([jnp.broadcast_to(gate[k:k + 1, :], (_ROW_CHUNKS, RT)) for k in range(K)], axis=0)
        gt_ref[:, h * 128:(h + 1) * 128] = rep.T
        et_ref[h * K:(h + 1) * K, :] = te_ref[...]
    e_ref[...] = et_ref[...].T
    g_ref[...] = gt_ref[...].reshape(RT, PEER_HEADS, 128)


def peer_route(qry, subkeys):
    T = qry.shape[0]
    RT = PEER_RT
    assert T % RT == 0
    K = PEER_TOPK
    return pl.pallas_call(
        _peer_route_kernel,
        grid=(T // RT,),
        in_specs=[
            pl.BlockSpec((RT, qry.shape[1]), lambda i: (i, 0)),
            pl.BlockSpec(subkeys.shape, lambda i: (0, 0, 0, 0)),
        ],
        out_specs=[pl.BlockSpec((RT, _EDGES), lambda i: (i, 0)),
                   pl.BlockSpec((RT, PEER_HEADS, 128), lambda i: (i, 0, 0))],
        out_shape=[jax.ShapeDtypeStruct((T, _EDGES), jnp.int32),
                   jax.ShapeDtypeStruct((T, PEER_HEADS, 128), jnp.float32)],
        scratch_shapes=[
            pltpu.VMEM((K, RT), jnp.float32), pltpu.VMEM((K, RT), jnp.int32),
            pltpu.VMEM((K, RT), jnp.float32), pltpu.VMEM((K, RT), jnp.int32),
            pltpu.VMEM((K, RT), jnp.float32), pltpu.VMEM((K, RT), jnp.int32),
            pltpu.VMEM((_EDGES, RT), jnp.int32),
            pltpu.VMEM((RT, PEER_HEADS * 128), jnp.float32),
        ],
        compiler_params=pltpu.CompilerParams(dimension_semantics=("arbitrary",)),
        name="peer_route",
    )(qry, subkeys)


def _pack_kernel(t_ref, o_ref):
    b = lax.bitcast_convert_type(t_ref[...], jnp.uint32)
    r = b + jnp.uint32(0x7FFF) + ((b >> 16) & jnp.uint32(1))
    words = [(r[:, (2 * s) * 128:(2 * s + 1) * 128] >> 16)
             | (r[:, (2 * s + 1) * 128:(2 * s + 2) * 128] & jnp.uint32(0xFFFF0000)) for s in range(_TAB_ROWS)]
    w = jnp.concatenate(words, axis=1)
    o_ref[...] = w.reshape(w.shape[0] * _TAB_ROWS, 128)


def pack_table(tab):
    n, d = tab.shape
    tr = 256
    assert n % tr == 0 and d == D_MODEL
    return pl.pallas_call(
        _pack_kernel,
        grid=(n // tr,),
        in_specs=[pl.BlockSpec((tr, d), lambda i: (i, 0))],
        out_specs=pl.BlockSpec((tr * _TAB_ROWS, 128), lambda i: (i, 0)),
        out_shape=jax.ShapeDtypeStruct((n * _TAB_ROWS, 128), jnp.uint32),
        compiler_params=pltpu.CompilerParams(dimension_semantics=("arbitrary",)),
        name="pack_table",
    )(tab)


def _gather_rows(idx_ref, tab_ref, stage_ref, t):
    R = _TAB_ROWS
    for i in range(_EDGES):
        row = pl.multiple_of(idx_ref[t, i], R)
        stage_ref[R * i:R * i + R, :] = tab_ref[pl.ds(row, R), :]


def _split_bf16(x):
    hi = x.astype(jnp.bfloat16)
    lo = (x - hi.astype(jnp.float32)).astype(jnp.bfloat16)
    return jnp.concatenate([hi, lo], axis=0)


_SLOTS = 16
_AHEAD = 2


def _pipelined_tokens(n_tok, gather, compute):
    for t in range(_AHEAD):
        gather(t, t)

    def body(tt, carry):
        t0 = _SLOTS * tt
        for u in range(_SLOTS):
            gather(jnp.minimum(t0 + u + _AHEAD, n_tok - 1), (u + _AHEAD) % _SLOTS)
            compute(t0 + u, u)
        return carry
    lax.fori_loop(0, n_tok // _SLOTS, body, 0)


def _peer_dot_kernel(idx_ref, tab_ref, x_ref, a_ref, *stages):
    n = _EDGES * _ROW_CHUNKS
    sub = lax.broadcasted_iota(jnp.int32, (2 * _ROW_CHUNKS, n), 0) % _ROW_CHUNKS
    lane = lax.broadcasted_iota(jnp.int32, (2 * _ROW_CHUNKS, n), 1)
    diag = sub == lane % _ROW_CHUNKS

    def gather(t, slot):
        _gather_rows(idx_ref, tab_ref, stages[slot], t)

    def compute(t, slot):
        m = pltpu.bitcast(stages[slot][...], jnp.bfloat16)
        x16 = _split_bf16(x_ref[t])
        r = lax.dot_general(x16, m, (((1,), (1,)), ((), ())),
                            preferred_element_type=jnp.float32)
        tot = jnp.sum(jnp.where(diag, r, 0.0), axis=0, keepdims=True)
        for g in range(n // 128):
            a_ref[t, g:g + 1, :] = tot[:, g * 128:(g + 1) * 128]

    _pipelined_tokens(a_ref.shape[0], gather, compute)


def _group_sum_kernel(p_ref, a_ref):
    a = p_ref[...]
    lane = lax.broadcasted_iota(jnp.int32, a.shape, 1)
    for sh in (1, 2, 4):
        a = a + jnp.where((lane & sh) == 0, pltpu.roll(a, 128 - sh, 1), pltpu.roll(a, sh, 1))
    a_ref[...] = a


def _group_sum(part):
    rows = part.shape[0]
    tr = 4096
    assert rows % tr == 0
    return pl.pallas_call(
        _group_sum_kernel,
        grid=(rows // tr,),
        in_specs=[pl.BlockSpec((tr, 128), lambda i: (i, 0))],
        out_specs=pl.BlockSpec((tr, 128), lambda i: (i, 0)),
        out_shape=jax.ShapeDtypeStruct(part.shape, part.dtype),
        compiler_params=pltpu.CompilerParams(dimension_semantics=("arbitrary",)),
        name="peer_group_sum",
    )(part)


def _peer_mix_kernel(idx_ref, tab_ref, a_ref, g_ref, x1_ref, g2_ref, y_ref, yc_ref, *stages):
    sub = lax.broadcasted_iota(jnp.int32, (_ROW_CHUNKS, 128), 0)
    lane = lax.broadcasted_iota(jnp.int32, (_ROW_CHUNKS, 128), 1)
    diag = sub == lane % _ROW_CHUNKS
    ngroups = _EDGES * _ROW_CHUNKS // 128

    def gather(t, slot):
        _gather_rows(idx_ref, tab_ref, stages[slot], t)

    def compute(t, slot):
        m = pltpu.bitcast(stages[slot][...], jnp.bfloat16)
        a = a_ref[t]
        c = g_ref[t] * (0.5 * a * (1.0 + lax.erf(a * (2.0 ** -0.5))))
        lhs = jnp.concatenate(
            [jnp.where(diag, jnp.broadcast_to(c[g:g + 1, :], (_ROW_CHUNKS, 128)), 0.0) for g in range(ngroups)],
            axis=1)
        r = jnp.dot(_split_bf16(lhs), m, preferred_element_type=jnp.float32)
        yc_ref[t] = x1_ref[t] + g2_ref[0] * (r[0:_ROW_CHUNKS] + r[_ROW_CHUNKS:])

    _pipelined_tokens(y_ref.shape[0], gather, compute)
    y_ref[...] = yc_ref[...].reshape(y_ref.shape)


def _expert_call(kern, name, idx, tab, ins, extra=(), extra_specs=(), rows_out=False):
    T = idx.shape[0]
    ET = PEER_ET
    assert T % ET == 0
    blk = pl.BlockSpec((ET, _ROW_CHUNKS, 128), lambda i: (i, 0, 0))
    stage = [pltpu.VMEM((_EDGES * _TAB_ROWS, 128), jnp.uint32)] * _SLOTS
    if rows_out:
        out_spec = pl.BlockSpec((ET, D_MODEL), lambda i: (i, 0))
        out_shape = jax.ShapeDtypeStruct((T, D_MODEL), jnp.float32)
        scratch = [pltpu.VMEM((ET, _ROW_CHUNKS, 128), jnp.float32)] + stage
    else:
        out_spec, out_shape, scratch = blk, jax.ShapeDtypeStruct((T, _ROW_CHUNKS, 128), jnp.float32), stage
    return pl.pallas_call(
        kern,
        grid=(T // ET,),
        in_specs=[
            pl.BlockSpec((ET, _EDGES), lambda i: (i, 0), memory_space=pltpu.SMEM),
            pl.BlockSpec(tab.shape, lambda i: (0, 0), pipeline_mode=pl.Buffered(1)),
        ] + [blk] * len(ins) + list(extra_specs),
        out_specs=out_spec,
        out_shape=out_shape,
        scratch_shapes=scratch,
        compiler_params=pltpu.CompilerParams(
            dimension_semantics=("arbitrary",),
            vmem_limit_bytes=_VMEM_LIMIT),
        name=name,
    )(idx, tab, *ins, *extra)


def peer_experts(h2c, idx, gexp, u, vtab, x1c, g2, seq):
    T = h2c.shape[0]
    ET = PEER_ET
    assert seq % ET == 0
    part = _expert_call(_peer_dot_kernel, "peer_dot", idx, pack_table(u), [h2c])
    a = _group_sum(part.reshape(T * _ROW_CHUNKS, 128)).reshape(T, _ROW_CHUNKS, 128)
    g2_spec = pl.BlockSpec((1, _ROW_CHUNKS, 128), lambda i: (i * ET // seq, 0, 0))
    return _expert_call(_peer_mix_kernel, "peer_mix", idx, pack_table(vtab), [a, gexp, x1c],
                        extra=[g2.reshape(g2.shape[0], _ROW_CHUNKS, 128)], extra_specs=[g2_spec], rows_out=True)


def kernel(x, c, w_ada, b_ada, norm1_g, norm2_g, w_in, hg_lb, hg_onorm_g, q_norm_g, k_norm_g,
           w_a, w_b, w_out, peer_wq, peer_subkeys, peer_u, peer_v):
    B, S, D = x.shape
    T = B * S
    for l in range(w_ada.shape[0]):
        mod = ada_modulation(c, w_ada[l], b_ada[l])
        sh1, sc1, g1, sh2, sc2, g2 = jnp.split(mod, 6, axis=-1)
        hg, q, k, iw, iq, ik, v, gate = in_projection(x, norm1_g[l], sc1, sh1, pack_w_in(w_in[l]))
        ya = hgrn2(hg, hg_lb, hg_onorm_g[l], l)
        yb = dsa_attention(q, iq, iw, k, v, ik, q_norm_g[l], k_norm_g[l])
        x1, h2, qry = out_projection(ya, yb, gate, x, g1, norm2_g[l], sc2, sh2,
                                     w_a[l], w_b[l], w_out[l], peer_wq[l])
        idx, gexp = peer_route(qry.reshape(T, -1), peer_subkeys[l])
        x = peer_experts(h2.reshape(T, D // 128, 128), idx, gexp, peer_u[l], peer_v[l],
                         x1.reshape(T, D // 128, 128), g2, S).reshape(B, S, D)
    return x
```

```python
import functools

import jax
import jax.numpy as jnp
import numpy as np
from jax import lax
from jax.experimental import pallas as pl
from jax.experimental.pallas import tpu as pltpu

D_MODEL = 1024
HG_HEADS = 4
HG_DK = 128
HG_WIDTH = HG_HEADS * HG_DK
HG_CHUNK = 64
ATT_HEADS = 8
ATT_GROUPS = 2
ATT_HEAD_DIM = 64
ATT_REP = ATT_HEADS // ATT_GROUPS
IDX_HEADS = 8
IDX_DIM = 64
IDX_TOPK_MAX = 256
PEER_HEADS = 8
PEER_NKEYS = 128
PEER_TOPK = 16
RMS_EPS = 1e-6
IN_SPLIT = (HG_WIDTH, HG_WIDTH, HG_WIDTH, HG_WIDTH,
            ATT_HEADS * ATT_HEAD_DIM, ATT_GROUPS * ATT_HEAD_DIM, ATT_GROUPS * ATT_HEAD_DIM,
            IDX_HEADS * IDX_DIM, IDX_DIM, IDX_HEADS,
            D_MODEL, D_MODEL)

_VMEM_LIMIT = 48 * 1024 * 1024


def _ada_kernel(c_ref, w_ref, b_ref, o_ref):
    c = c_ref[...]
    cs = c * (1.0 / (1.0 + jnp.exp(-c)))
    o_ref[...] = jnp.dot(cs, w_ref[...], preferred_element_type=jnp.float32,
                         precision=lax.Precision.HIGHEST) + b_ref[...]


def ada_modulation(c, w, b):
    B, D = c.shape
    N = w.shape[1]
    tn = 1536
    assert N % tn == 0
    return pl.pallas_call(
        _ada_kernel,
        grid=(N // tn,),
        in_specs=[pl.BlockSpec((B, D), lambda j: (0, 0)),
                  pl.BlockSpec((D, tn), lambda j: (0, j)),
                  pl.BlockSpec((1, tn), lambda j: (0, j))],
        out_specs=pl.BlockSpec((B, tn), lambda j: (0, j)),
        out_shape=jax.ShapeDtypeStruct((B, N), jnp.float32),
        compiler_params=pltpu.CompilerParams(dimension_semantics=("arbitrary",),
                                             vmem_limit_bytes=_VMEM_LIMIT),
        name="ada_modulation",
    )(c, w, b.reshape(1, N))


_PROJ_SEGS = (("hg", 2048, jnp.float32), ("q", 1024, jnp.float32), ("k", 256, jnp.float32),
              ("iw", 128, jnp.float32), ("iq", 1024, jnp.bfloat16), ("ik", 128, jnp.bfloat16),
              ("v", 128, jnp.bfloat16), ("gate", 2048, jnp.float32))
PROJ_TM = 256


def pack_w_in(w_in):
    D = w_in.shape[0]
    offs = np.cumsum((0,) + IN_SPLIT)
    hq, hf, hi, hg, aq, ak, av, iq, ik, iw, ga, gb = [w_in[:, offs[i]:offs[i + 1]] for i in range(12)]

    def pad_heads(w, nh):
        w = w.reshape(D, nh, 64)
        return jnp.concatenate([w, jnp.zeros_like(w)], axis=-1).reshape(D, nh * 128)

    iwp = jnp.concatenate([iw, jnp.zeros((D, 128 - IDX_HEADS), w_in.dtype)], axis=1)
    cols = [hq, hf, hi, hg, pad_heads(aq, ATT_HEADS), pad_heads(ak, ATT_GROUPS), iwp,
            pad_heads(iq, IDX_HEADS), pad_heads(ik, 1),
            av, ga, gb]
    return jnp.concatenate(cols, axis=1).astype(jnp.bfloat16)


def _proj_kernel(x_ref, g_ref, sc_ref, sh_ref, w_ref, *o_refs):
    x = x_ref[0]
    ms = jnp.mean(x * x, axis=-1, keepdims=True)
    h = x * lax.rsqrt(ms + RMS_EPS) * g_ref[...] * (1.0 + sc_ref[0]) + sh_ref[0]
    hb = h.astype(jnp.bfloat16)
    c0 = 0
    for (_, width, _), o_ref in zip(_PROJ_SEGS, o_refs):
        for s in range(0, width, 512):
            e = min(s + 512, width)
            o_ref[0, :, s:e] = jnp.dot(hb, w_ref[:, c0 + s:c0 + e],
                                       preferred_element_type=jnp.float32).astype(o_ref.dtype)
        c0 += width


def in_projection(x, norm_g, sc, sh, w_packed):
    B, S, D = x.shape
    tm = PROJ_TM
    assert S % tm == 0
    ntot = sum(w for _, w, _ in _PROJ_SEGS)
    assert w_packed.shape == (D, ntot)
    return pl.pallas_call(
        _proj_kernel,
        grid=(B, S // tm),
        in_specs=[
            pl.BlockSpec((1, tm, D), lambda b, i: (b, i, 0)),
            pl.BlockSpec((1, D), lambda b, i: (0, 0)),
            pl.BlockSpec((1, 1, D), lambda b, i: (b, 0, 0)),
            pl.BlockSpec((1, 1, D), lambda b, i: (b, 0, 0)),
            pl.BlockSpec((D, ntot), lambda b, i: (0, 0), pipeline_mode=pl.Buffered(1)),
        ],
        out_specs=[pl.BlockSpec((1, tm, w), lambda b, i: (b, i, 0)) for _, w, _ in _PROJ_SEGS],
        out_shape=[jax.ShapeDtypeStruct((B, S, w), dt) for _, w, dt in _PROJ_SEGS],
        compiler_params=pltpu.CompilerParams(dimension_semantics=("arbitrary", "arbitrary"),
                                             vmem_limit_bytes=_VMEM_LIMIT),
        name="in_projection",
    )(x, norm_g.reshape(1, D), sc.reshape(B, 1, D), sh.reshape(B, 1, D), w_packed)


HG_LT = 512


def _split3_bf16(x):
    a = x.astype(jnp.bfloat16)
    r = x - a.astype(jnp.float32)
    b = r.astype(jnp.bfloat16)
    c = (r - b.astype(jnp.float32)).astype(jnp.bfloat16)
    return a, b, c


def _hgrn_kernel(q_ref, f_ref, i_ref, g_ref, lb_ref, og_ref, o_ref, st_ref, *, layer):
    C = HG_CHUNK
    bf = jnp.bfloat16

    @pl.when(pl.program_id(1) == 0)
    def _():
        st_ref[...] = jnp.zeros(st_ref.shape, jnp.float32)

    lbr = lb_ref[...]
    e = jnp.exp(lbr - jnp.max(lbr, axis=0, keepdims=True))
    sm = e / jnp.sum(e, axis=0, keepdims=True)
    lb_all = jnp.sum(sm[0:layer + 1], axis=0, keepdims=True)

    r_i = lax.broadcasted_iota(jnp.int32, (C, C), 0)
    c_i = lax.broadcasted_iota(jnp.int32, (C, C), 1)
    causal = c_i <= r_i
    tri = jnp.where(causal, 1.0, 0.0).astype(bf)
    og = og_ref[...]

    def chunk(c, carry):
        r0 = pl.multiple_of(c * C, C)
        H = range(HG_HEADS)
        cols = [slice(h * HG_DK, (h + 1) * HG_DK) for h in H]
        dot = functools.partial(jnp.dot, preferred_element_type=jnp.float32)
        nt = (((1,), (1,)), ((), ()))
        tn = (((0,), (0,)), ((), ()))
        qr = [q_ref[0, pl.ds(r0, C), cs] for cs in cols]
        q = [x * (1.0 / (1.0 + jnp.exp(-x))) for x in qr]
        f = [lb_all[:, cs] + (1.0 - lb_all[:, cs]) * (1.0 / (1.0 + jnp.exp(-f_ref[0, pl.ds(r0, C), cs])))
             for cs in cols]
        k = [1.0 - x for x in f]
        vb = [i_ref[0, pl.ds(r0, C), cs].astype(bf) for cs in cols]
        lg = [_split3_bf16(jnp.log(x)) for x in f]
        b = [dot(tri, l1) + dot(tri, l2) + dot(tri, l3) for l1, l2, l3 in lg]
        bm = [x[C // 2 - 1:C // 2, :] for x in b]
        bl = [x[C - 1:C, :] for x in b]
        a = [lax.dot_general((q[h] * jnp.exp(b[h] - bm[h])).astype(bf), (k[h] * jnp.exp(bm[h] - b[h])).astype(bf),
                             nt, preferred_element_type=jnp.float32) for h in H]
        st = [st_ref[h] for h in H]
        o_inter = [lax.dot_general((q[h] * jnp.exp(b[h])).astype(bf), st[h].astype(bf), nt,
                                   preferred_element_type=jnp.float32) for h in H]
        ds = [lax.dot_general(vb[h], (k[h] * jnp.exp(bl[h] - b[h])).astype(bf), tn,
                              preferred_element_type=jnp.float32) for h in H]
        for h in H:
            st_ref[h] = st[h] * jnp.exp(bl[h]) + ds[h]
        o = [dot(jnp.where(causal, a[h], 0.0).astype(bf), vb[h]) + o_inter[h] for h in H]
        for h in H:
            on = o[h] * lax.rsqrt(jnp.mean(o[h] * o[h], axis=-1, keepdims=True) + RMS_EPS) * og
            gr = g_ref[0, pl.ds(r0, C), cols[h]]
            o_ref[0, pl.ds(r0, C), cols[h]] = (on * (gr * (1.0 / (1.0 + jnp.exp(-gr))))).astype(o_ref.dtype)
        return carry
    lax.fori_loop(0, q_ref.shape[1] // C, chunk, 0)


def hgrn2(hg, hg_lb, onorm_g, layer):
    B, S, _ = hg.shape
    LT = HG_LT
    assert S % LT == 0

    def piece(p):
        return pl.BlockSpec((1, LT, HG_WIDTH), lambda b, i, p=p: (b, i, p))
    return pl.pallas_call(
        functools.partial(_hgrn_kernel, layer=layer),
        grid=(B, S // LT),
        in_specs=[piece(0), piece(1), piece(2), piece(3),
                  pl.BlockSpec(hg_lb.shape, lambda b, i: (0, 0)),
                  pl.BlockSpec((1, HG_DK), lambda b, i: (0, 0))],
        out_specs=pl.BlockSpec((1, LT, HG_WIDTH), lambda b, i: (b, i, 0)),
        out_shape=jax.ShapeDtypeStruct((B, S, HG_WIDTH), jnp.bfloat16),
        scratch_shapes=[pltpu.VMEM((HG_HEADS, HG_DK, HG_DK), jnp.float32)],
        compiler_params=pltpu.CompilerParams(dimension_semantics=("arbitrary", "arbitrary"),
                                             vmem_limit_bytes=_VMEM_LIMIT),
        name="hgrn2",
    )(hg, hg, hg, hg, hg_lb, onorm_g.reshape(1, HG_DK))


OUT_TM = 256


def _out_kernel(ya_ref, yb_ref, ga_ref, gb_ref, x_ref, g1_ref, n2_ref, sc_ref, sh_ref,
                wa_ref, wb_ref, wo_ref, wq_ref, x1_ref, h2_ref, qry_ref):
    bf = jnp.bfloat16

    def sig(z):
        return 1.0 / (1.0 + jnp.exp(-z))
    ma = jnp.dot(ya_ref[0], wa_ref[...], preferred_element_type=jnp.float32)
    mb = jnp.dot(yb_ref[0], wb_ref[...], preferred_element_type=jnp.float32)
    merged = sig(ga_ref[0]) * ma + sig(gb_ref[0]) * mb
    x1 = x_ref[0] + g1_ref[0] * jnp.dot(merged.astype(bf), wo_ref[...], preferred_element_type=jnp.float32)
    tm, d = x1.shape
    x1_ref[0] = x1.reshape(tm, d // 128, 128)
    ms = jnp.mean(x1 * x1, axis=-1, keepdims=True)
    h2 = x1 * lax.rsqrt(ms + RMS_EPS) * n2_ref[...] * (1.0 + sc_ref[0]) + sh_ref[0]
    h2_ref[0] = h2.reshape(tm, d // 128, 128)
    qry_ref[0] = jnp.dot(h2.astype(bf), wq_ref[...], preferred_element_type=jnp.float32)


def out_projection(ya, yb, gate, x, g1, norm2_g, sc2, sh2, w_a, w_b, w_out, wq):
    B, S, D = x.shape
    tm = OUT_TM
    bf = jnp.bfloat16
    NQ = wq.shape[1]

    def tok(width, col=0):
        return pl.BlockSpec((1, tm, width), lambda b, i, col=col: (b, i, col))

    def chunked_tok():
        return pl.BlockSpec((1, tm, D // 128, 128), lambda b, i: (b, i, 0, 0))

    def per_batch():
        return pl.BlockSpec((1, 1, D), lambda b, i: (b, 0, 0))

    def const(shape):
        return pl.BlockSpec(shape, lambda b, i: (0,) * len(shape), pipeline_mode=pl.Buffered(1))
    return pl.pallas_call(
        _out_kernel,
        grid=(B, S // tm),
        in_specs=[tok(HG_WIDTH), tok(512), tok(D, 0), tok(D, 1), tok(D), per_batch(),
                  const((1, D)), per_batch(), per_batch(),
                  const(w_a.shape), const(w_b.shape), const(w_out.shape), const(wq.shape)],
        out_specs=[chunked_tok(), chunked_tok(), tok(NQ)],
        out_shape=[jax.ShapeDtypeStruct((B, S, D // 128, 128), jnp.float32),
                   jax.ShapeDtypeStruct((B, S, D // 128, 128), jnp.float32),
                   jax.ShapeDtypeStruct((B, S, NQ), jnp.float32)],
        compiler_params=pltpu.CompilerParams(dimension_semantics=("arbitrary", "arbitrary"),
                                             vmem_limit_bytes=_VMEM_LIMIT),
        name="out_projection",
    )(ya, yb, gate, gate, x, g1.reshape(B, 1, D), norm2_g.reshape(1, D), sc2.reshape(B, 1, D),
      sh2.reshape(B, 1, D), w_a.astype(bf), w_b.astype(bf), w_out.astype(bf), wq.astype(bf))


DSA_TQ = 128
DSA_KC = 512
DSA_KA = 256
_INT_MIN = -2 ** 31
_FILL_KEY = -2139095041
_NEG_BIG = -1e30


def _sortable_key(x):
    b = lax.bitcast_convert_type(x, jnp.int32)
    return b ^ ((b >> 31) & jnp.int32(0x7FFFFFFF))


def _dsa_kernel(q_ref, iq_ref, iw_ref, k_ref, v_ref, ik_ref, qg_ref, kg_ref, o_ref,
                kn_ref, vt_ref, qs_ref, keys_ref, lim_ref, m_ref, l_ref, acc_ref, sc_ref, *, seq, topk):
    TQ, KC, KA = DSA_TQ, DSA_KC, DSA_KA
    bf = jnp.bfloat16
    j = pl.program_id(1)
    n_kc = (j * TQ + TQ + KC - 1) // KC
    idx_scale = (IDX_DIM ** -0.5) * (IDX_HEADS ** -0.5)
    att_scale = ATT_HEAD_DIM ** -0.5
    nt = (((1,), (1,)), ((), ()))

    lane = lax.broadcasted_iota(jnp.int32, (1, 128), 1)
    krow = lax.broadcasted_iota(jnp.int32, (KC, 1), 0)

    def kfeat(kpos):
        a = (kpos // 64).astype(jnp.float32)
        b = (kpos % 64).astype(jnp.float32)
        return jnp.where(lane == ATT_HEAD_DIM, a, jnp.where(lane == ATT_HEAD_DIM + 1, b, 0.0))

    def qfeat(slope):
        return jnp.where(lane == ATT_HEAD_DIM, 64.0 * slope, jnp.where(lane == ATT_HEAD_DIM + 1, slope, 0.0))

    @pl.when(j == 0)
    def _():
        def body(c, carry):
            r0 = pl.multiple_of(c * KC, KC)
            for g in range(ATT_GROUPS):
                kk = k_ref[0, pl.ds(r0, KC), g * 128:(g + 1) * 128]
                ms = jnp.sum(kk * kk, axis=-1, keepdims=True) * (1.0 / ATT_HEAD_DIM)
                kn = kk * lax.rsqrt(ms + RMS_EPS) * kg_ref[...]
                kn_ref[g, pl.ds(r0, KC), :] = (kn + kfeat(r0 + krow)).astype(bf)
            vt_ref[c] = v_ref[0, pl.ds(r0, KC), :].astype(jnp.float32).T.astype(bf)
            return carry
        lax.fori_loop(0, seq // KC, body, 0)

    for h in range(ATT_HEADS):
        qq = q_ref[0, :, h * 128:(h + 1) * 128]
        ms = jnp.sum(qq * qq, axis=-1, keepdims=True) * (1.0 / ATT_HEAD_DIM)
        slope = float(2.0 ** (-8.0 * (h + 1) / ATT_HEADS))
        qs_ref[h] = (qq * lax.rsqrt(ms + RMS_EPS) * qg_ref[...] * att_scale + qfeat(slope)).astype(bf)
    iwt = iw_ref[0].T

    qpos = j * TQ + lax.broadcasted_iota(jnp.int32, (1, TQ), 1)

    def idx_body(c, carry):
        r0 = pl.multiple_of(c * KC, KC)
        ikc = ik_ref[0, pl.ds(r0, KC), :]
        acc = jnp.zeros((KC, TQ), jnp.float32)
        for h in range(IDX_HEADS):
            s = lax.dot_general(ikc, iq_ref[0, :, h * 128:(h + 1) * 128], nt,
                                preferred_element_type=jnp.float32)
            acc = acc + iwt[h:h + 1, :] * jnp.maximum(s, 0.0)
        score = acc * idx_scale
        score = jnp.where(score == 0.0, 0.0, score)
        score = jnp.where(c * KC + krow <= qpos, score, -jnp.inf)
        keys_ref[c] = _sortable_key(score)
        return carry
    lax.fori_loop(0, n_kc, idx_body, 0)

    def row_fold(m):
        parts = [m[8 * i:8 * i + 8] for i in range(m.shape[0] // 8)]
        while len(parts) > 1:
            parts = [parts[i] + parts[i + 1] for i in range(0, len(parts), 2)]
        return parts[0]

    def count(pred):
        def body(c, acc):
            return acc + row_fold(jnp.where(pred(c, keys_ref[c]), 1.0, 0.0))
        acc = lax.fori_loop(0, n_kc, body, jnp.zeros((8, TQ), jnp.float32))
        return jnp.sum(acc, axis=0, keepdims=True)

    def count_ge(cand):
        return count(lambda c, key: key >= cand)

    kf = float(topk)
    ans0 = jnp.where(count_ge(jnp.zeros((1, TQ), jnp.int32)) >= kf, jnp.int32(0), jnp.int32(_INT_MIN))

    def bit_body(i, ans):
        cand = ans | jnp.left_shift(jnp.int32(1), 30 - i)
        return jnp.where(count_ge(cand) >= kf, cand, ans)
    thr = lax.fori_loop(0, 31, bit_body, ans0)

    n_gt = count_ge(thr + 1)
    n_ge = count_ge(thr)
    need = kf - n_gt
    tie = jnp.logical_and(n_ge - n_gt > need, thr != _FILL_KEY)
    lim_ref[...] = jnp.full((1, TQ), seq, jnp.int32)

    @pl.when(jnp.max(jnp.where(tie, 1.0, 0.0)) > 0.0)
    def _():
        nbits = max(1, int(np.ceil(np.log2(seq))))

        def bit2(i, lo):
            cand = lo | jnp.left_shift(jnp.int32(1), nbits - 1 - i)
            n_before = count(lambda c, key: jnp.logical_and(key == thr, c * KC + krow < cand))
            return jnp.where(n_before < need, cand, lo)
        lo = lax.fori_loop(0, nbits, bit2, jnp.zeros((1, TQ), jnp.int32))
        lim_ref[...] = jnp.where(tie, lo, jnp.int32(seq))

    lim = lim_ref[...]

    m_ref[...] = jnp.full(m_ref.shape, _NEG_BIG, jnp.float32)
    l_ref[...] = jnp.zeros(l_ref.shape, jnp.float32)
    acc_ref[...] = jnp.zeros(acc_ref.shape, jnp.float32)
    hd = ATT_HEAD_DIM

    def att_body(c, carry):
        vtc = vt_ref[c]
        for half in range(KC // KA):
            h0 = half * KA
            r0 = pl.multiple_of(c * KC, KC) + h0
            kpos = c * KC + h0 + krow[0:KA]
            key = keys_ref[c, h0:h0 + KA, :]
            sel = jnp.logical_or(key > thr, jnp.logical_and(key == thr, kpos <= lim))
            sel = jnp.logical_and(sel, kpos <= qpos)
            madd = jnp.where(sel, 0.0, _NEG_BIG)
            mx = []
            for h in range(ATT_HEADS):
                kc = kn_ref[h // ATT_REP, pl.ds(r0, KA), :]
                sc = lax.dot_general(kc, qs_ref[h], nt, preferred_element_type=jnp.float32) + madd
                sc_ref[h] = sc
                mx.append(jnp.max(sc, axis=0, keepdims=True))
            for h in range(ATT_HEADS):
                g = h // ATT_REP
                m_old = m_ref[h]
                m_new = jnp.maximum(m_old, mx[h])
                p = jnp.exp(sc_ref[h] - m_new)
                alpha = jnp.exp(m_old - m_new)
                l_ref[h] = alpha * l_ref[h] + jnp.sum(p, axis=0, keepdims=True)
                m_ref[h] = m_new
                pv = jnp.dot(vtc[g * hd:(g + 1) * hd, h0:h0 + KA], p.astype(bf),
                             preferred_element_type=jnp.float32)
                rs = slice((h % 2) * hd, (h % 2 + 1) * hd)
                acc_ref[h // 2, rs, :] = acc_ref[h // 2, rs, :] * alpha + pv
        return carry
    lax.fori_loop(0, n_kc, att_body, 0)

    for pair in range(ATT_HEADS // 2):
        den = jnp.concatenate([jnp.broadcast_to(l_ref[2 * pair], (hd, TQ)),
                               jnp.broadcast_to(l_ref[2 * pair + 1], (hd, TQ))], axis=0)
        o_ref[0, :, pair * 128:(pair + 1) * 128] = (acc_ref[pair] / den).T.astype(o_ref.dtype)


def dsa_attention(q, iq, iw, k, v, ik, qg, kg):
    B, S, _ = q.shape
    TQ, KC = DSA_TQ, DSA_KC
    assert S % KC == 0 and S % TQ == 0
    topk = min(IDX_TOPK_MAX, S // 4)
    kern = functools.partial(_dsa_kernel, seq=S, topk=topk)

    def pad_gain(g):
        return jnp.concatenate([g, jnp.zeros_like(g)]).reshape(1, 128)
    return pl.pallas_call(
        kern,
        grid=(B, S // TQ),
        in_specs=[
            pl.BlockSpec((1, TQ, ATT_HEADS * 128), lambda b, j: (b, j, 0)),
            pl.BlockSpec((1, TQ, IDX_HEADS * 128), lambda b, j: (b, j, 0)),
            pl.BlockSpec((1, TQ, 128), lambda b, j: (b, j, 0)),
            pl.BlockSpec((1, S, ATT_GROUPS * 128), lambda b, j: (b, 0, 0)),
            pl.BlockSpec((1, S, ATT_GROUPS * ATT_HEAD_DIM), lambda b, j: (b, 0, 0)),
            pl.BlockSpec((1, S, 128), lambda b, j: (b, 0, 0)),
            pl.BlockSpec((1, 128), lambda b, j: (0, 0)),
            pl.BlockSpec((1, 128), lambda b, j: (0, 0)),
        ],
        out_specs=pl.BlockSpec((1, TQ, ATT_HEADS * ATT_HEAD_DIM), lambda b, j: (b, j, 0)),
        out_shape=jax.ShapeDtypeStruct((B, S, ATT_HEADS * ATT_HEAD_DIM), jnp.bfloat16),
        scratch_shapes=[
            pltpu.VMEM((ATT_GROUPS, S, 128), jnp.bfloat16),
            pltpu.VMEM((S // KC, ATT_GROUPS * ATT_HEAD_DIM, KC), jnp.bfloat16),
            pltpu.VMEM((ATT_HEADS, TQ, 128), jnp.bfloat16),
            pltpu.VMEM((S // KC, KC, TQ), jnp.int32),
            pltpu.VMEM((1, TQ), jnp.int32),
            pltpu.VMEM((ATT_HEADS, 1, TQ), jnp.float32),
            pltpu.VMEM((ATT_HEADS, 1, TQ), jnp.float32),
            pltpu.VMEM((ATT_HEADS // 2, 2 * ATT_HEAD_DIM, TQ), jnp.float32),
            pltpu.VMEM((ATT_HEADS, DSA_KA, TQ), jnp.float32),
        ],
        compiler_params=pltpu.CompilerParams(
            dimension_semantics=("arbitrary", "arbitrary"),
            vmem_limit_bytes=_VMEM_LIMIT),
        name="dsa_attention",
    )(q, iq, iw, k, v, ik, pad_gain(qg), pad_gain(kg))


PEER_RT = 256
PEER_ET = 128
_ROW_CHUNKS = D_MODEL // 128
_TAB_ROWS = _ROW_CHUNKS // 2
_EDGES = PEER_HEADS * PEER_TOPK


def _extract_topk_rows(work, n_out, val_ref, pay_ref, order=None, payload=None):
    if order is None:
        order = lax.broadcasted_iota(jnp.int32, work.shape, 0)
    big = jnp.int32(2 ** 30)
    for it in range(n_out):
        m = jnp.max(work, axis=0, keepdims=True)
        am = jnp.min(jnp.where(work == m, order, big), axis=0, keepdims=True)
        hit = order == am
        val_ref[it:it + 1, :] = m
        if payload is None:
            pay_ref[it:it + 1, :] = am
        else:
            pay_ref[it:it + 1, :] = jnp.max(jnp.where(hit, payload, -1), axis=0, keepdims=True)
        work = jnp.where(hit, -jnp.inf, work)


def _peer_route_kernel(q_ref, sk_ref, e_ref, g_ref, s1_ref, i1_ref, s2_ref, i2_ref, ts_ref, te_ref, et_ref,
                       gt_ref):
    K = PEER_TOPK
    RT = q_ref.shape[0]
    b8 = lax.broadcasted_iota(jnp.int32, (8, RT), 0)
    b16 = lax.broadcasted_iota(jnp.int32, (K, RT), 0)
    order = jnp.concatenate([b16] + [a * K + b8 for a in range(1, 8)] + [(8 + b8) * K], axis=0)
    for h in range(PEER_HEADS):
        for p, (s_ref, i_ref) in enumerate(((s1_ref, i1_ref), (s2_ref, i2_ref))):
            col = (h * 2 + p) * 128
            qs = q_ref[:, col:col + 128].astype(jnp.bfloat16)
            sk = sk_ref[h, p].astype(jnp.bfloat16)
            sc = lax.dot_general(sk, qs, (((1,), (1,)), ((), ())),
                                 preferred_element_type=jnp.float32)
            _extract_topk_rows(sc, K, s_ref, i_ref)
        s2 = s2_ref[...]
        i2 = i2_ref[...]
        cand = [s1_ref[0:1, :] + s2]
        cidx = [i1_ref[0:1, :] * PEER_NKEYS + i2]
        for a in range(1, 8):
            cand.append(jnp.where(b8 < K // (a + 1), s1_ref[a:a + 1, :] + s2[0:8], -jnp.inf))
            cidx.append(i1_ref[a:a + 1, :] * PEER_NKEYS + i2[0:8])
        cand.append(s1_ref[8:K, :] + s2[0:1])
        cidx.append(i1_ref[8:K, :] * PEER_NKEYS + i2[0:1])
        _extract_topk_rows(jnp.concatenate(cand, axis=0), K, ts_ref, te_ref, order,
                           jnp.concatenate(cidx, axis=0) * _TAB_ROWS)
        ts = ts_ref[...]
        ex = jnp.exp(ts - jnp.max(ts, axis=0, keepdims=True))
        gate = ex / jnp.sum(ex, axis=0, keepdims=True)
        rep = jnp.concatenate([jnp.broadcast_to(gate[k:k + 1, :], (_ROW_CHUNKS, RT)) for k in range(K)], axis=0)
        gt_ref[:, h * 128:(h + 1) * 128] = rep.T
        et_ref[h * K:(h + 1) * K, :] = te_ref[...]
    e_ref[...] = et_ref[...].T
    g_ref[...] = gt_ref[...].reshape(RT, PEER_HEADS, 128)


def peer_route(qry, subkeys):
    T = qry.shape[0]
    RT = PEER_RT
    assert T % RT == 0
    K = PEER_TOPK
    return pl.pallas_call(
        _peer_route_kernel,
        grid=(T // RT,),
        in_specs=[
            pl.BlockSpec((RT, qry.shape[1]), lambda i: (i, 0)),
            pl.BlockSpec(subkeys.shape, lambda i: (0, 0, 0, 0)),
        ],
        out_specs=[pl.BlockSpec((RT, _EDGES), lambda i: (i, 0)),
                   pl.BlockSpec((RT, PEER_HEADS, 128), lambda i: (i, 0, 0))],
        out_shape=[jax.ShapeDtypeStruct((T, _EDGES), jnp.int32),
                   jax.ShapeDtypeStruct((T, PEER_HEADS, 128), jnp.float32)],
        scratch_shapes=[
            pltpu.VMEM((K, RT), jnp.float32), pltpu.VMEM((K, RT), jnp.int32),
            pltpu.VMEM((K, RT), jnp.float32), pltpu.VMEM((K, RT), jnp.int32),
            pltpu.VMEM((K, RT), jnp.float32), pltpu.VMEM((K, RT), jnp.int32),
            pltpu.VMEM((_EDGES, RT), jnp.int32),
            pltpu.VMEM((RT, PEER_HEADS * 128), jnp.float32),
        ],
        compiler_params=pltpu.CompilerParams(dimension_semantics=("arbitrary",)),
        name="peer_route",
    )(qry, subkeys)


def _pack_kernel(t_ref, o_ref):
    b = lax.bitcast_convert_type(t_ref[...], jnp.uint32)
    r = b + jnp.uint32(0x7FFF) + ((b >> 16) & jnp.uint32(1))
    words = [(r[:, (2 * s) * 128:(2 * s + 1) * 128] >> 16)
             | (r[:, (2 * s + 1) * 128:(2 * s + 2) * 128] & jnp.uint32(0xFFFF0000)) for s in range(_TAB_ROWS)]
    w = jnp.concatenate(words, axis=1)
    o_ref[...] = w.reshape(w.shape[0] * _TAB_ROWS, 128)


def pack_table(tab):
    n, d = tab.shape
    tr = 256
    assert n % tr == 0 and d == D_MODEL
    return pl.pallas_call(
        _pack_kernel,
        grid=(n // tr,),
        in_specs=[pl.BlockSpec((tr, d), lambda i: (i, 0))],
        out_specs=pl.BlockSpec((tr * _TAB_ROWS, 128), lambda i: (i, 0)),
        out_shape=jax.ShapeDtypeStruct((n * _TAB_ROWS, 128), jnp.uint32),
        compiler_params=pltpu.CompilerParams(dimension_semantics=("arbitrary",)),
        name="pack_table",
    )(tab)


def _gather_rows(idx_ref, tab_ref, stage_ref, t):
    R = _TAB_ROWS
    for i in range(_EDGES):
        row = pl.multiple_of(idx_ref[t, i], R)
        stage_ref[R * i:R * i + R, :] = tab_ref[pl.ds(row, R), :]


def _split_bf16(x):
    hi = x.astype(jnp.bfloat16)
    lo = (x - hi.astype(jnp.float32)).astype(jnp.bfloat16)
    return jnp.concatenate([hi, lo], axis=0)


_SLOTS = 16
_AHEAD = 2


def _pipelined_tokens(n_tok, gather, compute):
    for t in range(_AHEAD):
        gather(t, t)

    def body(tt, carry):
        t0 = _SLOTS * tt
        for u in range(_SLOTS):
            gather(jnp.minimum(t0 + u + _AHEAD, n_tok - 1), (u + _AHEAD) % _SLOTS)
            compute(t0 + u, u)
        return carry
    lax.fori_loop(0, n_tok // _SLOTS, body, 0)


def _peer_dot_kernel(idx_ref, tab_ref, x_ref, a_ref, *stages):
    n = _EDGES * _ROW_CHUNKS
    sub = lax.broadcasted_iota(jnp.int32, (2 * _ROW_CHUNKS, n), 0) % _ROW_CHUNKS
    lane = lax.broadcasted_iota(jnp.int32, (2 * _ROW_CHUNKS, n), 1)
    diag = sub == lane % _ROW_CHUNKS

    def gather(t, slot):
        _gather_rows(idx_ref, tab_ref, stages[slot], t)

    def compute(t, slot):
        m = pltpu.bitcast(stages[slot][...], jnp.bfloat16)
        x16 = _split_bf16(x_ref[t])
        r = lax.dot_general(x16, m, (((1,), (1,)), ((), ())),
                            preferred_element_type=jnp.float32)
        tot = jnp.sum(jnp.where(diag, r, 0.0), axis=0, keepdims=True)
        for g in range(n // 128):
            a_ref[t, g:g + 1, :] = tot[:, g * 128:(g + 1) * 128]

    _pipelined_tokens(a_ref.shape[0], gather, compute)

    r_i = lax.broadcasted_iota(jnp.int32, (128, 128), 0) // _ROW_CHUNKS
    c_i = lax.broadcasted_iota(jnp.int32, (128, 128), 1) // _ROW_CHUNKS
    ones_bd = jnp.where(r_i == c_i, 1.0, 0.0).astype(jnp.bfloat16)
    a = a_ref[...].reshape(a_ref.shape[0] * _ROW_CHUNKS, 128)
    a = sum(jnp.dot(piece, ones_bd, preferred_element_type=jnp.float32) for piece in _split3_bf16(a))
    a_ref[...] = a.reshape(a_ref.shape)


def _peer_mix_kernel(idx_ref, tab_ref, a_ref, g_ref, x1_ref, g2_ref, y_ref, yc_ref, *stages):
    sub = lax.broadcasted_iota(jnp.int32, (_ROW_CHUNKS, 128), 0)
    lane = lax.broadcasted_iota(jnp.int32, (_ROW_CHUNKS, 128), 1)
    diag = sub == lane % _ROW_CHUNKS
    ngroups = _EDGES * _ROW_CHUNKS // 128

    def gather(t, slot):
        _gather_rows(idx_ref, tab_ref, stages[slot], t)

    def compute(t, slot):
        m = pltpu.bitcast(stages[slot][...], jnp.bfloat16)
        a = a_ref[t]
        c = g_ref[t] * (0.5 * a * (1.0 + lax.erf(a * (2.0 ** -0.5))))
        lhs = jnp.concatenate(
            [jnp.where(diag, jnp.broadcast_to(c[g:g + 1, :], (_ROW_CHUNKS, 128)), 0.0) for g in range(ngroups)],
            axis=1)
        r = jnp.dot(_split_bf16(lhs), m, preferred_element_type=jnp.float32)
        yc_ref[t] = x1_ref[t] + g2_ref[0] * (r[0:_ROW_CHUNKS] + r[_ROW_CHUNKS:])

    _pipelined_tokens(y_ref.shape[0], gather, compute)
    y_ref[...] = yc_ref[...].reshape(y_ref.shape)


def _expert_call(kern, name, idx, tab, ins, extra=(), extra_specs=(), rows_out=False):
    T = idx.shape[0]
    ET = PEER_ET
    assert T % ET == 0
    blk = pl.BlockSpec((ET, _ROW_CHUNKS, 128), lambda i: (i, 0, 0))
    stage = [pltpu.VMEM((_EDGES * _TAB_ROWS, 128), jnp.uint32)] * _SLOTS
    if rows_out:
        out_spec = pl.BlockSpec((ET, D_MODEL), lambda i: (i, 0))
        out_shape = jax.ShapeDtypeStruct((T, D_MODEL), jnp.float32)
        scratch = [pltpu.VMEM((ET, _ROW_CHUNKS, 128), jnp.float32)] + stage
    else:
        out_spec, out_shape, scratch = blk, jax.ShapeDtypeStruct((T, _ROW_CHUNKS, 128), jnp.float32), stage
    return pl.pallas_call(
        kern,
        grid=(T // ET,),
        in_specs=[
            pl.BlockSpec((ET, _EDGES), lambda i: (i, 0), memory_space=pltpu.SMEM),
            pl.BlockSpec(tab.shape, lambda i: (0, 0), pipeline_mode=pl.Buffered(1)),
        ] + [blk] * len(ins) + list(extra_specs),
        out_specs=out_spec,
        out_shape=out_shape,
        scratch_shapes=scratch,
        compiler_params=pltpu.CompilerParams(
            dimension_semantics=("arbitrary",),
            vmem_limit_bytes=_VMEM_LIMIT),
        name=name,
    )(idx, tab, *ins, *extra)


def peer_experts(h2c, idx, gexp, u, vtab, x1c, g2, seq):
    T = h2c.shape[0]
    ET = PEER_ET
    assert seq % ET == 0
    a = _expert_call(_peer_dot_kernel, "peer_dot", idx, pack_table(u), [h2c])
    g2_spec = pl.BlockSpec((1, _ROW_CHUNKS, 128), lambda i: (i * ET // seq, 0, 0))
    return _expert_call(_peer_mix_kernel, "peer_mix", idx, pack_table(vtab), [a, gexp, x1c],
                        extra=[g2.reshape(g2.shape[0], _ROW_CHUNKS, 128)], extra_specs=[g2_spec], rows_out=True)


def kernel(x, c, w_ada, b_ada, norm1_g, norm2_g, w_in, hg_lb, hg_onorm_g, q_norm_g, k_norm_g,
           w_a, w_b, w_out, peer_wq, peer_subkeys, peer_u, peer_v):
    B, S, D = x.shape
    T = B * S
    for l in range(w_ada.shape[0]):
        mod = ada_modulation(c, w_ada[l], b_ada[l])
        sh1, sc1, g1, sh2, sc2, g2 = jnp.split(mod, 6, axis=-1)
        hg, q, k, iw, iq, ik, v, gate = in_projection(x, norm1_g[l], sc1, sh1, pack_w_in(w_in[l]))
        ya = hgrn2(hg, hg_lb, hg_onorm_g[l], l)
        yb = dsa_attention(q, iq, iw, k, v, ik, q_norm_g[l], k_norm_g[l])
        x1, h2, qry = out_projection(ya, yb, gate, x, g1, norm2_g[l], sc2, sh2,
                                     w_a[l], w_b[l], w_out[l], peer_wq[l])
        idx, gexp = peer_route(qry.reshape(T, -1), peer_subkeys[l])
        x = peer_experts(h2.reshape(T, D // 128, 128), idx, gexp, peer_u[l], peer_v[l],
                         x1.reshape(T, D // 128, 128), g2, S).reshape(B, S, D)
    return x
```

```python
import functools

import jax
import jax.numpy as jnp
import numpy as np
from jax import lax
from jax.experimental import pallas as pl
from jax.experimental.pallas import tpu as pltpu

D_MODEL = 1024
HG_HEADS = 4
HG_DK = 128
HG_WIDTH = HG_HEADS * HG_DK
HG_CHUNK = 64
ATT_HEADS = 8
ATT_GROUPS = 2
ATT_HEAD_DIM = 64
ATT_REP = ATT_HEADS // ATT_GROUPS
IDX_HEADS = 8
IDX_DIM = 64
IDX_TOPK_MAX = 256
PEER_HEADS = 8
PEER_NKEYS = 128
PEER_TOPK = 16
RMS_EPS = 1e-6
IN_SPLIT = (HG_WIDTH, HG_WIDTH, HG_WIDTH, HG_WIDTH,
            ATT_HEADS * ATT_HEAD_DIM, ATT_GROUPS * ATT_HEAD_DIM, ATT_GROUPS * ATT_HEAD_DIM,
            IDX_HEADS * IDX_DIM, IDX_DIM, IDX_HEADS,
            D_MODEL, D_MODEL)

_VMEM_LIMIT = 48 * 1024 * 1024


def _ada_kernel(c_ref, w_ref, b_ref, o_ref):
    c = c_ref[...]
    cs = c * (1.0 / (1.0 + jnp.exp(-c)))
    o_ref[...] = jnp.dot(cs, w_ref[...], preferred_element_type=jnp.float32,
                         precision=lax.Precision.HIGHEST) + b_ref[...]


def ada_modulation(c, w, b):
    B, D = c.shape
    N = w.shape[1]
    tn = 1536
    assert N % tn == 0
    return pl.pallas_call(
        _ada_kernel,
        grid=(N // tn,),
        in_specs=[pl.BlockSpec((B, D), lambda j: (0, 0)),
                  pl.BlockSpec((D, tn), lambda j: (0, j)),
                  pl.BlockSpec((1, tn), lambda j: (0, j))],
        out_specs=pl.BlockSpec((B, tn), lambda j: (0, j)),
        out_shape=jax.ShapeDtypeStruct((B, N), jnp.float32),
        compiler_params=pltpu.CompilerParams(dimension_semantics=("arbitrary",),
                                             vmem_limit_bytes=_VMEM_LIMIT),
        name="ada_modulation",
    )(c, w, b.reshape(1, N))


_PROJ_SEGS = (("hg", 2048, jnp.float32), ("q", 1024, jnp.float32), ("k", 256, jnp.float32),
              ("iw", 128, jnp.float32), ("iq", 1024, jnp.bfloat16), ("ik", 128, jnp.bfloat16),
              ("v", 128, jnp.bfloat16), ("gate", 2048, jnp.float32))
PROJ_TM = 256


def pack_w_in(w_in):
    D = w_in.shape[0]
    offs = np.cumsum((0,) + IN_SPLIT)
    hq, hf, hi, hg, aq, ak, av, iq, ik, iw, ga, gb = [w_in[:, offs[i]:offs[i + 1]] for i in range(12)]

    def pad_heads(w, nh):
        w = w.reshape(D, nh, 64)
        return jnp.concatenate([w, jnp.zeros_like(w)], axis=-1).reshape(D, nh * 128)

    iwp = jnp.concatenate([iw, jnp.zeros((D, 128 - IDX_HEADS), w_in.dtype)], axis=1)
    cols = [hq, hf, hi, hg, pad_heads(aq, ATT_HEADS), pad_heads(ak, ATT_GROUPS), iwp,
            pad_heads(iq, IDX_HEADS), pad_heads(ik, 1),
            av, ga, gb]
    return jnp.concatenate(cols, axis=1).astype(jnp.bfloat16)


def _proj_kernel(x_ref, g_ref, sc_ref, sh_ref, w_ref, *o_refs):
    x = x_ref[0]
    ms = jnp.mean(x * x, axis=-1, keepdims=True)
    h = x * lax.rsqrt(ms + RMS_EPS) * g_ref[...] * (1.0 + sc_ref[0]) + sh_ref[0]
    hb = h.astype(jnp.bfloat16)
    c0 = 0
    for (_, width, _), o_ref in zip(_PROJ_SEGS, o_refs):
        for s in range(0, width, 512):
            e = min(s + 512, width)
            o_ref[0, :, s:e] = jnp.dot(hb, w_ref[:, c0 + s:c0 + e],
                                       preferred_element_type=jnp.float32).astype(o_ref.dtype)
        c0 += width


def in_projection(x, norm_g, sc, sh, w_packed):
    B, S, D = x.shape
    tm = PROJ_TM
    assert S % tm == 0
    ntot = sum(w for _, w, _ in _PROJ_SEGS)
    assert w_packed.shape == (D, ntot)
    return pl.pallas_call(
        _proj_kernel,
        grid=(B, S // tm),
        in_specs=[
            pl.BlockSpec((1, tm, D), lambda b, i: (b, i, 0)),
            pl.BlockSpec((1, D), lambda b, i: (0, 0)),
            pl.BlockSpec((1, 1, D), lambda b, i: (b, 0, 0)),
            pl.BlockSpec((1, 1, D), lambda b, i: (b, 0, 0)),
            pl.BlockSpec((D, ntot), lambda b, i: (0, 0), pipeline_mode=pl.Buffered(1)),
        ],
        out_specs=[pl.BlockSpec((1, tm, w), lambda b, i: (b, i, 0)) for _, w, _ in _PROJ_SEGS],
        out_shape=[jax.ShapeDtypeStruct((B, S, w), dt) for _, w, dt in _PROJ_SEGS],
        compiler_params=pltpu.CompilerParams(dimension_semantics=("arbitrary", "arbitrary"),
                                             vmem_limit_bytes=_VMEM_LIMIT),
        name="in_projection",
    )(x, norm_g.reshape(1, D), sc.reshape(B, 1, D), sh.reshape(B, 1, D), w_packed)


HG_LT = 512


def _split3_bf16(x):
    a = x.astype(jnp.bfloat16)
    r = x - a.astype(jnp.float32)
    b = r.astype(jnp.bfloat16)
    c = (r - b.astype(jnp.float32)).astype(jnp.bfloat16)
    return a, b, c


def _hgrn_kernel(q_ref, f_ref, i_ref, g_ref, lb_ref, og_ref, o_ref, st_ref, *, layer):
    C = HG_CHUNK
    bf = jnp.bfloat16

    @pl.when(pl.program_id(1) == 0)
    def _():
        st_ref[...] = jnp.zeros(st_ref.shape, jnp.float32)

    lbr = lb_ref[...]
    e = jnp.exp(lbr - jnp.max(lbr, axis=0, keepdims=True))
    sm = e / jnp.sum(e, axis=0, keepdims=True)
    lb_all = jnp.sum(sm[0:layer + 1], axis=0, keepdims=True)

    r_i = lax.broadcasted_iota(jnp.int32, (C, C), 0)
    c_i = lax.broadcasted_iota(jnp.int32, (C, C), 1)
    causal = c_i <= r_i
    tri = jnp.where(causal, 1.0, 0.0).astype(bf)
    og = og_ref[...]

    def chunk(c, carry):
        r0 = pl.multiple_of(c * C, C)
        H = range(HG_HEADS)
        cols = [slice(h * HG_DK, (h + 1) * HG_DK) for h in H]
        dot = functools.partial(jnp.dot, preferred_element_type=jnp.float32)
        nt = (((1,), (1,)), ((), ()))
        tn = (((0,), (0,)), ((), ()))
        qr = [q_ref[0, pl.ds(r0, C), cs] for cs in cols]
        q = [x * (1.0 / (1.0 + jnp.exp(-x))) for x in qr]
        f = [lb_all[:, cs] + (1.0 - lb_all[:, cs]) * (1.0 / (1.0 + jnp.exp(-f_ref[0, pl.ds(r0, C), cs])))
             for cs in cols]
        k = [1.0 - x for x in f]
        vb = [i_ref[0, pl.ds(r0, C), cs].astype(bf) for cs in cols]
        lg = [_split3_bf16(jnp.log(x)) for x in f]
        b = [dot(tri, l1) + dot(tri, l2) + dot(tri, l3) for l1, l2, l3 in lg]
        bm = [x[C // 2 - 1:C // 2, :] for x in b]
        bl = [x[C - 1:C, :] for x in b]
        a = [lax.dot_general((q[h] * jnp.exp(b[h] - bm[h])).astype(bf), (k[h] * jnp.exp(bm[h] - b[h])).astype(bf),
                             nt, preferred_element_type=jnp.float32) for h in H]
        st = [st_ref[h] for h in H]
        o_inter = [lax.dot_general((q[h] * jnp.exp(b[h])).astype(bf), st[h].astype(bf), nt,
                                   preferred_element_type=jnp.float32) for h in H]
        ds = [lax.dot_general(vb[h], (k[h] * jnp.exp(bl[h] - b[h])).astype(bf), tn,
                              preferred_element_type=jnp.float32) for h in H]
        for h in H:
            st_ref[h] = st[h] * jnp.exp(bl[h]) + ds[h]
        o = [dot(jnp.where(causal, a[h], 0.0).astype(bf), vb[h]) + o_inter[h] for h in H]
        for h in H:
            on = o[h] * lax.rsqrt(jnp.mean(o[h] * o[h], axis=-1, keepdims=True) + RMS_EPS) * og
            gr = g_ref[0, pl.ds(r0, C), cols[h]]
            o_ref[0, pl.ds(r0, C), cols[h]] = (on * (gr * (1.0 / (1.0 + jnp.exp(-gr))))).astype(o_ref.dtype)
        return carry
    lax.fori_loop(0, q_ref.shape[1] // C, chunk, 0)


def hgrn2(hg, hg_lb, onorm_g, layer):
    B, S, _ = hg.shape
    LT = HG_LT
    assert S % LT == 0

    def piece(p):
        return pl.BlockSpec((1, LT, HG_WIDTH), lambda b, i, p=p: (b, i, p))
    return pl.pallas_call(
        functools.partial(_hgrn_kernel, layer=layer),
        grid=(B, S // LT),
        in_specs=[piece(0), piece(1), piece(2), piece(3),
                  pl.BlockSpec(hg_lb.shape, lambda b, i: (0, 0)),
                  pl.BlockSpec((1, HG_DK), lambda b, i: (0, 0))],
        out_specs=pl.BlockSpec((1, LT, HG_WIDTH), lambda b, i: (b, i, 0)),
        out_shape=jax.ShapeDtypeStruct((B, S, HG_WIDTH), jnp.bfloat16),
        scratch_shapes=[pltpu.VMEM((HG_HEADS, HG_DK, HG_DK), jnp.float32)],
        compiler_params=pltpu.CompilerParams(dimension_semantics=("arbitrary", "arbitrary"),
                                             vmem_limit_bytes=_VMEM_LIMIT),
        name="hgrn2",
    )(hg, hg, hg, hg, hg_lb, onorm_g.reshape(1, HG_DK))


OUT_TM = 256


def _out_kernel(ya_ref, yb_ref, ga_ref, gb_ref, x_ref, g1_ref, n2_ref, sc_ref, sh_ref,
                wa_ref, wb_ref, wo_ref, wq_ref, x1_ref, h2_ref, qry_ref):
    bf = jnp.bfloat16

    def sig(z):
        return 1.0 / (1.0 + jnp.exp(-z))
    ma = jnp.dot(ya_ref[0], wa_ref[...], preferred_element_type=jnp.float32)
    mb = jnp.dot(yb_ref[0], wb_ref[...], preferred_element_type=jnp.float32)
    merged = sig(ga_ref[0]) * ma + sig(gb_ref[0]) * mb
    x1 = x_ref[0] + g1_ref[0] * jnp.dot(merged.astype(bf), wo_ref[...], preferred_element_type=jnp.float32)
    tm, d = x1.shape
    x1_ref[0] = x1.reshape(tm, d // 128, 128)
    ms = jnp.mean(x1 * x1, axis=-1, keepdims=True)
    h2 = x1 * lax.rsqrt(ms + RMS_EPS) * n2_ref[...] * (1.0 + sc_ref[0]) + sh_ref[0]
    h2_ref[0] = h2.reshape(tm, d // 128, 128)
    qry_ref[0] = jnp.dot(h2.astype(bf), wq_ref[...], preferred_element_type=jnp.float32)


def out_projection(ya, yb, gate, x, g1, norm2_g, sc2, sh2, w_a, w_b, w_out, wq):
    B, S, D = x.shape
    tm = OUT_TM
    bf = jnp.bfloat16
    NQ = wq.shape[1]

    def tok(width, col=0):
        return pl.BlockSpec((1, tm, width), lambda b, i, col=col: (b, i, col))

    def chunked_tok():
        return pl.BlockSpec((1, tm, D // 128, 128), lambda b, i: (b, i, 0, 0))

    def per_batch():
        return pl.BlockSpec((1, 1, D), lambda b, i: (b, 0, 0))

    def const(shape):
        return pl.BlockSpec(shape, lambda b, i: (0,) * len(shape), pipeline_mode=pl.Buffered(1))
    return pl.pallas_call(
        _out_kernel,
        grid=(B, S // tm),
        in_specs=[tok(HG_WIDTH), tok(512), tok(D, 0), tok(D, 1), tok(D), per_batch(),
                  const((1, D)), per_batch(), per_batch(),
                  const(w_a.shape), const(w_b.shape), const(w_out.shape), const(wq.shape)],
        out_specs=[chunked_tok(), chunked_tok(), tok(NQ)],
        out_shape=[jax.ShapeDtypeStruct((B, S, D // 128, 128), jnp.float32),
                   jax.ShapeDtypeStruct((B, S, D // 128, 128), jnp.float32),
                   jax.ShapeDtypeStruct((B, S, NQ), jnp.float32)],
        compiler_params=pltpu.CompilerParams(dimension_semantics=("arbitrary", "arbitrary"),
                                             vmem_limit_bytes=_VMEM_LIMIT),
        name="out_projection",
    )(ya, yb, gate, gate, x, g1.reshape(B, 1, D), norm2_g.reshape(1, D), sc2.reshape(B, 1, D),
      sh2.reshape(B, 1, D), w_a.astype(bf), w_b.astype(bf), w_out.astype(bf), wq.astype(bf))


DSA_TQ = 128
DSA_KC = 512
DSA_KA = 512
_FILL_KEY = -2139095041
_NEG_BIG = -1e30


def _sortable_key(x):
    b = lax.bitcast_convert_type(x, jnp.int32)
    return b ^ ((b >> 31) & jnp.int32(0x7FFFFFFF))


def _dsa_kernel(q_ref, iq_ref, iw_ref, k_ref, v_ref, ik_ref, qg_ref, kg_ref, o_ref,
                kn_ref, vt_ref, qs_ref, keys_ref, khi_ref, klo_ref, lim_ref, m_ref, l_ref, acc_ref, sc_ref,
                *, seq, topk):
    TQ, KC, KA = DSA_TQ, DSA_KC, DSA_KA
    bf = jnp.bfloat16
    j = pl.program_id(1)
    n_kc = (j * TQ + TQ + KC - 1) // KC
    idx_scale = (IDX_DIM ** -0.5) * (IDX_HEADS ** -0.5)
    att_scale = ATT_HEAD_DIM ** -0.5
    nt = (((1,), (1,)), ((), ()))

    lane = lax.broadcasted_iota(jnp.int32, (1, 128), 1)
    krow = lax.broadcasted_iota(jnp.int32, (KC, 1), 0)

    def kfeat(kpos):
        a = (kpos // 64).astype(jnp.float32)
        b = (kpos % 64).astype(jnp.float32)
        return jnp.where(lane == ATT_HEAD_DIM, a, jnp.where(lane == ATT_HEAD_DIM + 1, b, 0.0))

    def qfeat(slope):
        return jnp.where(lane == ATT_HEAD_DIM, 64.0 * slope, jnp.where(lane == ATT_HEAD_DIM + 1, slope, 0.0))

    @pl.when(j == 0)
    def _():
        def body(c, carry):
            r0 = pl.multiple_of(c * KC, KC)
            for g in range(ATT_GROUPS):
                kk = k_ref[0, pl.ds(r0, KC), g * 128:(g + 1) * 128]
                ms = jnp.sum(kk * kk, axis=-1, keepdims=True) * (1.0 / ATT_HEAD_DIM)
                kn = kk * lax.rsqrt(ms + RMS_EPS) * kg_ref[...]
                kn_ref[g, pl.ds(r0, KC), :] = (kn + kfeat(r0 + krow)).astype(bf)
            vt_ref[c] = v_ref[0, pl.ds(r0, KC), :].astype(jnp.float32).T.astype(bf)
            return carry
        lax.fori_loop(0, seq // KC, body, 0)

    for h in range(ATT_HEADS):
        qq = q_ref[0, :, h * 128:(h + 1) * 128]
        ms = jnp.sum(qq * qq, axis=-1, keepdims=True) * (1.0 / ATT_HEAD_DIM)
        slope = float(2.0 ** (-8.0 * (h + 1) / ATT_HEADS))
        qs_ref[h] = (qq * lax.rsqrt(ms + RMS_EPS) * qg_ref[...] * att_scale + qfeat(slope)).astype(bf)
    iwt = iw_ref[0].T

    qpos = j * TQ + lax.broadcasted_iota(jnp.int32, (1, TQ), 1)

    def idx_body(c, carry):
        r0 = pl.multiple_of(c * KC, KC)
        ikc = ik_ref[0, pl.ds(r0, KC), :]
        acc = jnp.zeros((KC, TQ), jnp.float32)
        for h in range(IDX_HEADS):
            s = lax.dot_general(ikc, iq_ref[0, :, h * 128:(h + 1) * 128], nt,
                                preferred_element_type=jnp.float32)
            acc = acc + iwt[h:h + 1, :] * jnp.maximum(s, 0.0)
        score = acc * idx_scale
        score = jnp.where(score == 0.0, 0.0, score)
        score = jnp.where(c * KC + krow <= qpos, score, -jnp.inf)
        key = _sortable_key(score)
        keys_ref[c] = key
        khi_ref[c] = (key >> 16).astype(jnp.int16)
        klo_ref[c] = ((key & 0xFFFF) - 2 ** 15).astype(jnp.int16)
        return carry
    lax.fori_loop(0, n_kc, idx_body, 0)

    def fold(m, rows):
        parts = [m[rows * i:rows * (i + 1)] for i in range(m.shape[0] // rows)]
        while len(parts) > 1:
            parts = [parts[i] + parts[i + 1] for i in range(0, len(parts), 2)]
        return parts[0]

    def count(pred):
        def body(c, acc):
            return acc + fold(jnp.where(pred(c, keys_ref[c]), 1.0, 0.0), 8)
        acc = lax.fori_loop(0, n_kc, body, jnp.zeros((8, TQ), jnp.float32))
        return jnp.sum(acc, axis=0, keepdims=True)

    def count16(ref, cand):
        c16 = cand.astype(jnp.int16)

        def body(c, acc):
            hit = jnp.where(ref[c] >= c16, jnp.int16(1), jnp.int16(0))
            return acc + fold(hit, 16).astype(jnp.float32)
        acc = lax.fori_loop(0, n_kc, body, jnp.zeros((16, TQ), jnp.float32))
        return jnp.sum(acc, axis=0, keepdims=True)

    def search16(ref, base):
        ans0 = jnp.where(base + count16(ref, jnp.zeros((1, TQ), jnp.int32)) >= kf, jnp.int32(0), jnp.int32(-2 ** 15))

        def bit_body(i, ans):
            cand = ans | jnp.left_shift(jnp.int32(1), 14 - i)
            return jnp.where(base + count16(ref, cand) >= kf, cand, ans)
        return lax.fori_loop(0, 15, bit_body, ans0)

    def count_ge(cand):
        return count(lambda c, key: key >= cand)

    kf = float(topk)
    thr_hi = search16(khi_ref, jnp.zeros((1, TQ), jnp.float32))
    n_hi_gt = count16(khi_ref, thr_hi + 1)
    hi16 = thr_hi.astype(jnp.int16)

    def mask_low(c, carry):
        klo_ref[c] = jnp.where(khi_ref[c] == hi16, klo_ref[c], jnp.int16(-2 ** 15))
        return carry
    lax.fori_loop(0, n_kc, mask_low, 0)
    thr_lo = search16(klo_ref, n_hi_gt)
    thr = jnp.left_shift(thr_hi, 16) | (thr_lo + 2 ** 15)

    n_gt = count_ge(thr + 1)
    n_ge = count_ge(thr)
    need = kf - n_gt
    tie = jnp.logical_and(n_ge - n_gt > need, thr != _FILL_KEY)
    lim_ref[...] = jnp.full((1, TQ), seq, jnp.int32)

    @pl.when(jnp.max(jnp.where(tie, 1.0, 0.0)) > 0.0)
    def _():
        nbits = max(1, int(np.ceil(np.log2(seq))))

        def bit2(i, lo):
            cand = lo | jnp.left_shift(jnp.int32(1), nbits - 1 - i)
            n_before = count(lambda c, key: jnp.logical_and(key == thr, c * KC + krow < cand))
            return jnp.where(n_before < need, cand, lo)
        lo = lax.fori_loop(0, nbits, bit2, jnp.zeros((1, TQ), jnp.int32))
        lim_ref[...] = jnp.where(tie, lo, jnp.int32(seq))

    lim = lim_ref[...]

    m_ref[...] = jnp.full(m_ref.shape, _NEG_BIG, jnp.float32)
    l_ref[...] = jnp.zeros(l_ref.shape, jnp.float32)
    acc_ref[...] = jnp.zeros(acc_ref.shape, jnp.float32)
    hd = ATT_HEAD_DIM

    def att_body(c, carry):
        vtc = vt_ref[c]
        for half in range(KC // KA):
            h0 = half * KA
            r0 = pl.multiple_of(c * KC, KC) + h0
            kpos = c * KC + h0 + krow[0:KA]
            key = keys_ref[c, h0:h0 + KA, :]
            sel = jnp.logical_or(key > thr, jnp.logical_and(key == thr, kpos <= lim))
            sel = jnp.logical_and(sel, kpos <= qpos)
            madd = jnp.where(sel, 0.0, _NEG_BIG)
            mx = []
            for h in range(ATT_HEADS):
                kc = kn_ref[h // ATT_REP, pl.ds(r0, KA), :]
                sc = lax.dot_general(kc, qs_ref[h], nt, preferred_element_type=jnp.float32) + madd
                sc_ref[h] = sc
                mx.append(jnp.max(sc, axis=0, keepdims=True))
            for h in range(ATT_HEADS):
                g = h // ATT_REP
                m_old = m_ref[h]
                m_new = jnp.maximum(m_old, mx[h])
                p = jnp.exp(sc_ref[h] - m_new)
                alpha = jnp.exp(m_old - m_new)
                l_ref[h] = alpha * l_ref[h] + jnp.sum(p, axis=0, keepdims=True)
                m_ref[h] = m_new
                pv = jnp.dot(vtc[g * hd:(g + 1) * hd, h0:h0 + KA], p.astype(bf),
                             preferred_element_type=jnp.float32)
                rs = slice((h % 2) * hd, (h % 2 + 1) * hd)
                acc_ref[h // 2, rs, :] = acc_ref[h // 2, rs, :] * alpha + pv
        return carry
    lax.fori_loop(0, n_kc, att_body, 0)

    for pair in range(ATT_HEADS // 2):
        den = jnp.concatenate([jnp.broadcast_to(l_ref[2 * pair], (hd, TQ)),
                               jnp.broadcast_to(l_ref[2 * pair + 1], (hd, TQ))], axis=0)
        o_ref[0, :, pair * 128:(pair + 1) * 128] = (acc_ref[pair] / den).T.astype(o_ref.dtype)


def dsa_attention(q, iq, iw, k, v, ik, qg, kg):
    B, S, _ = q.shape
    TQ, KC = DSA_TQ, DSA_KC
    assert S % KC == 0 and S % TQ == 0
    topk = min(IDX_TOPK_MAX, S // 4)
    kern = functools.partial(_dsa_kernel, seq=S, topk=topk)

    def pad_gain(g):
        return jnp.concatenate([g, jnp.zeros_like(g)]).reshape(1, 128)
    return pl.pallas_call(
        kern,
        grid=(B, S // TQ),
        in_specs=[
            pl.BlockSpec((1, TQ, ATT_HEADS * 128), lambda b, j: (b, j, 0)),
            pl.BlockSpec((1, TQ, IDX_HEADS * 128), lambda b, j: (b, j, 0)),
            pl.BlockSpec((1, TQ, 128), lambda b, j: (b, j, 0)),
            pl.BlockSpec((1, S, ATT_GROUPS * 128), lambda b, j: (b, 0, 0)),
            pl.BlockSpec((1, S, ATT_GROUPS * ATT_HEAD_DIM), lambda b, j: (b, 0, 0)),
            pl.BlockSpec((1, S, 128), lambda b, j: (b, 0, 0)),
            pl.BlockSpec((1, 128), lambda b, j: (0, 0)),
            pl.BlockSpec((1, 128), lambda b, j: (0, 0)),
        ],
        out_specs=pl.BlockSpec((1, TQ, ATT_HEADS * ATT_HEAD_DIM), lambda b, j: (b, j, 0)),
        out_shape=jax.ShapeDtypeStruct((B, S, ATT_HEADS * ATT_HEAD_DIM), jnp.bfloat16),
        scratch_shapes=[
            pltpu.VMEM((ATT_GROUPS, S, 128), jnp.bfloat16),
            pltpu.VMEM((S // KC, ATT_GROUPS * ATT_HEAD_DIM, KC), jnp.bfloat16),
            pltpu.VMEM((ATT_HEADS, TQ, 128), jnp.bfloat16),
            pltpu.VMEM((S // KC, KC, TQ), jnp.int32),
            pltpu.VMEM((S // KC, KC, TQ), jnp.int16),
            pltpu.VMEM((S // KC, KC, TQ), jnp.int16),
            pltpu.VMEM((1, TQ), jnp.int32),
            pltpu.VMEM((ATT_HEADS, 1, TQ), jnp.float32),
            pltpu.VMEM((ATT_HEADS, 1, TQ), jnp.float32),
            pltpu.VMEM((ATT_HEADS // 2, 2 * ATT_HEAD_DIM, TQ), jnp.float32),
            pltpu.VMEM((ATT_HEADS, DSA_KA, TQ), jnp.float32),
        ],
        compiler_params=pltpu.CompilerParams(
            dimension_semantics=("arbitrary", "arbitrary"),
            vmem_limit_bytes=_VMEM_LIMIT),
        name="dsa_attention",
    )(q, iq, iw, k, v, ik, pad_gain(qg), pad_gain(kg))


PEER_RT = 256
PEER_ET = 128
_ROW_CHUNKS = D_MODEL // 128
_TAB_ROWS = _ROW_CHUNKS // 2
_EDGES = PEER_HEADS * PEER_TOPK


def _extract_topk_rows(work, n_out, val_ref, pay_ref, order=None, payload=None):
    if order is None:
        order = lax.broadcasted_iota(jnp.int32, work.shape, 0)
    big = jnp.int32(2 ** 30)
    for it in range(n_out):
        m = jnp.max(work, axis=0, keepdims=True)
        am = jnp.min(jnp.where(work == m, order, big), axis=0, keepdims=True)
        hit = order == am
        val_ref[it:it + 1, :] = m
        if payload is None:
            pay_ref[it:it + 1, :] = am
        else:
            pay_ref[it:it + 1, :] = jnp.max(jnp.where(hit, payload, -1), axis=0, keepdims=True)
        work = jnp.where(hit, -jnp.inf, work)


def _peer_route_kernel(q_ref, sk_ref, e_ref, g_ref, s1_ref, i1_ref, s2_ref, i2_ref, ts_ref, te_ref, et_ref,
                       gt_ref):
    K = PEER_TOPK
    RT = q_ref.shape[0]
    b8 = lax.broadcasted_iota(jnp.int32, (8, RT), 0)
    b16 = lax.broadcasted_iota(jnp.int32, (K, RT), 0)
    order = jnp.concatenate([b16] + [a * K + b8 for a in range(1, 8)] + [(8 + b8) * K], axis=0)
    for h in range(PEER_HEADS):
        for p, (s_ref, i_ref) in enumerate(((s1_ref, i1_ref), (s2_ref, i2_ref))):
            col = (h * 2 + p) * 128
            qs = q_ref[:, col:col + 128].astype(jnp.bfloat16)
            sk = sk_ref[h, p].astype(jnp.bfloat16)
            sc = lax.dot_general(sk, qs, (((1,), (1,)), ((), ())),
                                 preferred_element_type=jnp.float32)
            _extract_topk_rows(sc, K, s_ref, i_ref)
        s2 = s2_ref[...]
        i2 = i2_ref[...]
        cand = [s1_ref[0:1, :] + s2]
        cidx = [i1_ref[0:1, :] * PEER_NKEYS + i2]
        for a in range(1, 8):
            cand.append(jnp.where(b8 < K // (a + 1), s1_ref[a:a + 1, :] + s2[0:8], -jnp.inf))
            cidx.append(i1_ref[a:a + 1, :] * PEER_NKEYS + i2[0:8])
        cand.append(s1_ref[8:K, :] + s2[0:1])
        cidx.append(i1_ref[8:K, :] * PEER_NKEYS + i2[0:1])
        _extract_topk_rows(jnp.concatenate(cand, axis=0), K, ts_ref, te_ref, order,
                           jnp.concatenate(cidx, axis=0) * _TAB_ROWS)
        ts = ts_ref[...]
        ex = jnp.exp(ts - jnp.max(ts, axis=0, keepdims=True))
        gate = ex / jnp.sum(ex, axis=0, keepdims=True)
        rep = jnp.concatenate([jnp.broadcast_to(gate[k:k + 1, :], (_ROW_CHUNKS, RT)) for k in range(K)], axis=0)
        gt_ref[:, h * 128:(h + 1) * 128] = rep.T
        et_ref[h * K:(h + 1) * K, :] = te_ref[...]
    e_ref[...] = et_ref[...].T
    g_ref[...] = gt_ref[...].reshape(RT, PEER_HEADS, 128)


def peer_route(qry, subkeys):
    T = qry.shape[0]
    RT = PEER_RT
    assert T % RT == 0
    K = PEER_TOPK
    return pl.pallas_call(
        _peer_route_kernel,
        grid=(T // RT,),
        in_specs=[
            pl.BlockSpec((RT, qry.shape[1]), lambda i: (i, 0)),
            pl.BlockSpec(subkeys.shape, lambda i: (0, 0, 0, 0)),
        ],
        out_specs=[pl.BlockSpec((RT, _EDGES), lambda i: (i, 0)),
                   pl.BlockSpec((RT, PEER_HEADS, 128), lambda i: (i, 0, 0))],
        out_shape=[jax.ShapeDtypeStruct((T, _EDGES), jnp.int32),
                   jax.ShapeDtypeStruct((T, PEER_HEADS, 128), jnp.float32)],
        scratch_shapes=[
            pltpu.VMEM((K, RT), jnp.float32), pltpu.VMEM((K, RT), jnp.int32),
            pltpu.VMEM((K, RT), jnp.float32), pltpu.VMEM((K, RT), jnp.int32),
            pltpu.VMEM((K, RT), jnp.float32), pltpu.VMEM((K, RT), jnp.int32),
            pltpu.VMEM((_EDGES, RT), jnp.int32),
            pltpu.VMEM((RT, PEER_HEADS * 128), jnp.float32),
        ],
        compiler_params=pltpu.CompilerParams(dimension_semantics=("arbitrary",)),
        name="peer_route",
    )(qry, subkeys)


def _pack_kernel(t_ref, o_ref):
    b = lax.bitcast_convert_type(t_ref[...], jnp.uint32)
    r = b + jnp.uint32(0x7FFF) + ((b >> 16) & jnp.uint32(1))
    words = [(r[:, (2 * s) * 128:(2 * s + 1) * 128] >> 16)
             | (r[:, (2 * s + 1) * 128:(2 * s + 2) * 128] & jnp.uint32(0xFFFF0000)) for s in range(_TAB_ROWS)]
    w = jnp.concatenate(words, axis=1)
    o_ref[...] = w.reshape(w.shape[0] * _TAB_ROWS, 128)


def pack_table(tab):
    n, d = tab.shape
    tr = 256
    assert n % tr == 0 and d == D_MODEL
    return pl.pallas_call(
        _pack_kernel,
        grid=(n // tr,),
        in_specs=[pl.BlockSpec((tr, d), lambda i: (i, 0))],
        out_specs=pl.BlockSpec((tr * _TAB_ROWS, 128), lambda i: (i, 0)),
        out_shape=jax.ShapeDtypeStruct((n * _TAB_ROWS, 128), jnp.uint32),
        compiler_params=pltpu.CompilerParams(dimension_semantics=("arbitrary",)),
        name="pack_table",
    )(tab)


def _gather_rows(idx_ref, tab_ref, stage_ref, t):
    R = _TAB_ROWS
    for i in range(_EDGES):
        row = pl.multiple_of(idx_ref[t, i], R)
        stage_ref[R * i:R * i + R, :] = tab_ref[pl.ds(row, R), :]


def _split_bf16(x):
    hi = x.astype(jnp.bfloat16)
    lo = (x - hi.astype(jnp.float32)).astype(jnp.bfloat16)
    return jnp.concatenate([hi, lo], axis=0)


_SLOTS = 16
_AHEAD = 2


def _pipelined_tokens(n_tok, gather, compute):
    for t in range(_AHEAD):
        gather(t, t)

    def body(tt, carry):
        t0 = _SLOTS * tt
        for u in range(_SLOTS):
            gather(jnp.minimum(t0 + u + _AHEAD, n_tok - 1), (u + _AHEAD) % _SLOTS)
            compute(t0 + u, u)
        return carry
    lax.fori_loop(0, n_tok // _SLOTS, body, 0)


def _peer_dot_kernel(idx_ref, tab_ref, x_ref, a_ref, *stages):
    n = _EDGES * _ROW_CHUNKS
    sub = lax.broadcasted_iota(jnp.int32, (2 * _ROW_CHUNKS, n), 0) % _ROW_CHUNKS
    lane = lax.broadcasted_iota(jnp.int32, (2 * _ROW_CHUNKS, n), 1)
    diag = sub == lane % _ROW_CHUNKS

    def gather(t, slot):
        _gather_rows(idx_ref, tab_ref, stages[slot], t)

    def compute(t, slot):
        m = pltpu.bitcast(stages[slot][...], jnp.bfloat16)
        x16 = _split_bf16(x_ref[t])
        r = lax.dot_general(x16, m, (((1,), (1,)), ((), ())),
                            preferred_element_type=jnp.float32)
        tot = jnp.sum(jnp.where(diag, r, 0.0), axis=0, keepdims=True)
        for g in range(n // 128):
            a_ref[t, g:g + 1, :] = tot[:, g * 128:(g + 1) * 128]

    _pipelined_tokens(a_ref.shape[0], gather, compute)

    r_i = lax.broadcasted_iota(jnp.int32, (128, 128), 0) // _ROW_CHUNKS
    c_i = lax.broadcasted_iota(jnp.int32, (128, 128), 1) // _ROW_CHUNKS
    ones_bd = jnp.where(r_i == c_i, 1.0, 0.0).astype(jnp.bfloat16)
    a = a_ref[...].reshape(a_ref.shape[0] * _ROW_CHUNKS, 128)
    a = sum(jnp.dot(piece, ones_bd, preferred_element_type=jnp.float32) for piece in _split3_bf16(a))
    a_ref[...] = a.reshape(a_ref.shape)


def _peer_mix_kernel(idx_ref, tab_ref, a_ref, g_ref, x1_ref, g2_ref, y_ref, yc_ref, *stages):
    sub = lax.broadcasted_iota(jnp.int32, (_ROW_CHUNKS, 128), 0)
    lane = lax.broadcasted_iota(jnp.int32, (_ROW_CHUNKS, 128), 1)
    diag = sub == lane % _ROW_CHUNKS
    ngroups = _EDGES * _ROW_CHUNKS // 128

    def gather(t, slot):
        _gather_rows(idx_ref, tab_ref, stages[slot], t)

    def compute(t, slot):
        m = pltpu.bitcast(stages[slot][...], jnp.bfloat16)
        a = a_ref[t]
        c = g_ref[t] * (0.5 * a * (1.0 + lax.erf(a * (2.0 ** -0.5))))
        lhs = jnp.concatenate(
            [jnp.where(diag, jnp.broadcast_to(c[g:g + 1, :], (_ROW_CHUNKS, 128)), 0.0) for g in range(ngroups)],
            axis=1)
        r = jnp.dot(_split_bf16(lhs), m, preferred_element_type=jnp.float32)
        yc_ref[t] = x1_ref[t] + g2_ref[0] * (r[0:_ROW_CHUNKS] + r[_ROW_CHUNKS:])

    _pipelined_tokens(y_ref.shape[0], gather, compute)
    y_ref[...] = yc_ref[...].reshape(y_ref.shape)


def _expert_call(kern, name, idx, tab, ins, extra=(), extra_specs=(), rows_out=False):
    T = idx.shape[0]
    ET = PEER_ET
    assert T % ET == 0
    blk = pl.BlockSpec((ET, _ROW_CHUNKS, 128), lambda i: (i, 0, 0))
    stage = [pltpu.VMEM((_EDGES * _TAB_ROWS, 128), jnp.uint32)] * _SLOTS
    if rows_out:
        out_spec = pl.BlockSpec((ET, D_MODEL), lambda i: (i, 0))
        out_shape = jax.ShapeDtypeStruct((T, D_MODEL), jnp.float32)
        scratch = [pltpu.VMEM((ET, _ROW_CHUNKS, 128), jnp.float32)] + stage
    else:
        out_spec, out_shape, scratch = blk, jax.ShapeDtypeStruct((T, _ROW_CHUNKS, 128), jnp.float32), stage
    return pl.pallas_call(
        kern,
        grid=(T // ET,),
        in_specs=[
            pl.BlockSpec((ET, _EDGES), lambda i: (i, 0), memory_space=pltpu.SMEM),
            pl.BlockSpec(tab.shape, lambda i: (0, 0), pipeline_mode=pl.Buffered(1)),
        ] + [blk] * len(ins) + list(extra_specs),
        out_specs=out_spec,
        out_shape=out_shape,
        scratch_shapes=scratch,
        compiler_params=pltpu.CompilerParams(
            dimension_semantics=("arbitrary",),
            vmem_limit_bytes=_VMEM_LIMIT),
        name=name,
    )(idx, tab, *ins, *extra)


def peer_experts(h2c, idx, gexp, u, vtab, x1c, g2, seq):
    T = h2c.shape[0]
    ET = PEER_ET
    assert seq % ET == 0
    a = _expert_call(_peer_dot_kernel, "peer_dot", idx, pack_table(u), [h2c])
    g2_spec = pl.BlockSpec((1, _ROW_CHUNKS, 128), lambda i: (i * ET // seq, 0, 0))
    return _expert_call(_peer_mix_kernel, "peer_mix", idx, pack_table(vtab), [a, gexp, x1c],
                        extra=[g2.reshape(g2.shape[0], _ROW_CHUNKS, 128)], extra_specs=[g2_spec], rows_out=True)


def kernel(x, c, w_ada, b_ada, norm1_g, norm2_g, w_in, hg_lb, hg_onorm_g, q_norm_g, k_norm_g,
           w_a, w_b, w_out, peer_wq, peer_subkeys, peer_u, peer_v):
    B, S, D = x.shape
    T = B * S
    for l in range(w_ada.shape[0]):
        mod = ada_modulation(c, w_ada[l], b_ada[l])
        sh1, sc1, g1, sh2, sc2, g2 = jnp.split(mod, 6, axis=-1)
        hg, q, k, iw, iq, ik, v, gate = in_projection(x, norm1_g[l], sc1, sh1, pack_w_in(w_in[l]))
        ya = hgrn2(hg, hg_lb, hg_onorm_g[l], l)
        yb = dsa_attention(q, iq, iw, k, v, ik, q_norm_g[l], k_norm_g[l])
        x1, h2, qry = out_projection(ya, yb, gate, x, g1, norm2_g[l], sc2, sh2,
                                     w_a[l], w_b[l], w_out[l], peer_wq[l])
        idx, gexp = peer_route(qry.reshape(T, -1), peer_subkeys[l])
        x = peer_experts(h2.reshape(T, D // 128, 128), idx, gexp, peer_u[l], peer_v[l],
                         x1.reshape(T, D // 128, 128), g2, S).reshape(B, S, D)
    return x
```

```python
import functools

import jax
import jax.numpy as jnp
import numpy as np
from jax import lax
from jax.experimental import pallas as pl
from jax.experimental.pallas import tpu as pltpu

D_MODEL = 1024
HG_HEADS = 4
HG_DK = 128
HG_WIDTH = HG_HEADS * HG_DK
HG_CHUNK = 64
ATT_HEADS = 8
ATT_GROUPS = 2
ATT_HEAD_DIM = 64
ATT_REP = ATT_HEADS // ATT_GROUPS
IDX_HEADS = 8
IDX_DIM = 64
IDX_TOPK_MAX = 256
PEER_HEADS = 8
PEER_NKEYS = 128
PEER_TOPK = 16
RMS_EPS = 1e-6
IN_SPLIT = (HG_WIDTH, HG_WIDTH, HG_WIDTH, HG_WIDTH,
            ATT_HEADS * ATT_HEAD_DIM, ATT_GROUPS * ATT_HEAD_DIM, ATT_GROUPS * ATT_HEAD_DIM,
            IDX_HEADS * IDX_DIM, IDX_DIM, IDX_HEADS,
            D_MODEL, D_MODEL)

_VMEM_LIMIT = 48 * 1024 * 1024


def _ada_kernel(c_ref, w_ref, b_ref, o_ref):
    c = c_ref[...]
    cs = c * (1.0 / (1.0 + jnp.exp(-c)))
    o_ref[...] = jnp.dot(cs, w_ref[...], preferred_element_type=jnp.float32,
                         precision=lax.Precision.HIGHEST) + b_ref[...]


def ada_modulation(c, w, b):
    B, D = c.shape
    N = w.shape[1]
    tn = 1536
    assert N % tn == 0
    return pl.pallas_call(
        _ada_kernel,
        grid=(N // tn,),
        in_specs=[pl.BlockSpec((B, D), lambda j: (0, 0)),
                  pl.BlockSpec((D, tn), lambda j: (0, j)),
                  pl.BlockSpec((1, tn), lambda j: (0, j))],
        out_specs=pl.BlockSpec((B, tn), lambda j: (0, j)),
        out_shape=jax.ShapeDtypeStruct((B, N), jnp.float32),
        compiler_params=pltpu.CompilerParams(dimension_semantics=("arbitrary",),
                                             vmem_limit_bytes=_VMEM_LIMIT),
        name="ada_modulation",
    )(c, w, b.reshape(1, N))


_PROJ_SEGS = (("hg", 2048, jnp.float32), ("q", 1024, jnp.float32), ("k", 256, jnp.float32),
              ("iw", 128, jnp.float32), ("iq", 1024, jnp.bfloat16), ("ik", 128, jnp.bfloat16),
              ("v", 128, jnp.bfloat16), ("gate", 2048, jnp.float32))
PROJ_TM = 256


def pack_w_in(w_in):
    D = w_in.shape[0]
    offs = np.cumsum((0,) + IN_SPLIT)
    hq, hf, hi, hg, aq, ak, av, iq, ik, iw, ga, gb = [w_in[:, offs[i]:offs[i + 1]] for i in range(12)]

    def pad_heads(w, nh):
        w = w.reshape(D, nh, 64)
        return jnp.concatenate([w, jnp.zeros_like(w)], axis=-1).reshape(D, nh * 128)

    iwp = jnp.concatenate([iw, jnp.zeros((D, 128 - IDX_HEADS), w_in.dtype)], axis=1)
    cols = [hq, hf, hi, hg, pad_heads(aq, ATT_HEADS), pad_heads(ak, ATT_GROUPS), iwp,
            pad_heads(iq, IDX_HEADS), pad_heads(ik, 1),
            av, ga, gb]
    return jnp.concatenate(cols, axis=1).astype(jnp.bfloat16)


def _proj_kernel(x_ref, g_ref, sc_ref, sh_ref, w_ref, *o_refs):
    x = x_ref[0]
    ms = jnp.mean(x * x, axis=-1, keepdims=True)
    h = x * lax.rsqrt(ms + RMS_EPS) * g_ref[...] * (1.0 + sc_ref[0]) + sh_ref[0]
    hb = h.astype(jnp.bfloat16)
    c0 = 0
    for (_, width, _), o_ref in zip(_PROJ_SEGS, o_refs):
        for s in range(0, width, 512):
            e = min(s + 512, width)
            o_ref[0, :, s:e] = jnp.dot(hb, w_ref[:, c0 + s:c0 + e],
                                       preferred_element_type=jnp.float32).astype(o_ref.dtype)
        c0 += width


def in_projection(x, norm_g, sc, sh, w_packed):
    B, S, D = x.shape
    tm = PROJ_TM
    assert S % tm == 0
    ntot = sum(w for _, w, _ in _PROJ_SEGS)
    assert w_packed.shape == (D, ntot)
    return pl.pallas_call(
        _proj_kernel,
        grid=(B, S // tm),
        in_specs=[
            pl.BlockSpec((1, tm, D), lambda b, i: (b, i, 0)),
            pl.BlockSpec((1, D), lambda b, i: (0, 0)),
            pl.BlockSpec((1, 1, D), lambda b, i: (b, 0, 0)),
            pl.BlockSpec((1, 1, D), lambda b, i: (b, 0, 0)),
            pl.BlockSpec((D, ntot), lambda b, i: (0, 0), pipeline_mode=pl.Buffered(1)),
        ],
        out_specs=[pl.BlockSpec((1, tm, w), lambda b, i: (b, i, 0)) for _, w, _ in _PROJ_SEGS],
        out_shape=[jax.ShapeDtypeStruct((B, S, w), dt) for _, w, dt in _PROJ_SEGS],
        compiler_params=pltpu.CompilerParams(dimension_semantics=("arbitrary", "arbitrary"),
                                             vmem_limit_bytes=_VMEM_LIMIT),
        name="in_projection",
    )(x, norm_g.reshape(1, D), sc.reshape(B, 1, D), sh.reshape(B, 1, D), w_packed)


HG_LT = 512


def _split3_bf16(x):
    a = x.astype(jnp.bfloat16)
    r = x - a.astype(jnp.float32)
    b = r.astype(jnp.bfloat16)
    c = (r - b.astype(jnp.float32)).astype(jnp.bfloat16)
    return a, b, c


def _hgrn_kernel(q_ref, f_ref, i_ref, g_ref, lb_ref, og_ref, o_ref, st_ref, *, layer):
    C = HG_CHUNK
    bf = jnp.bfloat16

    @pl.when(pl.program_id(1) == 0)
    def _():
        st_ref[...] = jnp.zeros(st_ref.shape, jnp.float32)

    lbr = lb_ref[...]
    e = jnp.exp(lbr - jnp.max(lbr, axis=0, keepdims=True))
    sm = e / jnp.sum(e, axis=0, keepdims=True)
    lb_all = jnp.sum(sm[0:layer + 1], axis=0, keepdims=True)

    r_i = lax.broadcasted_iota(jnp.int32, (C, C), 0)
    c_i = lax.broadcasted_iota(jnp.int32, (C, C), 1)
    causal = c_i <= r_i
    tri = jnp.where(causal, 1.0, 0.0).astype(bf)
    og = og_ref[...]

    def chunk(c, carry):
        r0 = pl.multiple_of(c * C, C)
        H = range(HG_HEADS)
        cols = [slice(h * HG_DK, (h + 1) * HG_DK) for h in H]
        dot = functools.partial(jnp.dot, preferred_element_type=jnp.float32)
        nt = (((1,), (1,)), ((), ()))
        tn = (((0,), (0,)), ((), ()))
        qr = [q_ref[0, pl.ds(r0, C), cs] for cs in cols]
        q = [x * (1.0 / (1.0 + jnp.exp(-x))) for x in qr]
        f = [lb_all[:, cs] + (1.0 - lb_all[:, cs]) * (1.0 / (1.0 + jnp.exp(-f_ref[0, pl.ds(r0, C), cs])))
             for cs in cols]
        k = [1.0 - x for x in f]
        vb = [i_ref[0, pl.ds(r0, C), cs].astype(bf) for cs in cols]
        lg = [_split3_bf16(jnp.log(x)) for x in f]
        b = [dot(tri, l1) + dot(tri, l2) + dot(tri, l3) for l1, l2, l3 in lg]
        bm = [x[C // 2 - 1:C // 2, :] for x in b]
        bl = [x[C - 1:C, :] for x in b]
        a = [lax.dot_general((q[h] * jnp.exp(b[h] - bm[h])).astype(bf), (k[h] * jnp.exp(bm[h] - b[h])).astype(bf),
                             nt, preferred_element_type=jnp.float32) for h in H]
        st = [st_ref[h] for h in H]
        o_inter = [lax.dot_general((q[h] * jnp.exp(b[h])).astype(bf), st[h].astype(bf), nt,
                                   preferred_element_type=jnp.float32) for h in H]
        ds = [lax.dot_general(vb[h], (k[h] * jnp.exp(bl[h] - b[h])).astype(bf), tn,
                              preferred_element_type=jnp.float32) for h in H]
        for h in H:
            st_ref[h] = st[h] * jnp.exp(bl[h]) + ds[h]
        o = [dot(jnp.where(causal, a[h], 0.0).astype(bf), vb[h]) + o_inter[h] for h in H]
        for h in H:
            on = o[h] * lax.rsqrt(jnp.mean(o[h] * o[h], axis=-1, keepdims=True) + RMS_EPS) * og
            gr = g_ref[0, pl.ds(r0, C), cols[h]]
            o_ref[0, pl.ds(r0, C), cols[h]] = (on * (gr * (1.0 / (1.0 + jnp.exp(-gr))))).astype(o_ref.dtype)
        return carry
    lax.fori_loop(0, q_ref.shape[1] // C, chunk, 0)


def hgrn2(hg, hg_lb, onorm_g, layer):
    B, S, _ = hg.shape
    LT = HG_LT
    assert S % LT == 0

    def piece(p):
        return pl.BlockSpec((1, LT, HG_WIDTH), lambda b, i, p=p: (b, i, p))
    return pl.pallas_call(
        functools.partial(_hgrn_kernel, layer=layer),
        grid=(B, S // LT),
        in_specs=[piece(0), piece(1), piece(2), piece(3),
                  pl.BlockSpec(hg_lb.shape, lambda b, i: (0, 0)),
                  pl.BlockSpec((1, HG_DK), lambda b, i: (0, 0))],
        out_specs=pl.BlockSpec((1, LT, HG_WIDTH), lambda b, i: (b, i, 0)),
        out_shape=jax.ShapeDtypeStruct((B, S, HG_WIDTH), jnp.bfloat16),
        scratch_shapes=[pltpu.VMEM((HG_HEADS, HG_DK, HG_DK), jnp.float32)],
        compiler_params=pltpu.CompilerParams(dimension_semantics=("arbitrary", "arbitrary"),
                                             vmem_limit_bytes=_VMEM_LIMIT),
        name="hgrn2",
    )(hg, hg, hg, hg, hg_lb, onorm_g.reshape(1, HG_DK))


OUT_TM = 256


def _out_kernel(ya_ref, yb_ref, ga_ref, gb_ref, x_ref, g1_ref, n2_ref, sc_ref, sh_ref,
                wa_ref, wb_ref, wo_ref, wq_ref, x1_ref, h2_ref, qry_ref):
    bf = jnp.bfloat16

    def sig(z):
        return 1.0 / (1.0 + jnp.exp(-z))
    ma = jnp.dot(ya_ref[0], wa_ref[...], preferred_element_type=jnp.float32)
    mb = jnp.dot(yb_ref[0], wb_ref[...], preferred_element_type=jnp.float32)
    merged = sig(ga_ref[0]) * ma + sig(gb_ref[0]) * mb
    x1 = x_ref[0] + g1_ref[0] * jnp.dot(merged.astype(bf), wo_ref[...], preferred_element_type=jnp.float32)
    tm, d = x1.shape
    x1_ref[0] = x1.reshape(tm, d // 128, 128)
    ms = jnp.mean(x1 * x1, axis=-1, keepdims=True)
    h2 = x1 * lax.rsqrt(ms + RMS_EPS) * n2_ref[...] * (1.0 + sc_ref[0]) + sh_ref[0]
    h2_ref[0] = h2.reshape(tm, d // 128, 128)
    qry_ref[0] = jnp.dot(h2.astype(bf), wq_ref[...], preferred_element_type=jnp.float32)


def out_projection(ya, yb, gate, x, g1, norm2_g, sc2, sh2, w_a, w_b, w_out, wq):
    B, S, D = x.shape
    tm = OUT_TM
    bf = jnp.bfloat16
    NQ = wq.shape[1]

    def tok(width, col=0):
        return pl.BlockSpec((1, tm, width), lambda b, i, col=col: (b, i, col))

    def chunked_tok():
        return pl.BlockSpec((1, tm, D // 128, 128), lambda b, i: (b, i, 0, 0))

    def per_batch():
        return pl.BlockSpec((1, 1, D), lambda b, i: (b, 0, 0))

    def const(shape):
        return pl.BlockSpec(shape, lambda b, i: (0,) * len(shape), pipeline_mode=pl.Buffered(1))
    return pl.pallas_call(
        _out_kernel,
        grid=(B, S // tm),
        in_specs=[tok(HG_WIDTH), tok(512), tok(D, 0), tok(D, 1), tok(D), per_batch(),
                  const((1, D)), per_batch(), per_batch(),
                  const(w_a.shape), const(w_b.shape), const(w_out.shape), const(wq.shape)],
        out_specs=[chunked_tok(), chunked_tok(), tok(NQ)],
        out_shape=[jax.ShapeDtypeStruct((B, S, D // 128, 128), jnp.float32),
                   jax.ShapeDtypeStruct((B, S, D // 128, 128), jnp.float32),
                   jax.ShapeDtypeStruct((B, S, NQ), jnp.float32)],
        compiler_params=pltpu.CompilerParams(dimension_semantics=("arbitrary", "arbitrary"),
                                             vmem_limit_bytes=_VMEM_LIMIT),
        name="out_projection",
    )(ya, yb, gate, gate, x, g1.reshape(B, 1, D), norm2_g.reshape(1, D), sc2.reshape(B, 1, D),
      sh2.reshape(B, 1, D), w_a.astype(bf), w_b.astype(bf), w_out.astype(bf), wq.astype(bf))


DSA_TQ = 128
DSA_KC = 512
DSA_KA = 256
_FILL_KEY = -2139095041
_NEG_BIG = -1e30


def _sortable_key(x):
    b = lax.bitcast_convert_type(x, jnp.int32)
    return b ^ ((b >> 31) & jnp.int32(0x7FFFFFFF))


def _dsa_kernel(q_ref, iq_ref, iw_ref, k_ref, v_ref, ik_ref, qg_ref, kg_ref, o_ref,
                kn_ref, vt_ref, qs_ref, keys_ref, khi_ref, klo_ref, lim_ref, m_ref, l_ref, acc_ref, sc_ref,
                *, seq, topk):
    TQ, KC, KA = DSA_TQ, DSA_KC, DSA_KA
    bf = jnp.bfloat16
    j = pl.program_id(1)
    n_kc = (j * TQ + TQ + KC - 1) // KC
    idx_scale = (IDX_DIM ** -0.5) * (IDX_HEADS ** -0.5)
    att_scale = ATT_HEAD_DIM ** -0.5
    nt = (((1,), (1,)), ((), ()))

    lane = lax.broadcasted_iota(jnp.int32, (1, 128), 1)
    krow = lax.broadcasted_iota(jnp.int32, (KC, 1), 0)

    def kfeat(kpos):
        a = (kpos // 64).astype(jnp.float32)
        b = (kpos % 64).astype(jnp.float32)
        return jnp.where(lane == ATT_HEAD_DIM, a, jnp.where(lane == ATT_HEAD_DIM + 1, b, 0.0))

    def qfeat(slope):
        return jnp.where(lane == ATT_HEAD_DIM, 64.0 * slope, jnp.where(lane == ATT_HEAD_DIM + 1, slope, 0.0))

    @pl.when(j == 0)
    def _():
        def body(c, carry):
            r0 = pl.multiple_of(c * KC, KC)
            for g in range(ATT_GROUPS):
                kk = k_ref[0, pl.ds(r0, KC), g * 128:(g + 1) * 128]
                ms = jnp.sum(kk * kk, axis=-1, keepdims=True) * (1.0 / ATT_HEAD_DIM)
                kn = kk * lax.rsqrt(ms + RMS_EPS) * kg_ref[...]
                kn_ref[g, pl.ds(r0, KC), :] = (kn + kfeat(r0 + krow)).astype(bf)
            vt_ref[c] = v_ref[0, pl.ds(r0, KC), :].astype(jnp.float32).T.astype(bf)
            return carry
        lax.fori_loop(0, seq // KC, body, 0)

    for h in range(ATT_HEADS):
        qq = q_ref[0, :, h * 128:(h + 1) * 128]
        ms = jnp.sum(qq * qq, axis=-1, keepdims=True) * (1.0 / ATT_HEAD_DIM)
        slope = float(2.0 ** (-8.0 * (h + 1) / ATT_HEADS))
        qs_ref[h] = (qq * lax.rsqrt(ms + RMS_EPS) * qg_ref[...] * att_scale + qfeat(slope)).astype(bf)
    iwt = iw_ref[0].T

    qpos = j * TQ + lax.broadcasted_iota(jnp.int32, (1, TQ), 1)

    def idx_body(c, carry):
        r0 = pl.multiple_of(c * KC, KC)
        ikc = ik_ref[0, pl.ds(r0, KC), :]
        acc = jnp.zeros((KC, TQ), jnp.float32)
        for h in range(IDX_HEADS):
            s = lax.dot_general(ikc, iq_ref[0, :, h * 128:(h + 1) * 128], nt,
                                preferred_element_type=jnp.float32)
            acc = acc + iwt[h:h + 1, :] * jnp.maximum(s, 0.0)
        score = acc * idx_scale
        score = jnp.where(score == 0.0, 0.0, score)
        score = jnp.where(c * KC + krow <= qpos, score, -jnp.inf)
        key = _sortable_key(score)
        keys_ref[c] = key
        khi_ref[c] = (key >> 16).astype(jnp.int16)
        klo_ref[c] = ((key & 0xFFFF) - 2 ** 15).astype(jnp.int16)
        return carry
    lax.fori_loop(0, n_kc, idx_body, 0)

    def fold(m, rows):
        parts = [m[rows * i:rows * (i + 1)] for i in range(m.shape[0] // rows)]
        while len(parts) > 1:
            parts = [parts[i] + parts[i + 1] for i in range(0, len(parts), 2)]
        return parts[0]

    def count(pred):
        def body(c, acc):
            return acc + fold(jnp.where(pred(c, keys_ref[c]), 1.0, 0.0), 8)
        acc = lax.fori_loop(0, n_kc, body, jnp.zeros((8, TQ), jnp.float32))
        return jnp.sum(acc, axis=0, keepdims=True)

    def count16(ref, cand):
        c16 = cand.astype(jnp.int16)

        def body(c, acc):
            hit = jnp.where(ref[c] >= c16, jnp.int16(1), jnp.int16(0))
            return acc + fold(hit, 16).astype(jnp.float32)
        acc = lax.fori_loop(0, n_kc, body, jnp.zeros((16, TQ), jnp.float32))
        return jnp.sum(acc, axis=0, keepdims=True)

    def search16(ref, base):
        ans0 = jnp.where(base + count16(ref, jnp.zeros((1, TQ), jnp.int32)) >= kf, jnp.int32(0), jnp.int32(-2 ** 15))

        def bit_body(i, ans):
            cand = ans | jnp.left_shift(jnp.int32(1), 14 - i)
            return jnp.where(base + count16(ref, cand) >= kf, cand, ans)
        return lax.fori_loop(0, 15, bit_body, ans0)

    def count_ge(cand):
        return count(lambda c, key: key >= cand)

    kf = float(topk)
    thr_hi = search16(khi_ref, jnp.zeros((1, TQ), jnp.float32))
    n_hi_gt = count16(khi_ref, thr_hi + 1)
    hi16 = thr_hi.astype(jnp.int16)

    def mask_low(c, carry):
        klo_ref[c] = jnp.where(khi_ref[c] == hi16, klo_ref[c], jnp.int16(-2 ** 15))
        return carry
    lax.fori_loop(0, n_kc, mask_low, 0)
    thr_lo = search16(klo_ref, n_hi_gt)
    thr = jnp.left_shift(thr_hi, 16) | (thr_lo + 2 ** 15)

    n_gt = count_ge(thr + 1)
    n_ge = count_ge(thr)
    need = kf - n_gt
    tie = jnp.logical_and(n_ge - n_gt > need, thr != _FILL_KEY)
    lim_ref[...] = jnp.full((1, TQ), seq, jnp.int32)

    @pl.when(jnp.max(jnp.where(tie, 1.0, 0.0)) > 0.0)
    def _():
        nbits = max(1, int(np.ceil(np.log2(seq))))

        def bit2(i, lo):
            cand = lo | jnp.left_shift(jnp.int32(1), nbits - 1 - i)
            n_before = count(lambda c, key: jnp.logical_and(key == thr, c * KC + krow < cand))
            return jnp.where(n_before < need, cand, lo)
        lo = lax.fori_loop(0, nbits, bit2, jnp.zeros((1, TQ), jnp.int32))
        lim_ref[...] = jnp.where(tie, lo, jnp.int32(seq))

    lim = lim_ref[...]

    m_ref[...] = jnp.full(m_ref.shape, _NEG_BIG, jnp.float32)
    l_ref[...] = jnp.zeros(l_ref.shape, jnp.float32)
    acc_ref[...] = jnp.zeros(acc_ref.shape, jnp.float32)
    hd = ATT_HEAD_DIM

    def att_body(c, carry):
        vtc = vt_ref[c]
        for half in range(KC // KA):
            h0 = half * KA
            r0 = pl.multiple_of(c * KC, KC) + h0
            kpos = c * KC + h0 + krow[0:KA]
            key = keys_ref[c, h0:h0 + KA, :]
            sel = jnp.logical_or(key > thr, jnp.logical_and(key == thr, kpos <= lim))
            sel = jnp.logical_and(sel, kpos <= qpos)
            madd = jnp.where(sel, 0.0, _NEG_BIG)
            mx = []
            for h in range(ATT_HEADS):
                kc = kn_ref[h // ATT_REP, pl.ds(r0, KA), :]
                sc = lax.dot_general(kc, qs_ref[h], nt, preferred_element_type=jnp.float32) + madd
                sc_ref[h] = sc
                mx.append(jnp.max(sc, axis=0, keepdims=True))
            for h in range(ATT_HEADS):
                g = h // ATT_REP
                m_old = m_ref[h]
                m_new = jnp.maximum(m_old, mx[h])
                p = jnp.exp(sc_ref[h] - m_new)
                alpha = jnp.exp(m_old - m_new)
                l_ref[h] = alpha * l_ref[h] + jnp.sum(p, axis=0, keepdims=True)
                m_ref[h] = m_new
                pv = jnp.dot(vtc[g * hd:(g + 1) * hd, h0:h0 + KA], p.astype(bf),
                             preferred_element_type=jnp.float32)
                rs = slice((h % 2) * hd, (h % 2 + 1) * hd)
                acc_ref[h // 2, rs, :] = acc_ref[h // 2, rs, :] * alpha + pv
        return carry
    lax.fori_loop(0, n_kc, att_body, 0)

    for pair in range(ATT_HEADS // 2):
        den = jnp.concatenate([jnp.broadcast_to(l_ref[2 * pair], (hd, TQ)),
                               jnp.broadcast_to(l_ref[2 * pair + 1], (hd, TQ))], axis=0)
        o_ref[0, :, pair * 128:(pair + 1) * 128] = (acc_ref[pair] / den).T.astype(o_ref.dtype)


def dsa_attention(q, iq, iw, k, v, ik, qg, kg):
    B, S, _ = q.shape
    TQ, KC = DSA_TQ, DSA_KC
    assert S % KC == 0 and S % TQ == 0
    topk = min(IDX_TOPK_MAX, S // 4)
    kern = functools.partial(_dsa_kernel, seq=S, topk=topk)

    def pad_gain(g):
        return jnp.concatenate([g, jnp.zeros_like(g)]).reshape(1, 128)
    return pl.pallas_call(
        kern,
        grid=(B, S // TQ),
        in_specs=[
            pl.BlockSpec((1, TQ, ATT_HEADS * 128), lambda b, j: (b, j, 0)),
            pl.BlockSpec((1, TQ, IDX_HEADS * 128), lambda b, j: (b, j, 0)),
            pl.BlockSpec((1, TQ, 128), lambda b, j: (b, j, 0)),
            pl.BlockSpec((1, S, ATT_GROUPS * 128), lambda b, j: (b, 0, 0)),
            pl.BlockSpec((1, S, ATT_GROUPS * ATT_HEAD_DIM), lambda b, j: (b, 0, 0)),
            pl.BlockSpec((1, S, 128), lambda b, j: (b, 0, 0)),
            pl.BlockSpec((1, 128), lambda b, j: (0, 0)),
            pl.BlockSpec((1, 128), lambda b, j: (0, 0)),
        ],
        out_specs=pl.BlockSpec((1, TQ, ATT_HEADS * ATT_HEAD_DIM), lambda b, j: (b, j, 0)),
        out_shape=jax.ShapeDtypeStruct((B, S, ATT_HEADS * ATT_HEAD_DIM), jnp.bfloat16),
        scratch_shapes=[
            pltpu.VMEM((ATT_GROUPS, S, 128), jnp.bfloat16),
            pltpu.VMEM((S // KC, ATT_GROUPS * ATT_HEAD_DIM, KC), jnp.bfloat16),
            pltpu.VMEM((ATT_HEADS, TQ, 128), jnp.bfloat16),
            pltpu.VMEM((S // KC, KC, TQ), jnp.int32),
            pltpu.VMEM((S // KC, KC, TQ), jnp.int16),
            pltpu.VMEM((S // KC, KC, TQ), jnp.int16),
            pltpu.VMEM((1, TQ), jnp.int32),
            pltpu.VMEM((ATT_HEADS, 1, TQ), jnp.float32),
            pltpu.VMEM((ATT_HEADS, 1, TQ), jnp.float32),
            pltpu.VMEM((ATT_HEADS // 2, 2 * ATT_HEAD_DIM, TQ), jnp.float32),
            pltpu.VMEM((ATT_HEADS, DSA_KA, TQ), jnp.float32),
        ],
        compiler_params=pltpu.CompilerParams(
            dimension_semantics=("arbitrary", "arbitrary"),
            vmem_limit_bytes=_VMEM_LIMIT),
        name="dsa_attention",
    )(q, iq, iw, k, v, ik, pad_gain(qg), pad_gain(kg))


PEER_RT = 256
PEER_ET = 128
_ROW_CHUNKS = D_MODEL // 128
_TAB_ROWS = _ROW_CHUNKS // 2
_EDGES = PEER_HEADS * PEER_TOPK


def _extract_topk_rows(work, n_out, val_ref, pay_ref, order=None, payload=None):
    if order is None:
        order = lax.broadcasted_iota(jnp.int32, work.shape, 0)
    big = jnp.int32(2 ** 30)
    for it in range(n_out):
        m = jnp.max(work, axis=0, keepdims=True)
        am = jnp.min(jnp.where(work == m, order, big), axis=0, keepdims=True)
        hit = order == am
        val_ref[it:it + 1, :] = m
        if payload is None:
            pay_ref[it:it + 1, :] = am
        else:
            pay_ref[it:it + 1, :] = jnp.max(jnp.where(hit, payload, -1), axis=0, keepdims=True)
        work = jnp.where(hit, -jnp.inf, work)


def _peer_route_kernel(q_ref, sk_ref, e_ref, g_ref, s1_ref, i1_ref, s2_ref, i2_ref, ts_ref, te_ref, et_ref,
                       gt_ref):
    K = PEER_TOPK
    RT = q_ref.shape[0]
    b8 = lax.broadcasted_iota(jnp.int32, (8, RT), 0)
    b16 = lax.broadcasted_iota(jnp.int32, (K, RT), 0)
    order = jnp.concatenate([b16] + [a * K + b8 for a in range(1, 8)] + [(8 + b8) * K], axis=0)
    for h in range(PEER_HEADS):
        for p, (s_ref, i_ref) in enumerate(((s1_ref, i1_ref), (s2_ref, i2_ref))):
            col = (h * 2 + p) * 128
            qs = q_ref[:, col:col + 128].astype(jnp.bfloat16)
            sk = sk_ref[h, p].astype(jnp.bfloat16)
            sc = lax.dot_general(sk, qs, (((1,), (1,)), ((), ())),
                                 preferred_element_type=jnp.float32)
            _extract_topk_rows(sc, K, s_ref, i_ref)
        s2 = s2_ref[...]
        i2 = i2_ref[...]
        cand = [s1_ref[0:1, :] + s2]
        cidx = [i1_ref[0:1, :] * PEER_NKEYS + i2]
        for a in range(1, 8):
            cand.append(jnp.where(b8 < K // (a + 1), s1_ref[a:a + 1, :] + s2[0:8], -jnp.inf))
            cidx.append(i1_ref[a:a + 1, :] * PEER_NKEYS + i2[0:8])
        cand.append(s1_ref[8:K, :] + s2[0:1])
        cidx.append(i1_ref[8:K, :] * PEER_NKEYS + i2[0:1])
        _extract_topk_rows(jnp.concatenate(cand, axis=0), K, ts_ref, te_ref, order,
                           jnp.concatenate(cidx, axis=0) * _TAB_ROWS)
        ts = ts_ref[...]
        ex = jnp.exp(ts - jnp.max(ts, axis=0, keepdims=True))
        gate = ex / jnp.sum(ex, axis=0, keepdims=True)
        rep = jnp.concatenate([jnp.broadcast_to(gate[k:k + 1, :], (_ROW_CHUNKS, RT)) for k in range(K)], axis=0)
        gt_ref[:, h * 128:(h + 1) * 128] = rep.T
        et_ref[h * K:(h + 1) * K, :] = te_ref[...]
    e_ref[...] = et_ref[...].T
    g_ref[...] = gt_ref[...].reshape(RT, PEER_HEADS, 128)


def peer_route(qry, subkeys):
    T = qry.shape[0]
    RT = PEER_RT
    assert T % RT == 0
    K = PEER_TOPK
    return pl.pallas_call(
        _peer_route_kernel,
        grid=(T // RT,),
        in_specs=[
            pl.BlockSpec((RT, qry.shape[1]), lambda i: (i, 0)),
            pl.BlockSpec(subkeys.shape, lambda i: (0, 0, 0, 0)),
        ],
        out_specs=[pl.BlockSpec((RT, _EDGES), lambda i: (i, 0)),
                   pl.BlockSpec((RT, PEER_HEADS, 128), lambda i: (i, 0, 0))],
        out_shape=[jax.ShapeDtypeStruct((T, _EDGES), jnp.int32),
                   jax.ShapeDtypeStruct((T, PEER_HEADS, 128), jnp.float32)],
        scratch_shapes=[
            pltpu.VMEM((K, RT), jnp.float32), pltpu.VMEM((K, RT), jnp.int32),
            pltpu.VMEM((K, RT), jnp.float32), pltpu.VMEM((K, RT), jnp.int32),
            pltpu.VMEM((K, RT), jnp.float32), pltpu.VMEM((K, RT), jnp.int32),
            pltpu.VMEM((_EDGES, RT), jnp.int32),
            pltpu.VMEM((RT, PEER_HEADS * 128), jnp.float32),
        ],
        compiler_params=pltpu.CompilerParams(dimension_semantics=("arbitrary",)),
        name="peer_route",
    )(qry, subkeys)


def _pack_kernel(t_ref, o_ref):
    b = lax.bitcast_convert_type(t_ref[...], jnp.uint32)
    r = b + jnp.uint32(0x7FFF) + ((b >> 16) & jnp.uint32(1))
    words = [(r[:, (2 * s) * 128:(2 * s + 1) * 128] >> 16)
             | (r[:, (2 * s + 1) * 128:(2 * s + 2) * 128] & jnp.uint32(0xFFFF0000)) for s in range(_TAB_ROWS)]
    w = jnp.concatenate(words, axis=1)
    o_ref[...] = w.reshape(w.shape[0] * _TAB_ROWS, 128)


def pack_table(tab):
    n, d = tab.shape
    tr = 256
    assert n % tr == 0 and d == D_MODEL
    return pl.pallas_call(
        _pack_kernel,
        grid=(n // tr,),
        in_specs=[pl.BlockSpec((tr, d), lambda i: (i, 0))],
        out_specs=pl.BlockSpec((tr * _TAB_ROWS, 128), lambda i: (i, 0)),
        out_shape=jax.ShapeDtypeStruct((n * _TAB_ROWS, 128), jnp.uint32),
        compiler_params=pltpu.CompilerParams(dimension_semantics=("arbitrary",)),
        name="pack_table",
    )(tab)


def _gather_rows(idx_ref, tab_ref, stage_ref, t):
    R = _TAB_ROWS
    for i in range(_EDGES):
        row = pl.multiple_of(idx_ref[t, i], R)
        stage_ref[R * i:R * i + R, :] = tab_ref[pl.ds(row, R), :]


def _split_bf16(x):
    hi = x.astype(jnp.bfloat16)
    lo = (x - hi.astype(jnp.float32)).astype(jnp.bfloat16)
    return jnp.concatenate([hi, lo], axis=0)


_SLOTS = 16
_AHEAD = 2


def _pipelined_tokens(n_tok, gather, compute):
    for t in range(_AHEAD):
        gather(t, t)

    def body(tt, carry):
        t0 = _SLOTS * tt
        for u in range(_SLOTS):
            gather(jnp.minimum(t0 + u + _AHEAD, n_tok - 1), (u + _AHEAD) % _SLOTS)
            compute(t0 + u, u)
        return carry
    lax.fori_loop(0, n_tok // _SLOTS, body, 0)


def _peer_dot_kernel(idx_ref, tab_ref, x_ref, a_ref, *stages):
    n = _EDGES * _ROW_CHUNKS
    sub = lax.broadcasted_iota(jnp.int32, (2 * _ROW_CHUNKS, n), 0) % _ROW_CHUNKS
    lane = lax.broadcasted_iota(jnp.int32, (2 * _ROW_CHUNKS, n), 1)
    diag = sub == lane % _ROW_CHUNKS

    def gather(t, slot):
        _gather_rows(idx_ref, tab_ref, stages[slot], t)

    def compute(t, slot):
        m = pltpu.bitcast(stages[slot][...], jnp.bfloat16)
        x16 = _split_bf16(x_ref[t])
        r = lax.dot_general(x16, m, (((1,), (1,)), ((), ())),
                            preferred_element_type=jnp.float32)
        tot = jnp.sum(jnp.where(diag, r, 0.0), axis=0, keepdims=True)
        for g in range(n // 128):
            a_ref[t, g:g + 1, :] = tot[:, g * 128:(g + 1) * 128]

    _pipelined_tokens(a_ref.shape[0], gather, compute)

    r_i = lax.broadcasted_iota(jnp.int32, (128, 128), 0) // _ROW_CHUNKS
    c_i = lax.broadcasted_iota(jnp.int32, (128, 128), 1) // _ROW_CHUNKS
    ones_bd = jnp.where(r_i == c_i, 1.0, 0.0).astype(jnp.bfloat16)
    a = a_ref[...].reshape(a_ref.shape[0] * _ROW_CHUNKS, 128)
    a = sum(jnp.dot(piece, ones_bd, preferred_element_type=jnp.float32) for piece in _split3_bf16(a))
    a_ref[...] = a.reshape(a_ref.shape)


def _peer_mix_kernel(idx_ref, tab_ref, a_ref, g_ref, x1_ref, g2_ref, y_ref, yc_ref, *stages):
    sub = lax.broadcasted_iota(jnp.int32, (_ROW_CHUNKS, 128), 0)
    lane = lax.broadcasted_iota(jnp.int32, (_ROW_CHUNKS, 128), 1)
    diag = sub == lane % _ROW_CHUNKS
    ngroups = _EDGES * _ROW_CHUNKS // 128

    def gather(t, slot):
        _gather_rows(idx_ref, tab_ref, stages[slot], t)

    def compute(t, slot):
        m = pltpu.bitcast(stages[slot][...], jnp.bfloat16)
        a = a_ref[t]
        c = g_ref[t] * (0.5 * a * (1.0 + lax.erf(a * (2.0 ** -0.5))))
        lhs = jnp.concatenate(
            [jnp.where(diag, jnp.broadcast_to(c[g:g + 1, :], (_ROW_CHUNKS, 128)), 0.0) for g in range(ngroups)],
            axis=1)
        r = jnp.dot(_split_bf16(lhs), m, preferred_element_type=jnp.float32)
        yc_ref[t] = x1_ref[t] + g2_ref[0] * (r[0:_ROW_CHUNKS] + r[_ROW_CHUNKS:])

    _pipelined_tokens(y_ref.shape[0], gather, compute)
    y_ref[...] = yc_ref[...].reshape(y_ref.shape)


def _expert_call(kern, name, idx, tab, ins, extra=(), extra_specs=(), rows_out=False):
    T = idx.shape[0]
    ET = PEER_ET
    assert T % ET == 0
    blk = pl.BlockSpec((ET, _ROW_CHUNKS, 128), lambda i: (i, 0, 0))
    stage = [pltpu.VMEM((_EDGES * _TAB_ROWS, 128), jnp.uint32)] * _SLOTS
    if rows_out:
        out_spec = pl.BlockSpec((ET, D_MODEL), lambda i: (i, 0))
        out_shape = jax.ShapeDtypeStruct((T, D_MODEL), jnp.float32)
        scratch = [pltpu.VMEM((ET, _ROW_CHUNKS, 128), jnp.float32)] + stage
    else:
        out_spec, out_shape, scratch = blk, jax.ShapeDtypeStruct((T, _ROW_CHUNKS, 128), jnp.float32), stage
    return pl.pallas_call(
        kern,
        grid=(T // ET,),
        in_specs=[
            pl.BlockSpec((ET, _EDGES), lambda i: (i, 0), memory_space=pltpu.SMEM),
            pl.BlockSpec(tab.shape, lambda i: (0, 0), pipeline_mode=pl.Buffered(1)),
        ] + [blk] * len(ins) + list(extra_specs),
        out_specs=out_spec,
        out_shape=out_shape,
        scratch_shapes=scratch,
        compiler_params=pltpu.CompilerParams(
            dimension_semantics=("arbitrary",),
            vmem_limit_bytes=_VMEM_LIMIT),
        name=name,
    )(idx, tab, *ins, *extra)


def peer_experts(h2c, idx, gexp, u, vtab, x1c, g2, seq):
    T = h2c.shape[0]
    ET = PEER_ET
    assert seq % ET == 0
    a = _expert_call(_peer_dot_kernel, "peer_dot", idx, pack_table(u), [h2c])
    g2_spec = pl.BlockSpec((1, _ROW_CHUNKS, 128), lambda i: (i * ET // seq, 0, 0))
    return _expert_call(_peer_mix_kernel, "peer_mix", idx, pack_table(vtab), [a, gexp, x1c],
                        extra=[g2.reshape(g2.shape[0], _ROW_CHUNKS, 128)], extra_specs=[g2_spec], rows_out=True)


def kernel(x, c, w_ada, b_ada, norm1_g, norm2_g, w_in, hg_lb, hg_onorm_g, q_norm_g, k_norm_g,
           w_a, w_b, w_out, peer_wq, peer_subkeys, peer_u, peer_v):
    B, S, D = x.shape
    T = B * S
    for l in range(w_ada.shape[0]):
        mod = ada_modulation(c, w_ada[l], b_ada[l])
        sh1, sc1, g1, sh2, sc2, g2 = jnp.split(mod, 6, axis=-1)
        hg, q, k, iw, iq, ik, v, gate = in_projection(x, norm1_g[l], sc1, sh1, pack_w_in(w_in[l]))
        ya = hgrn2(hg, hg_lb, hg_onorm_g[l], l)
        yb = dsa_attention(q, iq, iw, k, v, ik, q_norm_g[l], k_norm_g[l])
        x1, h2, qry = out_projection(ya, yb, gate, x, g1, norm2_g[l], sc2, sh2,
                                     w_a[l], w_b[l], w_out[l], peer_wq[l])
        idx, gexp = peer_route(qry.reshape(T, -1), peer_subkeys[l])
        x = peer_experts(h2.reshape(T, D // 128, 128), idx, gexp, peer_u[l], peer_v[l],
                         x1.reshape(T, D // 128, 128), g2, S).reshape(B, S, D)
    return x
```

```python
import functools

import jax
import jax.numpy as jnp
import numpy as np
from jax import lax
from jax.experimental import pallas as pl
from jax.experimental.pallas import tpu as pltpu

D_MODEL = 1024
HG_HEADS = 4
HG_DK = 128
HG_WIDTH = HG_HEADS * HG_DK
HG_CHUNK = 64
ATT_HEADS = 8
ATT_GROUPS = 2
ATT_HEAD_DIM = 64
ATT_REP = ATT_HEADS // ATT_GROUPS
IDX_HEADS = 8
IDX_DIM = 64
IDX_TOPK_MAX = 256
PEER_HEADS = 8
PEER_NKEYS = 128
PEER_TOPK = 16
RMS_EPS = 1e-6
IN_SPLIT = (HG_WIDTH, HG_WIDTH, HG_WIDTH, HG_WIDTH,
            ATT_HEADS * ATT_HEAD_DIM, ATT_GROUPS * ATT_HEAD_DIM, ATT_GROUPS * ATT_HEAD_DIM,
            IDX_HEADS * IDX_DIM, IDX_DIM, IDX_HEADS,
            D_MODEL, D_MODEL)

_VMEM_LIMIT = 48 * 1024 * 1024


def _ada_kernel(c_ref, w_ref, b_ref, o_ref):
    c = c_ref[...]
    cs = c * (1.0 / (1.0 + jnp.exp(-c)))
    o_ref[...] = jnp.dot(cs, w_ref[...], preferred_element_type=jnp.float32,
                         precision=lax.Precision.HIGHEST) + b_ref[...]


def ada_modulation(c, w, b):
    B, D = c.shape
    N = w.shape[1]
    tn = 1536
    assert N % tn == 0
    return pl.pallas_call(
        _ada_kernel,
        grid=(N // tn,),
        in_specs=[pl.BlockSpec((B, D), lambda j: (0, 0)),
                  pl.BlockSpec((D, tn), lambda j: (0, j)),
                  pl.BlockSpec((1, tn), lambda j: (0, j))],
        out_specs=pl.BlockSpec((B, tn), lambda j: (0, j)),
        out_shape=jax.ShapeDtypeStruct((B, N), jnp.float32),
        compiler_params=pltpu.CompilerParams(dimension_semantics=("arbitrary",),
                                             vmem_limit_bytes=_VMEM_LIMIT),
        name="ada_modulation",
    )(c, w, b.reshape(1, N))


_PROJ_SEGS = (("hg", 2048, jnp.float32), ("q", 1024, jnp.float32), ("k", 256, jnp.float32),
              ("iw", 128, jnp.float32), ("iq", 1024, jnp.bfloat16), ("ik", 128, jnp.bfloat16),
              ("v", 128, jnp.bfloat16), ("gate", 2048, jnp.float32))
PROJ_TM = 256


def pack_w_in(w_in):
    D = w_in.shape[0]
    offs = np.cumsum((0,) + IN_SPLIT)
    hq, hf, hi, hg, aq, ak, av, iq, ik, iw, ga, gb = [w_in[:, offs[i]:offs[i + 1]] for i in range(12)]

    def pad_heads(w, nh):
        w = w.reshape(D, nh, 64)
        return jnp.concatenate([w, jnp.zeros_like(w)], axis=-1).reshape(D, nh * 128)

    iwp = jnp.concatenate([iw, jnp.zeros((D, 128 - IDX_HEADS), w_in.dtype)], axis=1)
    cols = [hq, hf, hi, hg, pad_heads(aq, ATT_HEADS), pad_heads(ak, ATT_GROUPS), iwp,
            pad_heads(iq, IDX_HEADS), pad_heads(ik, 1),
            av, ga, gb]
    return jnp.concatenate(cols, axis=1).astype(jnp.bfloat16)


def _proj_kernel(x_ref, g_ref, sc_ref, sh_ref, w_ref, *o_refs):
    x = x_ref[0]
    ms = jnp.mean(x * x, axis=-1, keepdims=True)
    h = x * lax.rsqrt(ms + RMS_EPS) * g_ref[...] * (1.0 + sc_ref[0]) + sh_ref[0]
    hb = h.astype(jnp.bfloat16)
    c0 = 0
    for (_, width, _), o_ref in zip(_PROJ_SEGS, o_refs):
        for s in range(0, width, 512):
            e = min(s + 512, width)
            o_ref[0, :, s:e] = jnp.dot(hb, w_ref[:, c0 + s:c0 + e],
                                       preferred_element_type=jnp.float32).astype(o_ref.dtype)
        c0 += width


def in_projection(x, norm_g, sc, sh, w_packed):
    B, S, D = x.shape
    tm = PROJ_TM
    assert S % tm == 0
    ntot = sum(w for _, w, _ in _PROJ_SEGS)
    assert w_packed.shape == (D, ntot)
    return pl.pallas_call(
        _proj_kernel,
        grid=(B, S // tm),
        in_specs=[
            pl.BlockSpec((1, tm, D), lambda b, i: (b, i, 0)),
            pl.BlockSpec((1, D), lambda b, i: (0, 0)),
            pl.BlockSpec((1, 1, D), lambda b, i: (b, 0, 0)),
            pl.BlockSpec((1, 1, D), lambda b, i: (b, 0, 0)),
            pl.BlockSpec((D, ntot), lambda b, i: (0, 0), pipeline_mode=pl.Buffered(1)),
        ],
        out_specs=[pl.BlockSpec((1, tm, w), lambda b, i: (b, i, 0)) for _, w, _ in _PROJ_SEGS],
        out_shape=[jax.ShapeDtypeStruct((B, S, w), dt) for _, w, dt in _PROJ_SEGS],
        compiler_params=pltpu.CompilerParams(dimension_semantics=("arbitrary", "arbitrary"),
                                             vmem_limit_bytes=_VMEM_LIMIT),
        name="in_projection",
    )(x, norm_g.reshape(1, D), sc.reshape(B, 1, D), sh.reshape(B, 1, D), w_packed)


HG_LT = 512


def _split3_bf16(x):
    a = x.astype(jnp.bfloat16)
    r = x - a.astype(jnp.float32)
    b = r.astype(jnp.bfloat16)
    c = (r - b.astype(jnp.float32)).astype(jnp.bfloat16)
    return a, b, c


def _hgrn_kernel(q_ref, f_ref, i_ref, g_ref, lb_ref, og_ref, o_ref, st_ref, *, layer):
    C = HG_CHUNK
    bf = jnp.bfloat16

    @pl.when(pl.program_id(1) == 0)
    def _():
        st_ref[...] = jnp.zeros(st_ref.shape, jnp.float32)

    lbr = lb_ref[...]
    e = jnp.exp(lbr - jnp.max(lbr, axis=0, keepdims=True))
    sm = e / jnp.sum(e, axis=0, keepdims=True)
    lb_all = jnp.sum(sm[0:layer + 1], axis=0, keepdims=True)

    r_i = lax.broadcasted_iota(jnp.int32, (C, C), 0)
    c_i = lax.broadcasted_iota(jnp.int32, (C, C), 1)
    causal = c_i <= r_i
    tri = jnp.where(causal, 1.0, 0.0).astype(bf)
    og = og_ref[...]

    def chunk(c, carry):
        r0 = pl.multiple_of(c * C, C)
        H = range(HG_HEADS)
        cols = [slice(h * HG_DK, (h + 1) * HG_DK) for h in H]
        dot = functools.partial(jnp.dot, preferred_element_type=jnp.float32)
        nt = (((1,), (1,)), ((), ()))
        tn = (((0,), (0,)), ((), ()))
        qr = [q_ref[0, pl.ds(r0, C), cs] for cs in cols]
        q = [x * (1.0 / (1.0 + jnp.exp(-x))) for x in qr]
        f = [lb_all[:, cs] + (1.0 - lb_all[:, cs]) * (1.0 / (1.0 + jnp.exp(-f_ref[0, pl.ds(r0, C), cs])))
             for cs in cols]
        k = [1.0 - x for x in f]
        vb = [i_ref[0, pl.ds(r0, C), cs].astype(bf) for cs in cols]
        lg = [_split3_bf16(jnp.log(x)) for x in f]
        b = [dot(tri, l1) + dot(tri, l2) + dot(tri, l3) for l1, l2, l3 in lg]
        bm = [x[C // 2 - 1:C // 2, :] for x in b]
        bl = [x[C - 1:C, :] for x in b]
        a = [lax.dot_general((q[h] * jnp.exp(b[h] - bm[h])).astype(bf), (k[h] * jnp.exp(bm[h] - b[h])).astype(bf),
                             nt, preferred_element_type=jnp.float32) for h in H]
        st = [st_ref[h] for h in H]
        o_inter = [lax.dot_general((q[h] * jnp.exp(b[h])).astype(bf), st[h].astype(bf), nt,
                                   preferred_element_type=jnp.float32) for h in H]
        ds = [lax.dot_general(vb[h], (k[h] * jnp.exp(bl[h] - b[h])).astype(bf), tn,
                              preferred_element_type=jnp.float32) for h in H]
        for h in H:
            st_ref[h] = st[h] * jnp.exp(bl[h]) + ds[h]
        o = [dot(jnp.where(causal, a[h], 0.0).astype(bf), vb[h]) + o_inter[h] for h in H]
        for h in H:
            on = o[h] * lax.rsqrt(jnp.mean(o[h] * o[h], axis=-1, keepdims=True) + RMS_EPS) * og
            gr = g_ref[0, pl.ds(r0, C), cols[h]]
            o_ref[0, pl.ds(r0, C), cols[h]] = (on * (gr * (1.0 / (1.0 + jnp.exp(-gr))))).astype(o_ref.dtype)
        return carry
    lax.fori_loop(0, q_ref.shape[1] // C, chunk, 0)


def hgrn2(hg, hg_lb, onorm_g, layer):
    B, S, _ = hg.shape
    LT = HG_LT
    assert S % LT == 0

    def piece(p):
        return pl.BlockSpec((1, LT, HG_WIDTH), lambda b, i, p=p: (b, i, p))
    return pl.pallas_call(
        functools.partial(_hgrn_kernel, layer=layer),
        grid=(B, S // LT),
        in_specs=[piece(0), piece(1), piece(2), piece(3),
                  pl.BlockSpec(hg_lb.shape, lambda b, i: (0, 0)),
                  pl.BlockSpec((1, HG_DK), lambda b, i: (0, 0))],
        out_specs=pl.BlockSpec((1, LT, HG_WIDTH), lambda b, i: (b, i, 0)),
        out_shape=jax.ShapeDtypeStruct((B, S, HG_WIDTH), jnp.bfloat16),
        scratch_shapes=[pltpu.VMEM((HG_HEADS, HG_DK, HG_DK), jnp.float32)],
        compiler_params=pltpu.CompilerParams(dimension_semantics=("arbitrary", "arbitrary"),
                                             vmem_limit_bytes=_VMEM_LIMIT),
        name="hgrn2",
    )(hg, hg, hg, hg, hg_lb, onorm_g.reshape(1, HG_DK))


OUT_TM = 256


def _out_kernel(ya_ref, yb_ref, ga_ref, gb_ref, x_ref, g1_ref, n2_ref, sc_ref, sh_ref,
                wa_ref, wb_ref, wo_ref, wq_ref, x1_ref, h2_ref, qry_ref):
    bf = jnp.bfloat16

    def sig(z):
        return 1.0 / (1.0 + jnp.exp(-z))
    ma = jnp.dot(ya_ref[0], wa_ref[...], preferred_element_type=jnp.float32)
    mb = jnp.dot(yb_ref[0], wb_ref[...], preferred_element_type=jnp.float32)
    merged = sig(ga_ref[0]) * ma + sig(gb_ref[0]) * mb
    x1 = x_ref[0] + g1_ref[0] * jnp.dot(merged.astype(bf), wo_ref[...], preferred_element_type=jnp.float32)
    x1_ref[0] = x1
    ms = jnp.mean(x1 * x1, axis=-1, keepdims=True)
    h2 = (x1 * lax.rsqrt(ms + RMS_EPS) * n2_ref[...] * (1.0 + sc_ref[0]) + sh_ref[0]).astype(bf)
    h2_ref[0] = h2
    qry_ref[0] = jnp.dot(h2, wq_ref[...], preferred_element_type=jnp.float32)


def out_projection(ya, yb, gate, x, g1, norm2_g, sc2, sh2, w_a, w_b, w_out, wq):
    B, S, D = x.shape
    tm = OUT_TM
    bf = jnp.bfloat16
    NQ = wq.shape[1]

    def tok(width, col=0):
        return pl.BlockSpec((1, tm, width), lambda b, i, col=col: (b, i, col))

    def per_batch():
        return pl.BlockSpec((1, 1, D), lambda b, i: (b, 0, 0))

    def const(shape):
        return pl.BlockSpec(shape, lambda b, i: (0,) * len(shape), pipeline_mode=pl.Buffered(1))
    return pl.pallas_call(
        _out_kernel,
        grid=(B, S // tm),
        in_specs=[tok(HG_WIDTH), tok(512), tok(D, 0), tok(D, 1), tok(D), per_batch(),
                  const((1, D)), per_batch(), per_batch(),
                  const(w_a.shape), const(w_b.shape), const(w_out.shape), const(wq.shape)],
        out_specs=[tok(D), tok(D), tok(NQ)],
        out_shape=[jax.ShapeDtypeStruct((B, S, D), jnp.float32),
                   jax.ShapeDtypeStruct((B, S, D), jnp.bfloat16),
                   jax.ShapeDtypeStruct((B, S, NQ), jnp.float32)],
        compiler_params=pltpu.CompilerParams(dimension_semantics=("arbitrary", "arbitrary"),
                                             vmem_limit_bytes=_VMEM_LIMIT),
        name="out_projection",
    )(ya, yb, gate, gate, x, g1.reshape(B, 1, D), norm2_g.reshape(1, D), sc2.reshape(B, 1, D),
      sh2.reshape(B, 1, D), w_a.astype(bf), w_b.astype(bf), w_out.astype(bf), wq.astype(bf))


DSA_TQ = 128
DSA_KC = 512
DSA_KA = 256
_INT_MIN = -2 ** 31
_FILL_KEY = -2139095041
_NEG_BIG = -1e30


def _sortable_key(x):
    b = lax.bitcast_convert_type(x, jnp.int32)
    return b ^ ((b >> 31) & jnp.int32(0x7FFFFFFF))


def _dsa_kernel(q_ref, iq_ref, iw_ref, k_ref, v_ref, ik_ref, qg_ref, kg_ref, o_ref,
                kn_ref, vt_ref, qs_ref, keys_ref, lim_ref, m_ref, l_ref, acc_ref, sc_ref, *, seq, topk):
    TQ, KC, KA = DSA_TQ, DSA_KC, DSA_KA
    bf = jnp.bfloat16
    j = pl.program_id(1)
    n_kc = (j * TQ + TQ + KC - 1) // KC
    idx_scale = (IDX_DIM ** -0.5) * (IDX_HEADS ** -0.5)
    att_scale = ATT_HEAD_DIM ** -0.5
    nt = (((1,), (1,)), ((), ()))

    lane = lax.broadcasted_iota(jnp.int32, (1, 128), 1)
    krow = lax.broadcasted_iota(jnp.int32, (KC, 1), 0)

    def kfeat(kpos):
        a = (kpos // 64).astype(jnp.float32)
        b = (kpos % 64).astype(jnp.float32)
        return jnp.where(lane == ATT_HEAD_DIM, a, jnp.where(lane == ATT_HEAD_DIM + 1, b, 0.0))

    def qfeat(slope):
        return jnp.where(lane == ATT_HEAD_DIM, 64.0 * slope, jnp.where(lane == ATT_HEAD_DIM + 1, slope, 0.0))

    @pl.when(j == 0)
    def _():
        def body(c, carry):
            r0 = pl.multiple_of(c * KC, KC)
            for g in range(ATT_GROUPS):
                kk = k_ref[0, pl.ds(r0, KC), g * 128:(g + 1) * 128]
                ms = jnp.sum(kk * kk, axis=-1, keepdims=True) * (1.0 / ATT_HEAD_DIM)
                kn = kk * lax.rsqrt(ms + RMS_EPS) * kg_ref[...]
                kn_ref[g, pl.ds(r0, KC), :] = (kn + kfeat(r0 + krow)).astype(bf)
            vt_ref[c] = v_ref[0, pl.ds(r0, KC), :].astype(jnp.float32).T.astype(bf)
            return carry
        lax.fori_loop(0, seq // KC, body, 0)

    for h in range(ATT_HEADS):
        qq = q_ref[0, :, h * 128:(h + 1) * 128]
        ms = jnp.sum(qq * qq, axis=-1, keepdims=True) * (1.0 / ATT_HEAD_DIM)
        slope = float(2.0 ** (-8.0 * (h + 1) / ATT_HEADS))
        qs_ref[h] = (qq * lax.rsqrt(ms + RMS_EPS) * qg_ref[...] * att_scale + qfeat(slope)).astype(bf)
    iwt = iw_ref[0].T

    qpos = j * TQ + lax.broadcasted_iota(jnp.int32, (1, TQ), 1)

    def idx_body(c, carry):
        r0 = pl.multiple_of(c * KC, KC)
        ikc = ik_ref[0, pl.ds(r0, KC), :]
        acc = jnp.zeros((KC, TQ), jnp.float32)
        for h in range(IDX_HEADS):
            s = lax.dot_general(ikc, iq_ref[0, :, h * 128:(h + 1) * 128], nt,
                                preferred_element_type=jnp.float32)
            acc = acc + iwt[h:h + 1, :] * jnp.maximum(s, 0.0)
        score = acc * idx_scale
        score = jnp.where(score == 0.0, 0.0, score)
        score = jnp.where(c * KC + krow <= qpos, score, -jnp.inf)
        keys_ref[c] = _sortable_key(score)
        return carry
    lax.fori_loop(0, n_kc, idx_body, 0)

    def row_fold(m):
        parts = [m[8 * i:8 * i + 8] for i in range(m.shape[0] // 8)]
        while len(parts) > 1:
            parts = [parts[i] + parts[i + 1] for i in range(0, len(parts), 2)]
        return parts[0]

    def count(pred):
        def body(c, acc):
            return acc + row_fold(jnp.where(pred(c, keys_ref[c]), 1.0, 0.0))
        acc = lax.fori_loop(0, n_kc, body, jnp.zeros((8, TQ), jnp.float32))
        return jnp.sum(acc, axis=0, keepdims=True)

    def count_ge(cand):
        return count(lambda c, key: key >= cand)

    kf = float(topk)
    ans0 = jnp.where(count_ge(jnp.zeros((1, TQ), jnp.int32)) >= kf, jnp.int32(0), jnp.int32(_INT_MIN))

    def bit_body(i, ans):
        cand = ans | jnp.left_shift(jnp.int32(1), 30 - i)
        return jnp.where(count_ge(cand) >= kf, cand, ans)
    thr = lax.fori_loop(0, 31, bit_body, ans0)

    n_gt = count_ge(thr + 1)
    n_ge = count_ge(thr)
    need = kf - n_gt
    tie = jnp.logical_and(n_ge - n_gt > need, thr != _FILL_KEY)
    lim_ref[...] = jnp.full((1, TQ), seq, jnp.int32)

    @pl.when(jnp.max(jnp.where(tie, 1.0, 0.0)) > 0.0)
    def _():
        nbits = max(1, int(np.ceil(np.log2(seq))))

        def bit2(i, lo):
            cand = lo | jnp.left_shift(jnp.int32(1), nbits - 1 - i)
            n_before = count(lambda c, key: jnp.logical_and(key == thr, c * KC + krow < cand))
            return jnp.where(n_before < need, cand, lo)
        lo = lax.fori_loop(0, nbits, bit2, jnp.zeros((1, TQ), jnp.int32))
        lim_ref[...] = jnp.where(tie, lo, jnp.int32(seq))

    lim = lim_ref[...]

    m_ref[...] = jnp.full(m_ref.shape, _NEG_BIG, jnp.float32)
    l_ref[...] = jnp.zeros(l_ref.shape, jnp.float32)
    acc_ref[...] = jnp.zeros(acc_ref.shape, jnp.float32)
    hd = ATT_HEAD_DIM

    def att_body(c, carry):
        vtc = vt_ref[c]
        for half in range(KC // KA):
            h0 = half * KA
            r0 = pl.multiple_of(c * KC, KC) + h0
            kpos = c * KC + h0 + krow[0:KA]
            key = keys_ref[c, h0:h0 + KA, :]
            sel = jnp.logical_or(key > thr, jnp.logical_and(key == thr, kpos <= lim))
            sel = jnp.logical_and(sel, kpos <= qpos)
            madd = jnp.where(sel, 0.0, _NEG_BIG)
            mx = []
            for h in range(ATT_HEADS):
                kc = kn_ref[h // ATT_REP, pl.ds(r0, KA), :]
                sc = lax.dot_general(kc, qs_ref[h], nt, preferred_element_type=jnp.float32) + madd
                sc_ref[h] = sc
                mx.append(jnp.max(sc, axis=0, keepdims=True))
            for h in range(ATT_HEADS):
                g = h // ATT_REP
                m_old = m_ref[h]
                m_new = jnp.maximum(m_old, mx[h])
                p = jnp.exp(sc_ref[h] - m_new)
                alpha = jnp.exp(m_old - m_new)
                l_ref[h] = alpha * l_ref[h] + jnp.sum(p, axis=0, keepdims=True)
                m_ref[h] = m_new
                pv = jnp.dot(vtc[g * hd:(g + 1) * hd, h0:h0 + KA], p.astype(bf),
                             preferred_element_type=jnp.float32)
                rs = slice((h % 2) * hd, (h % 2 + 1) * hd)
                acc_ref[h // 2, rs, :] = acc_ref[h // 2, rs, :] * alpha + pv
        return carry
    lax.fori_loop(0, n_kc, att_body, 0)

    for pair in range(ATT_HEADS // 2):
        den = jnp.concatenate([jnp.broadcast_to(l_ref[2 * pair], (hd, TQ)),
                               jnp.broadcast_to(l_ref[2 * pair + 1], (hd, TQ))], axis=0)
        o_ref[0, :, pair * 128:(pair + 1) * 128] = (acc_ref[pair] / den).T.astype(o_ref.dtype)


def dsa_attention(q, iq, iw, k, v, ik, qg, kg):
    B, S, _ = q.shape
    TQ, KC = DSA_TQ, DSA_KC
    assert S % KC == 0 and S % TQ == 0
    topk = min(IDX_TOPK_MAX, S // 4)
    kern = functools.partial(_dsa_kernel, seq=S, topk=topk)

    def pad_gain(g):
        return jnp.concatenate([g, jnp.zeros_like(g)]).reshape(1, 128)
    return pl.pallas_call(
        kern,
        grid=(B, S // TQ),
        in_specs=[
            pl.BlockSpec((1, TQ, ATT_HEADS * 128), lambda b, j: (b, j, 0)),
            pl.BlockSpec((1, TQ, IDX_HEADS * 128), lambda b, j: (b, j, 0)),
            pl.BlockSpec((1, TQ, 128), lambda b, j: (b, j, 0)),
            pl.BlockSpec((1, S, ATT_GROUPS * 128), lambda b, j: (b, 0, 0)),
            pl.BlockSpec((1, S, ATT_GROUPS * ATT_HEAD_DIM), lambda b, j: (b, 0, 0)),
            pl.BlockSpec((1, S, 128), lambda b, j: (b, 0, 0)),
            pl.BlockSpec((1, 128), lambda b, j: (0, 0)),
            pl.BlockSpec((1, 128), lambda b, j: (0, 0)),
        ],
        out_specs=pl.BlockSpec((1, TQ, ATT_HEADS * ATT_HEAD_DIM), lambda b, j: (b, j, 0)),
        out_shape=jax.ShapeDtypeStruct((B, S, ATT_HEADS * ATT_HEAD_DIM), jnp.bfloat16),
        scratch_shapes=[
            pltpu.VMEM((ATT_GROUPS, S, 128), jnp.bfloat16),
            pltpu.VMEM((S // KC, ATT_GROUPS * ATT_HEAD_DIM, KC), jnp.bfloat16),
            pltpu.VMEM((ATT_HEADS, TQ, 128), jnp.bfloat16),
            pltpu.VMEM((S // KC, KC, TQ), jnp.int32),
            pltpu.VMEM((1, TQ), jnp.int32),
            pltpu.VMEM((ATT_HEADS, 1, TQ), jnp.float32),
            pltpu.VMEM((ATT_HEADS, 1, TQ), jnp.float32),
            pltpu.VMEM((ATT_HEADS // 2, 2 * ATT_HEAD_DIM, TQ), jnp.float32),
            pltpu.VMEM((ATT_HEADS, DSA_KA, TQ), jnp.float32),
        ],
        compiler_params=pltpu.CompilerParams(
            dimension_semantics=("arbitrary", "arbitrary"),
            vmem_limit_bytes=_VMEM_LIMIT),
        name="dsa_attention",
    )(q, iq, iw, k, v, ik, pad_gain(qg), pad_gain(kg))


PEER_RT = 256
_EDGES = PEER_HEADS * PEER_TOPK


def _extract_topk_rows(work, n_out, val_ref, pay_ref, order=None, payload=None):
    if order is None:
        order = lax.broadcasted_iota(jnp.int32, work.shape, 0)
    big = jnp.int32(2 ** 30)
    for it in range(n_out):
        m = jnp.max(work, axis=0, keepdims=True)
        am = jnp.min(jnp.where(work == m, order, big), axis=0, keepdims=True)
        hit = order == am
        val_ref[it:it + 1, :] = m
        if payload is None:
            pay_ref[it:it + 1, :] = am
        else:
            pay_ref[it:it + 1, :] = jnp.max(jnp.where(hit, payload, -1), axis=0, keepdims=True)
        work = jnp.where(hit, -jnp.inf, work)


def _peer_route_kernel(q_ref, sk_ref, e_ref, g_ref, s1_ref, i1_ref, s2_ref, i2_ref, ts_ref, te_ref, et_ref,
                       gt_ref):
    K = PEER_TOPK
    RT = q_ref.shape[0]
    b8 = lax.broadcasted_iota(jnp.int32, (8, RT), 0)
    b16 = lax.broadcasted_iota(jnp.int32, (K, RT), 0)
    order = jnp.concatenate([b16] + [a * K + b8 for a in range(1, 8)] + [(8 + b8) * K], axis=0)
    for h in range(PEER_HEADS):
        for p, (s_ref, i_ref) in enumerate(((s1_ref, i1_ref), (s2_ref, i2_ref))):
            col = (h * 2 + p) * 128
            qs = q_ref[:, col:col + 128].astype(jnp.bfloat16)
            sk = sk_ref[h, p].astype(jnp.bfloat16)
            sc = lax.dot_general(sk, qs, (((1,), (1,)), ((), ())),
                                 preferred_element_type=jnp.float32)
            _extract_topk_rows(sc, K, s_ref, i_ref)
        s2 = s2_ref[...]
        i2 = i2_ref[...]
        cand = [s1_ref[0:1, :] + s2]
        cidx = [i1_ref[0:1, :] * PEER_NKEYS + i2]
        for a in range(1, 8):
            cand.append(jnp.where(b8 < K // (a + 1), s1_ref[a:a + 1, :] + s2[0:8], -jnp.inf))
            cidx.append(i1_ref[a:a + 1, :] * PEER_NKEYS + i2[0:8])
        cand.append(s1_ref[8:K, :] + s2[0:1])
        cidx.append(i1_ref[8:K, :] * PEER_NKEYS + i2[0:1])
        _extract_topk_rows(jnp.concatenate(cand, axis=0), K, ts_ref, te_ref, order, jnp.concatenate(cidx, axis=0))
        ts = ts_ref[...]
        ex = jnp.exp(ts - jnp.max(ts, axis=0, keepdims=True))
        gt_ref[h * K:(h + 1) * K, :] = ex / jnp.sum(ex, axis=0, keepdims=True)
        et_ref[h * K:(h + 1) * K, :] = te_ref[...]
    e_ref[...] = et_ref[...].T
    g_ref[...] = gt_ref[...].T


def peer_route(qry, subkeys):
    T = qry.shape[0]
    RT = PEER_RT
    assert T % RT == 0
    K = PEER_TOPK
    return pl.pallas_call(
        _peer_route_kernel,
        grid=(T // RT,),
        in_specs=[
            pl.BlockSpec((RT, qry.shape[1]), lambda i: (i, 0)),
            pl.BlockSpec(subkeys.shape, lambda i: (0, 0, 0, 0)),
        ],
        out_specs=[pl.BlockSpec((RT, _EDGES), lambda i: (i, 0)),
                   pl.BlockSpec((RT, _EDGES), lambda i: (i, 0))],
        out_shape=[jax.ShapeDtypeStruct((T, _EDGES), jnp.int32),
                   jax.ShapeDtypeStruct((T, _EDGES), jnp.float32)],
        scratch_shapes=[
            pltpu.VMEM((K, RT), jnp.float32), pltpu.VMEM((K, RT), jnp.int32),
            pltpu.VMEM((K, RT), jnp.float32), pltpu.VMEM((K, RT), jnp.int32),
            pltpu.VMEM((K, RT), jnp.float32), pltpu.VMEM((K, RT), jnp.int32),
            pltpu.VMEM((_EDGES, RT), jnp.int32),
            pltpu.VMEM((_EDGES, RT), jnp.float32),
        ],
        compiler_params=pltpu.CompilerParams(dimension_semantics=("arbitrary",)),
        name="peer_route",
    )(qry, subkeys)


PEER_CT = 128
PEER_DT = 512
PEER_DE = 2048
PEER_DS = 512
_COEF_UNROLL = 32


def _cast_kernel(t_ref, o_ref):
    o_ref[...] = t_ref[...].astype(o_ref.dtype)


def _to_bf16(tab):
    n, d = tab.shape
    tr = 512
    assert n % tr == 0
    return pl.pallas_call(
        _cast_kernel,
        grid=(n // tr,),
        in_specs=[pl.BlockSpec((tr, d), lambda i: (i, 0))],
        out_specs=pl.BlockSpec((tr, d), lambda i: (i, 0)),
        out_shape=jax.ShapeDtypeStruct((n, d), jnp.bfloat16),
        compiler_params=pltpu.CompilerParams(dimension_semantics=("arbitrary",)),
        name="table_to_bf16",
    )(tab)


def _peer_coef_kernel(e_ref, g_ref, c_ref):
    nk = PEER_NKEYS
    rows = lax.broadcasted_iota(jnp.int32, (nk, _EDGES), 0)
    shift = nk.bit_length() - 1

    def body(tt, carry):
        ts = [tt * _COEF_UNROLL + u for u in range(_COEF_UNROLL)]
        es = [e_ref[pl.ds(t, 1), :] for t in ts]
        a = [jnp.where(rows == (e >> shift), g_ref[pl.ds(t, 1), :], 0.0).astype(jnp.bfloat16)
             for t, e in zip(ts, es)]
        b = [jnp.where(rows == (e & (nk - 1)), 1.0, 0.0).astype(jnp.bfloat16) for e in es]
        c = [lax.dot_general(x, y, (((1,), (1,)), ((), ())), preferred_element_type=jnp.float32)
             for x, y in zip(a, b)]
        for t, x in zip(ts, c):
            c_ref[t] = x.astype(c_ref.dtype)
        return carry
    lax.fori_loop(0, e_ref.shape[0] // _COEF_UNROLL, body, 0)


def peer_coefficients(eidx, gates):
    T = eidx.shape[0]
    CT = PEER_CT
    assert T % CT == 0 and PEER_NKEYS & (PEER_NKEYS - 1) == 0
    return pl.pallas_call(
        _peer_coef_kernel,
        grid=(T // CT,),
        in_specs=[pl.BlockSpec((CT, _EDGES), lambda i: (i, 0)),
                  pl.BlockSpec((CT, _EDGES), lambda i: (i, 0))],
        out_specs=pl.BlockSpec((CT, PEER_NKEYS, PEER_NKEYS), lambda i: (i, 0, 0)),
        out_shape=jax.ShapeDtypeStruct((T, PEER_NKEYS, PEER_NKEYS), jnp.bfloat16),
        compiler_params=pltpu.CompilerParams(dimension_semantics=("arbitrary",)),
        name="peer_coef",
    )(eidx, gates)


def _peer_dense_kernel(h_ref, c_ref, u_ref, v_ref, x1_ref, g2_ref, y_ref, acc_ref):
    j = pl.program_id(1)
    bf = jnp.bfloat16

    @pl.when(j == 0)
    def _():
        acc_ref[...] = jnp.zeros(acc_ref.shape, jnp.float32)

    h = h_ref[...]
    dt, de = h.shape[0], u_ref.shape[0]
    coef = c_ref[...].reshape(dt, de)
    for s in range(0, de, PEER_DS):
        a = lax.dot_general(h, u_ref[s:s + PEER_DS, :], (((1,), (1,)), ((), ())),
                            preferred_element_type=jnp.float32)
        act = 0.5 * a * (1.0 + lax.erf(a * (2.0 ** -0.5)))
        w = (coef[:, s:s + PEER_DS].astype(jnp.float32) * act).astype(bf)
        acc_ref[...] += jnp.dot(w, v_ref[s:s + PEER_DS, :], preferred_element_type=jnp.float32)

    @pl.when(j == pl.num_programs(1) - 1)
    def _():
        y_ref[...] = x1_ref[...] + g2_ref[0] * acc_ref[...]


def peer_experts(h2, coef, u, vtab, x1, g2, seq):
    T, D = h2.shape
    N = u.shape[0]
    DT, DE = PEER_DT, PEER_DE
    rows = DE // PEER_NKEYS
    assert T % DT == 0 and N % DE == 0 and seq % DT == 0 and N == PEER_NKEYS * PEER_NKEYS
    return pl.pallas_call(
        _peer_dense_kernel,
        grid=(T // DT, N // DE),
        in_specs=[
            pl.BlockSpec((DT, D), lambda i, j: (i, 0)),
            pl.BlockSpec((DT, rows, PEER_NKEYS), lambda i, j: (i, j, 0)),
            pl.BlockSpec((DE, D), lambda i, j: (j, 0)),
            pl.BlockSpec((DE, D), lambda i, j: (j, 0)),
            pl.BlockSpec((DT, D), lambda i, j: (i, 0)),
            pl.BlockSpec((1, 1, D), lambda i, j: (i * DT // seq, 0, 0)),
        ],
        out_specs=pl.BlockSpec((DT, D), lambda i, j: (i, 0)),
        out_shape=jax.ShapeDtypeStruct((T, D), jnp.float32),
        scratch_shapes=[pltpu.VMEM((DT, D), jnp.float32)],
        compiler_params=pltpu.CompilerParams(dimension_semantics=("arbitrary", "arbitrary"),
                                             vmem_limit_bytes=56 * 1024 * 1024),
        name="peer_dense",
    )(h2, coef, _to_bf16(u), _to_bf16(vtab), x1, g2.reshape(g2.shape[0], 1, D))


def kernel(x, c, w_ada, b_ada, norm1_g, norm2_g, w_in, hg_lb, hg_onorm_g, q_norm_g, k_norm_g,
           w_a, w_b, w_out, peer_wq, peer_subkeys, peer_u, peer_v):
    B, S, D = x.shape
    T = B * S
    for l in range(w_ada.shape[0]):
        mod = ada_modulation(c, w_ada[l], b_ada[l])
        sh1, sc1, g1, sh2, sc2, g2 = jnp.split(mod, 6, axis=-1)
        hg, q, k, iw, iq, ik, v, gate = in_projection(x, norm1_g[l], sc1, sh1, pack_w_in(w_in[l]))
        ya = hgrn2(hg, hg_lb, hg_onorm_g[l], l)
        yb = dsa_attention(q, iq, iw, k, v, ik, q_norm_g[l], k_norm_g[l])
        x1, h2, qry = out_projection(ya, yb, gate, x, g1, norm2_g[l], sc2, sh2,
                                     w_a[l], w_b[l], w_out[l], peer_wq[l])
        eidx, gates = peer_route(qry.reshape(T, -1), peer_subkeys[l])
        x = peer_experts(h2.reshape(T, D), peer_coefficients(eidx, gates), peer_u[l], peer_v[l],
                         x1.reshape(T, D), g2, S).reshape(B, S, D)
    return x
```

```python
import functools

import jax
import jax.numpy as jnp
import numpy as np
from jax import lax
from jax.experimental import pallas as pl
from jax.experimental.pallas import tpu as pltpu

D_MODEL = 1024
HG_HEADS = 4
HG_DK = 128
HG_WIDTH = HG_HEADS * HG_DK
HG_CHUNK = 64
ATT_HEADS = 8
ATT_GROUPS = 2
ATT_HEAD_DIM = 64
ATT_REP = ATT_HEADS // ATT_GROUPS
IDX_HEADS = 8
IDX_DIM = 64
IDX_TOPK_MAX = 256
PEER_HEADS = 8
PEER_NKEYS = 128
PEER_TOPK = 16
RMS_EPS = 1e-6
IN_SPLIT = (HG_WIDTH, HG_WIDTH, HG_WIDTH, HG_WIDTH,
            ATT_HEADS * ATT_HEAD_DIM, ATT_GROUPS * ATT_HEAD_DIM, ATT_GROUPS * ATT_HEAD_DIM,
            IDX_HEADS * IDX_DIM, IDX_DIM, IDX_HEADS,
            D_MODEL, D_MODEL)

_VMEM_LIMIT = 48 * 1024 * 1024


def _ada_kernel(c_ref, w_ref, b_ref, o_ref):
    c = c_ref[...]
    cs = c * (1.0 / (1.0 + jnp.exp(-c)))
    o_ref[...] = jnp.dot(cs, w_ref[...], preferred_element_type=jnp.float32,
                         precision=lax.Precision.HIGHEST) + b_ref[...]


def ada_modulation(c, w, b):
    B, D = c.shape
    N = w.shape[1]
    tn = 1536
    assert N % tn == 0
    return pl.pallas_call(
        _ada_kernel,
        grid=(N // tn,),
        in_specs=[pl.BlockSpec((B, D), lambda j: (0, 0)),
                  pl.BlockSpec((D, tn), lambda j: (0, j)),
                  pl.BlockSpec((1, tn), lambda j: (0, j))],
        out_specs=pl.BlockSpec((B, tn), lambda j: (0, j)),
        out_shape=jax.ShapeDtypeStruct((B, N), jnp.float32),
        compiler_params=pltpu.CompilerParams(dimension_semantics=("arbitrary",),
                                             vmem_limit_bytes=_VMEM_LIMIT),
        name="ada_modulation",
    )(c, w, b.reshape(1, N))


_PROJ_SEGS = (("hg", 2048, jnp.float32), ("q", 1024, jnp.float32), ("k", 256, jnp.float32),
              ("iw", 128, jnp.float32), ("iq", 1024, jnp.bfloat16), ("ik", 128, jnp.bfloat16),
              ("v", 128, jnp.bfloat16), ("gate", 2048, jnp.float32))
PROJ_TM = 256


def pack_w_in(w_in):
    D = w_in.shape[0]
    offs = np.cumsum((0,) + IN_SPLIT)
    hq, hf, hi, hg, aq, ak, av, iq, ik, iw, ga, gb = [w_in[:, offs[i]:offs[i + 1]] for i in range(12)]

    def pad_heads(w, nh):
        w = w.reshape(D, nh, 64)
        return jnp.concatenate([w, jnp.zeros_like(w)], axis=-1).reshape(D, nh * 128)

    iwp = jnp.concatenate([iw, jnp.zeros((D, 128 - IDX_HEADS), w_in.dtype)], axis=1)
    cols = [hq, hf, hi, hg, pad_heads(aq, ATT_HEADS), pad_heads(ak, ATT_GROUPS), iwp,
            pad_heads(iq, IDX_HEADS), pad_heads(ik, 1),
            av, ga, gb]
    return jnp.concatenate(cols, axis=1).astype(jnp.bfloat16)


def _proj_kernel(x_ref, g_ref, sc_ref, sh_ref, w_ref, *o_refs):
    x = x_ref[0]
    ms = jnp.mean(x * x, axis=-1, keepdims=True)
    h = x * lax.rsqrt(ms + RMS_EPS) * g_ref[...] * (1.0 + sc_ref[0]) + sh_ref[0]
    hb = h.astype(jnp.bfloat16)
    c0 = 0
    for (_, width, _), o_ref in zip(_PROJ_SEGS, o_refs):
        for s in range(0, width, 512):
            e = min(s + 512, width)
            o_ref[0, :, s:e] = jnp.dot(hb, w_ref[:, c0 + s:c0 + e],
                                       preferred_element_type=jnp.float32).astype(o_ref.dtype)
        c0 += width


def in_projection(x, norm_g, sc, sh, w_packed):
    B, S, D = x.shape
    tm = PROJ_TM
    assert S % tm == 0
    ntot = sum(w for _, w, _ in _PROJ_SEGS)
    assert w_packed.shape == (D, ntot)
    return pl.pallas_call(
        _proj_kernel,
        grid=(B, S // tm),
        in_specs=[
            pl.BlockSpec((1, tm, D), lambda b, i: (b, i, 0)),
            pl.BlockSpec((1, D), lambda b, i: (0, 0)),
            pl.BlockSpec((1, 1, D), lambda b, i: (b, 0, 0)),
            pl.BlockSpec((1, 1, D), lambda b, i: (b, 0, 0)),
            pl.BlockSpec((D, ntot), lambda b, i: (0, 0), pipeline_mode=pl.Buffered(1)),
        ],
        out_specs=[pl.BlockSpec((1, tm, w), lambda b, i: (b, i, 0)) for _, w, _ in _PROJ_SEGS],
        out_shape=[jax.ShapeDtypeStruct((B, S, w), dt) for _, w, dt in _PROJ_SEGS],
        compiler_params=pltpu.CompilerParams(dimension_semantics=("arbitrary", "arbitrary"),
                                             vmem_limit_bytes=_VMEM_LIMIT),
        name="in_projection",
    )(x, norm_g.reshape(1, D), sc.reshape(B, 1, D), sh.reshape(B, 1, D), w_packed)


HG_LT = 512


def _split3_bf16(x):
    a = x.astype(jnp.bfloat16)
    r = x - a.astype(jnp.float32)
    b = r.astype(jnp.bfloat16)
    c = (r - b.astype(jnp.float32)).astype(jnp.bfloat16)
    return a, b, c


def _hgrn_kernel(q_ref, f_ref, i_ref, g_ref, lb_ref, og_ref, o_ref, st_ref, *, layer):
    C = HG_CHUNK
    bf = jnp.bfloat16

    @pl.when(pl.program_id(1) == 0)
    def _():
        st_ref[...] = jnp.zeros(st_ref.shape, jnp.float32)

    lbr = lb_ref[...]
    e = jnp.exp(lbr - jnp.max(lbr, axis=0, keepdims=True))
    sm = e / jnp.sum(e, axis=0, keepdims=True)
    lb_all = jnp.sum(sm[0:layer + 1], axis=0, keepdims=True)

    r_i = lax.broadcasted_iota(jnp.int32, (C, C), 0)
    c_i = lax.broadcasted_iota(jnp.int32, (C, C), 1)
    causal = c_i <= r_i
    tri = jnp.where(causal, 1.0, 0.0).astype(bf)
    og = og_ref[...]

    def chunk(c, carry):
        r0 = pl.multiple_of(c * C, C)
        H = range(HG_HEADS)
        cols = [slice(h * HG_DK, (h + 1) * HG_DK) for h in H]
        dot = functools.partial(jnp.dot, preferred_element_type=jnp.float32)
        nt = (((1,), (1,)), ((), ()))
        tn = (((0,), (0,)), ((), ()))
        qr = [q_ref[0, pl.ds(r0, C), cs] for cs in cols]
        q = [x * (1.0 / (1.0 + jnp.exp(-x))) for x in qr]
        f = [lb_all[:, cs] + (1.0 - lb_all[:, cs]) * (1.0 / (1.0 + jnp.exp(-f_ref[0, pl.ds(r0, C), cs])))
             for cs in cols]
        k = [1.0 - x for x in f]
        vb = [i_ref[0, pl.ds(r0, C), cs].astype(bf) for cs in cols]
        lg = [_split3_bf16(jnp.log(x)) for x in f]
        b = [dot(tri, l1) + dot(tri, l2) + dot(tri, l3) for l1, l2, l3 in lg]
        bm = [x[C // 2 - 1:C // 2, :] for x in b]
        bl = [x[C - 1:C, :] for x in b]
        a = [lax.dot_general((q[h] * jnp.exp(b[h] - bm[h])).astype(bf), (k[h] * jnp.exp(bm[h] - b[h])).astype(bf),
                             nt, preferred_element_type=jnp.float32) for h in H]
        st = [st_ref[h] for h in H]
        o_inter = [lax.dot_general((q[h] * jnp.exp(b[h])).astype(bf), st[h].astype(bf), nt,
                                   preferred_element_type=jnp.float32) for h in H]
        ds = [lax.dot_general(vb[h], (k[h] * jnp.exp(bl[h] - b[h])).astype(bf), tn,
                              preferred_element_type=jnp.float32) for h in H]
        for h in H:
            st_ref[h] = st[h] * jnp.exp(bl[h]) + ds[h]
        o = [dot(jnp.where(causal, a[h], 0.0).astype(bf), vb[h]) + o_inter[h] for h in H]
        for h in H:
            on = o[h] * lax.rsqrt(jnp.mean(o[h] * o[h], axis=-1, keepdims=True) + RMS_EPS) * og
            gr = g_ref[0, pl.ds(r0, C), cols[h]]
            o_ref[0, pl.ds(r0, C), cols[h]] = (on * (gr * (1.0 / (1.0 + jnp.exp(-gr))))).astype(o_ref.dtype)
        return carry
    lax.fori_loop(0, q_ref.shape[1] // C, chunk, 0)


def hgrn2(hg, hg_lb, onorm_g, layer):
    B, S, _ = hg.shape
    LT = HG_LT
    assert S % LT == 0

    def piece(p):
        return pl.BlockSpec((1, LT, HG_WIDTH), lambda b, i, p=p: (b, i, p))
    return pl.pallas_call(
        functools.partial(_hgrn_kernel, layer=layer),
        grid=(B, S // LT),
        in_specs=[piece(0), piece(1), piece(2), piece(3),
                  pl.BlockSpec(hg_lb.shape, lambda b, i: (0, 0)),
                  pl.BlockSpec((1, HG_DK), lambda b, i: (0, 0))],
        out_specs=pl.BlockSpec((1, LT, HG_WIDTH), lambda b, i: (b, i, 0)),
        out_shape=jax.ShapeDtypeStruct((B, S, HG_WIDTH), jnp.bfloat16),
        scratch_shapes=[pltpu.VMEM((HG_HEADS, HG_DK, HG_DK), jnp.float32)],
        compiler_params=pltpu.CompilerParams(dimension_semantics=("arbitrary", "arbitrary"),
                                             vmem_limit_bytes=_VMEM_LIMIT),
        name="hgrn2",
    )(hg, hg, hg, hg, hg_lb, onorm_g.reshape(1, HG_DK))


OUT_TM = 256


def _out_kernel(ya_ref, yb_ref, ga_ref, gb_ref, x_ref, g1_ref, n2_ref, sc_ref, sh_ref,
                wa_ref, wb_ref, wo_ref, wq_ref, x1_ref, h2_ref, qry_ref):
    bf = jnp.bfloat16

    def sig(z):
        return 1.0 / (1.0 + jnp.exp(-z))
    ma = jnp.dot(ya_ref[0], wa_ref[...], preferred_element_type=jnp.float32)
    mb = jnp.dot(yb_ref[0], wb_ref[...], preferred_element_type=jnp.float32)
    merged = sig(ga_ref[0]) * ma + sig(gb_ref[0]) * mb
    x1 = x_ref[0] + g1_ref[0] * jnp.dot(merged.astype(bf), wo_ref[...], preferred_element_type=jnp.float32)
    x1_ref[0] = x1
    ms = jnp.mean(x1 * x1, axis=-1, keepdims=True)
    h2 = (x1 * lax.rsqrt(ms + RMS_EPS) * n2_ref[...] * (1.0 + sc_ref[0]) + sh_ref[0]).astype(bf)
    h2_ref[0] = h2
    qry_ref[0] = jnp.dot(h2, wq_ref[...], preferred_element_type=jnp.float32)


def out_projection(ya, yb, gate, x, g1, norm2_g, sc2, sh2, w_a, w_b, w_out, wq):
    B, S, D = x.shape
    tm = OUT_TM
    bf = jnp.bfloat16
    NQ = wq.shape[1]

    def tok(width, col=0):
        return pl.BlockSpec((1, tm, width), lambda b, i, col=col: (b, i, col))

    def per_batch():
        return pl.BlockSpec((1, 1, D), lambda b, i: (b, 0, 0))

    def const(shape):
        return pl.BlockSpec(shape, lambda b, i: (0,) * len(shape), pipeline_mode=pl.Buffered(1))
    return pl.pallas_call(
        _out_kernel,
        grid=(B, S // tm),
        in_specs=[tok(HG_WIDTH), tok(512), tok(D, 0), tok(D, 1), tok(D), per_batch(),
                  const((1, D)), per_batch(), per_batch(),
                  const(w_a.shape), const(w_b.shape), const(w_out.shape), const(wq.shape)],
        out_specs=[tok(D), tok(D), tok(NQ)],
        out_shape=[jax.ShapeDtypeStruct((B, S, D), jnp.float32),
                   jax.ShapeDtypeStruct((B, S, D), jnp.bfloat16),
                   jax.ShapeDtypeStruct((B, S, NQ), jnp.float32)],
        compiler_params=pltpu.CompilerParams(dimension_semantics=("arbitrary", "arbitrary"),
                                             vmem_limit_bytes=_VMEM_LIMIT),
        name="out_projection",
    )(ya, yb, gate, gate, x, g1.reshape(B, 1, D), norm2_g.reshape(1, D), sc2.reshape(B, 1, D),
      sh2.reshape(B, 1, D), w_a.astype(bf), w_b.astype(bf), w_out.astype(bf), wq.astype(bf))


DSA_TQ = 128
DSA_KC = 512
DSA_KA = 256
_INT_MIN = -2 ** 31
_FILL_KEY = -2139095041
_NEG_BIG = -1e30


def _sortable_key(x):
    b = lax.bitcast_convert_type(x, jnp.int32)
    return b ^ ((b >> 31) & jnp.int32(0x7FFFFFFF))


def _dsa_kernel(q_ref, iq_ref, iw_ref, k_ref, v_ref, ik_ref, qg_ref, kg_ref, o_ref,
                kn_ref, vt_ref, qs_ref, keys_ref, lim_ref, m_ref, l_ref, acc_ref, sc_ref, *, seq, topk):
    TQ, KC, KA = DSA_TQ, DSA_KC, DSA_KA
    bf = jnp.bfloat16
    j = pl.program_id(1)
    n_kc = (j * TQ + TQ + KC - 1) // KC
    idx_scale = (IDX_DIM ** -0.5) * (IDX_HEADS ** -0.5)
    att_scale = ATT_HEAD_DIM ** -0.5
    nt = (((1,), (1,)), ((), ()))

    lane = lax.broadcasted_iota(jnp.int32, (1, 128), 1)
    krow = lax.broadcasted_iota(jnp.int32, (KC, 1), 0)

    def kfeat(kpos):
        a = (kpos // 64).astype(jnp.float32)
        b = (kpos % 64).astype(jnp.float32)
        return jnp.where(lane == ATT_HEAD_DIM, a, jnp.where(lane == ATT_HEAD_DIM + 1, b, 0.0))

    def qfeat(slope):
        return jnp.where(lane == ATT_HEAD_DIM, 64.0 * slope, jnp.where(lane == ATT_HEAD_DIM + 1, slope, 0.0))

    @pl.when(j == 0)
    def _():
        def body(c, carry):
            r0 = pl.multiple_of(c * KC, KC)
            for g in range(ATT_GROUPS):
                kk = k_ref[0, pl.ds(r0, KC), g * 128:(g + 1) * 128]
                ms = jnp.sum(kk * kk, axis=-1, keepdims=True) * (1.0 / ATT_HEAD_DIM)
                kn = kk * lax.rsqrt(ms + RMS_EPS) * kg_ref[...]
                kn_ref[g, pl.ds(r0, KC), :] = (kn + kfeat(r0 + krow)).astype(bf)
            vt_ref[c] = v_ref[0, pl.ds(r0, KC), :].astype(jnp.float32).T.astype(bf)
            return carry
        lax.fori_loop(0, seq // KC, body, 0)

    for h in range(ATT_HEADS):
        qq = q_ref[0, :, h * 128:(h + 1) * 128]
        ms = jnp.sum(qq * qq, axis=-1, keepdims=True) * (1.0 / ATT_HEAD_DIM)
        slope = float(2.0 ** (-8.0 * (h + 1) / ATT_HEADS))
        qs_ref[h] = (qq * lax.rsqrt(ms + RMS_EPS) * qg_ref[...] * att_scale + qfeat(slope)).astype(bf)
    iwt = iw_ref[0].T

    qpos = j * TQ + lax.broadcasted_iota(jnp.int32, (1, TQ), 1)

    def idx_body(c, carry):
        r0 = pl.multiple_of(c * KC, KC)
        ikc = ik_ref[0, pl.ds(r0, KC), :]
        acc = jnp.zeros((KC, TQ), jnp.float32)
        for h in range(IDX_HEADS):
            s = lax.dot_general(ikc, iq_ref[0, :, h * 128:(h + 1) * 128], nt,
                                preferred_element_type=jnp.float32)
            acc = acc + iwt[h:h + 1, :] * jnp.maximum(s, 0.0)
        score = acc * idx_scale
        score = jnp.where(score == 0.0, 0.0, score)
        score = jnp.where(c * KC + krow <= qpos, score, -jnp.inf)
        keys_ref[c] = _sortable_key(score)
        return carry
    lax.fori_loop(0, n_kc, idx_body, 0)

    def row_fold(m):
        parts = [m[8 * i:8 * i + 8] for i in range(m.shape[0] // 8)]
        while len(parts) > 1:
            parts = [parts[i] + parts[i + 1] for i in range(0, len(parts), 2)]
        return parts[0]

    def count(pred):
        def body(c, acc):
            return acc + row_fold(jnp.where(pred(c, keys_ref[c]), 1.0, 0.0))
        acc = lax.fori_loop(0, n_kc, body, jnp.zeros((8, TQ), jnp.float32))
        return jnp.sum(acc, axis=0, keepdims=True)

    def count_ge(cand):
        return count(lambda c, key: key >= cand)

    kf = float(topk)
    ans0 = jnp.where(count_ge(jnp.zeros((1, TQ), jnp.int32)) >= kf, jnp.int32(0), jnp.int32(_INT_MIN))

    def bit_body(i, ans):
        cand = ans | jnp.left_shift(jnp.int32(1), 30 - i)
        return jnp.where(count_ge(cand) >= kf, cand, ans)
    thr = lax.fori_loop(0, 31, bit_body, ans0)

    n_gt = count_ge(thr + 1)
    n_ge = count_ge(thr)
    need = kf - n_gt
    tie = jnp.logical_and(n_ge - n_gt > need, thr != _FILL_KEY)
    lim_ref[...] = jnp.full((1, TQ), seq, jnp.int32)

    @pl.when(jnp.max(jnp.where(tie, 1.0, 0.0)) > 0.0)
    def _():
        nbits = max(1, int(np.ceil(np.log2(seq))))

        def bit2(i, lo):
            cand = lo | jnp.left_shift(jnp.int32(1), nbits - 1 - i)
            n_before = count(lambda c, key: jnp.logical_and(key == thr, c * KC + krow < cand))
            return jnp.where(n_before < need, cand, lo)
        lo = lax.fori_loop(0, nbits, bit2, jnp.zeros((1, TQ), jnp.int32))
        lim_ref[...] = jnp.where(tie, lo, jnp.int32(seq))

    lim = lim_ref[...]

    m_ref[...] = jnp.full(m_ref.shape, _NEG_BIG, jnp.float32)
    l_ref[...] = jnp.zeros(l_ref.shape, jnp.float32)
    acc_ref[...] = jnp.zeros(acc_ref.shape, jnp.float32)
    hd = ATT_HEAD_DIM

    def att_body(c, carry):
        vtc = vt_ref[c]
        for half in range(KC // KA):
            h0 = half * KA
            r0 = pl.multiple_of(c * KC, KC) + h0
            kpos = c * KC + h0 + krow[0:KA]
            key = keys_ref[c, h0:h0 + KA, :]
            sel = jnp.logical_or(key > thr, jnp.logical_and(key == thr, kpos <= lim))
            sel = jnp.logical_and(sel, kpos <= qpos)
            madd = jnp.where(sel, 0.0, _NEG_BIG)
            mx = []
            for h in range(ATT_HEADS):
                kc = kn_ref[h // ATT_REP, pl.ds(r0, KA), :]
                sc = lax.dot_general(kc, qs_ref[h], nt, preferred_element_type=jnp.float32) + madd
                sc_ref[h] = sc
                mx.append(jnp.max(sc, axis=0, keepdims=True))
            for h in range(ATT_HEADS):
                g = h // ATT_REP
                m_old = m_ref[h]
                m_new = jnp.maximum(m_old, mx[h])
                p = jnp.exp(sc_ref[h] - m_new)
                alpha = jnp.exp(m_old - m_new)
                l_ref[h] = alpha * l_ref[h] + jnp.sum(p, axis=0, keepdims=True)
                m_ref[h] = m_new
                pv = jnp.dot(vtc[g * hd:(g + 1) * hd, h0:h0 + KA], p.astype(bf),
                             preferred_element_type=jnp.float32)
                rs = slice((h % 2) * hd, (h % 2 + 1) * hd)
                acc_ref[h // 2, rs, :] = acc_ref[h // 2, rs, :] * alpha + pv
        return carry
    lax.fori_loop(0, n_kc, att_body, 0)

    for pair in range(ATT_HEADS // 2):
        den = jnp.concatenate([jnp.broadcast_to(l_ref[2 * pair], (hd, TQ)),
                               jnp.broadcast_to(l_ref[2 * pair + 1], (hd, TQ))], axis=0)
        o_ref[0, :, pair * 128:(pair + 1) * 128] = (acc_ref[pair] / den).T.astype(o_ref.dtype)


def dsa_attention(q, iq, iw, k, v, ik, qg, kg):
    B, S, _ = q.shape
    TQ, KC = DSA_TQ, DSA_KC
    assert S % KC == 0 and S % TQ == 0
    topk = min(IDX_TOPK_MAX, S // 4)
    kern = functools.partial(_dsa_kernel, seq=S, topk=topk)

    def pad_gain(g):
        return jnp.concatenate([g, jnp.zeros_like(g)]).reshape(1, 128)
    return pl.pallas_call(
        kern,
        grid=(B, S // TQ),
        in_specs=[
            pl.BlockSpec((1, TQ, ATT_HEADS * 128), lambda b, j: (b, j, 0)),
            pl.BlockSpec((1, TQ, IDX_HEADS * 128), lambda b, j: (b, j, 0)),
            pl.BlockSpec((1, TQ, 128), lambda b, j: (b, j, 0)),
            pl.BlockSpec((1, S, ATT_GROUPS * 128), lambda b, j: (b, 0, 0)),
            pl.BlockSpec((1, S, ATT_GROUPS * ATT_HEAD_DIM), lambda b, j: (b, 0, 0)),
            pl.BlockSpec((1, S, 128), lambda b, j: (b, 0, 0)),
            pl.BlockSpec((1, 128), lambda b, j: (0, 0)),
            pl.BlockSpec((1, 128), lambda b, j: (0, 0)),
        ],
        out_specs=pl.BlockSpec((1, TQ, ATT_HEADS * ATT_HEAD_DIM), lambda b, j: (b, j, 0)),
        out_shape=jax.ShapeDtypeStruct((B, S, ATT_HEADS * ATT_HEAD_DIM), jnp.bfloat16),
        scratch_shapes=[
            pltpu.VMEM((ATT_GROUPS, S, 128), jnp.bfloat16),
            pltpu.VMEM((S // KC, ATT_GROUPS * ATT_HEAD_DIM, KC), jnp.bfloat16),
            pltpu.VMEM((ATT_HEADS, TQ, 128), jnp.bfloat16),
            pltpu.VMEM((S // KC, KC, TQ), jnp.int32),
            pltpu.VMEM((1, TQ), jnp.int32),
            pltpu.VMEM((ATT_HEADS, 1, TQ), jnp.float32),
            pltpu.VMEM((ATT_HEADS, 1, TQ), jnp.float32),
            pltpu.VMEM((ATT_HEADS // 2, 2 * ATT_HEAD_DIM, TQ), jnp.float32),
            pltpu.VMEM((ATT_HEADS, DSA_KA, TQ), jnp.float32),
        ],
        compiler_params=pltpu.CompilerParams(
            dimension_semantics=("arbitrary", "arbitrary"),
            vmem_limit_bytes=_VMEM_LIMIT),
        name="dsa_attention",
    )(q, iq, iw, k, v, ik, pad_gain(qg), pad_gain(kg))


PEER_RT = 256
_EDGES = PEER_HEADS * PEER_TOPK


def _extract_topk_rows(work, n_out, val_ref, pay_ref, order=None, payload=None):
    if order is None:
        order = lax.broadcasted_iota(jnp.int32, work.shape, 0)
    big = jnp.int32(2 ** 30)
    for it in range(n_out):
        m = jnp.max(work, axis=0, keepdims=True)
        am = jnp.min(jnp.where(work == m, order, big), axis=0, keepdims=True)
        hit = order == am
        val_ref[it:it + 1, :] = m
        if payload is None:
            pay_ref[it:it + 1, :] = am
        else:
            pay_ref[it:it + 1, :] = jnp.max(jnp.where(hit, payload, -1), axis=0, keepdims=True)
        work = jnp.where(hit, -jnp.inf, work)


def _peer_route_kernel(q_ref, sk_ref, e_ref, g_ref, s1_ref, i1_ref, s2_ref, i2_ref, ts_ref, te_ref, et_ref,
                       gt_ref):
    K = PEER_TOPK
    RT = q_ref.shape[0]
    b8 = lax.broadcasted_iota(jnp.int32, (8, RT), 0)
    b16 = lax.broadcasted_iota(jnp.int32, (K, RT), 0)
    order = jnp.concatenate([b16] + [a * K + b8 for a in range(1, 8)] + [(8 + b8) * K], axis=0)
    for h in range(PEER_HEADS):
        for p, (s_ref, i_ref) in enumerate(((s1_ref, i1_ref), (s2_ref, i2_ref))):
            col = (h * 2 + p) * 128
            qs = q_ref[:, col:col + 128].astype(jnp.bfloat16)
            sk = sk_ref[h, p].astype(jnp.bfloat16)
            sc = lax.dot_general(sk, qs, (((1,), (1,)), ((), ())),
                                 preferred_element_type=jnp.float32)
            _extract_topk_rows(sc, K, s_ref, i_ref)
        s2 = s2_ref[...]
        i2 = i2_ref[...]
        cand = [s1_ref[0:1, :] + s2]
        cidx = [i1_ref[0:1, :] * PEER_NKEYS + i2]
        for a in range(1, 8):
            cand.append(jnp.where(b8 < K // (a + 1), s1_ref[a:a + 1, :] + s2[0:8], -jnp.inf))
            cidx.append(i1_ref[a:a + 1, :] * PEER_NKEYS + i2[0:8])
        cand.append(s1_ref[8:K, :] + s2[0:1])
        cidx.append(i1_ref[8:K, :] * PEER_NKEYS + i2[0:1])
        _extract_topk_rows(jnp.concatenate(cand, axis=0), K, ts_ref, te_ref, order, jnp.concatenate(cidx, axis=0))
        ts = ts_ref[...]
        ex = jnp.exp(ts - jnp.max(ts, axis=0, keepdims=True))
        gt_ref[h * K:(h + 1) * K, :] = ex / jnp.sum(ex, axis=0, keepdims=True)
        et_ref[h * K:(h + 1) * K, :] = te_ref[...]
    e_ref[...] = et_ref[...].T
    g_ref[...] = gt_ref[...].T


def peer_route(qry, subkeys):
    T = qry.shape[0]
    RT = PEER_RT
    assert T % RT == 0
    K = PEER_TOPK
    return pl.pallas_call(
        _peer_route_kernel,
        grid=(T // RT,),
        in_specs=[
            pl.BlockSpec((RT, qry.shape[1]), lambda i: (i, 0)),
            pl.BlockSpec(subkeys.shape, lambda i: (0, 0, 0, 0)),
        ],
        out_specs=[pl.BlockSpec((RT, _EDGES), lambda i: (i, 0)),
                   pl.BlockSpec((RT, _EDGES), lambda i: (i, 0))],
        out_shape=[jax.ShapeDtypeStruct((T, _EDGES), jnp.int32),
                   jax.ShapeDtypeStruct((T, _EDGES), jnp.float32)],
        scratch_shapes=[
            pltpu.VMEM((K, RT), jnp.float32), pltpu.VMEM((K, RT), jnp.int32),
            pltpu.VMEM((K, RT), jnp.float32), pltpu.VMEM((K, RT), jnp.int32),
            pltpu.VMEM((K, RT), jnp.float32), pltpu.VMEM((K, RT), jnp.int32),
            pltpu.VMEM((_EDGES, RT), jnp.int32),
            pltpu.VMEM((_EDGES, RT), jnp.float32),
        ],
        compiler_params=pltpu.CompilerParams(dimension_semantics=("arbitrary",)),
        name="peer_route",
    )(qry, subkeys)


PEER_CT = 128
PEER_DT = 512
PEER_DE = 2048
PEER_DS = 2048
_COEF_UNROLL = 32


def _cast_kernel(t_ref, o_ref):
    o_ref[...] = t_ref[...].astype(o_ref.dtype)


def _to_bf16(tab):
    n, d = tab.shape
    tr = 512
    assert n % tr == 0
    return pl.pallas_call(
        _cast_kernel,
        grid=(n // tr,),
        in_specs=[pl.BlockSpec((tr, d), lambda i: (i, 0))],
        out_specs=pl.BlockSpec((tr, d), lambda i: (i, 0)),
        out_shape=jax.ShapeDtypeStruct((n, d), jnp.bfloat16),
        compiler_params=pltpu.CompilerParams(dimension_semantics=("arbitrary",)),
        name="table_to_bf16",
    )(tab)


def _peer_coef_kernel(e_ref, g_ref, c_ref):
    nk = PEER_NKEYS
    rows = lax.broadcasted_iota(jnp.int32, (nk, _EDGES), 0)
    shift = nk.bit_length() - 1

    def body(tt, carry):
        ts = [tt * _COEF_UNROLL + u for u in range(_COEF_UNROLL)]
        es = [e_ref[pl.ds(t, 1), :] for t in ts]
        a = [jnp.where(rows == (e >> shift), g_ref[pl.ds(t, 1), :], 0.0).astype(jnp.bfloat16)
             for t, e in zip(ts, es)]
        b = [jnp.where(rows == (e & (nk - 1)), 1.0, 0.0).astype(jnp.bfloat16) for e in es]
        c = [lax.dot_general(x, y, (((1,), (1,)), ((), ())), preferred_element_type=jnp.float32)
             for x, y in zip(a, b)]
        for t, x in zip(ts, c):
            c_ref[t] = x.astype(c_ref.dtype)
        return carry
    lax.fori_loop(0, e_ref.shape[0] // _COEF_UNROLL, body, 0)


def peer_coefficients(eidx, gates):
    T = eidx.shape[0]
    CT = PEER_CT
    assert T % CT == 0 and PEER_NKEYS & (PEER_NKEYS - 1) == 0
    return pl.pallas_call(
        _peer_coef_kernel,
        grid=(T // CT,),
        in_specs=[pl.BlockSpec((CT, _EDGES), lambda i: (i, 0)),
                  pl.BlockSpec((CT, _EDGES), lambda i: (i, 0))],
        out_specs=pl.BlockSpec((CT, PEER_NKEYS, PEER_NKEYS), lambda i: (i, 0, 0)),
        out_shape=jax.ShapeDtypeStruct((T, PEER_NKEYS, PEER_NKEYS), jnp.bfloat16),
        compiler_params=pltpu.CompilerParams(dimension_semantics=("arbitrary",)),
        name="peer_coef",
    )(eidx, gates)


def _peer_dense_kernel(h_ref, c_ref, u_ref, v_ref, x1_ref, g2_ref, y_ref, acc_ref):
    j = pl.program_id(1)
    bf = jnp.bfloat16

    @pl.when(j == 0)
    def _():
        acc_ref[...] = jnp.zeros(acc_ref.shape, jnp.float32)

    h = h_ref[...]
    dt, de = h.shape[0], u_ref.shape[0]
    coef = c_ref[...].reshape(dt, de)
    for s in range(0, de, PEER_DS):
        a = lax.dot_general(h, u_ref[s:s + PEER_DS, :], (((1,), (1,)), ((), ())),
                            preferred_element_type=jnp.float32)
        act = 0.5 * a * (1.0 + lax.erf(a * (2.0 ** -0.5)))
        w = (coef[:, s:s + PEER_DS].astype(jnp.float32) * act).astype(bf)
        acc_ref[...] += jnp.dot(w, v_ref[s:s + PEER_DS, :], preferred_element_type=jnp.float32)

    @pl.when(j == pl.num_programs(1) - 1)
    def _():
        y_ref[...] = x1_ref[...] + g2_ref[0] * acc_ref[...]


def peer_experts(h2, coef, u, vtab, x1, g2, seq):
    T, D = h2.shape
    N = u.shape[0]
    DT, DE = PEER_DT, PEER_DE
    rows = DE // PEER_NKEYS
    assert T % DT == 0 and N % DE == 0 and seq % DT == 0 and N == PEER_NKEYS * PEER_NKEYS
    return pl.pallas_call(
        _peer_dense_kernel,
        grid=(T // DT, N // DE),
        in_specs=[
            pl.BlockSpec((DT, D), lambda i, j: (i, 0)),
            pl.BlockSpec((DT, rows, PEER_NKEYS), lambda i, j: (i, j, 0)),
            pl.BlockSpec((DE, D), lambda i, j: (j, 0)),
            pl.BlockSpec((DE, D), lambda i, j: (j, 0)),
            pl.BlockSpec((DT, D), lambda i, j: (i, 0)),
            pl.BlockSpec((1, 1, D), lambda i, j: (i * DT // seq, 0, 0)),
        ],
        out_specs=pl.BlockSpec((DT, D), lambda i, j: (i, 0)),
        out_shape=jax.ShapeDtypeStruct((T, D), jnp.float32),
        scratch_shapes=[pltpu.VMEM((DT, D), jnp.float32)],
        compiler_params=pltpu.CompilerParams(dimension_semantics=("arbitrary", "arbitrary"),
                                             vmem_limit_bytes=56 * 1024 * 1024),
        name="peer_dense",
    )(h2, coef, _to_bf16(u), _to_bf16(vtab), x1, g2.reshape(g2.shape[0], 1, D))


def kernel(x, c, w_ada, b_ada, norm1_g, norm2_g, w_in, hg_lb, hg_onorm_g, q_norm_g, k_norm_g,
           w_a, w_b, w_out, peer_wq, peer_subkeys, peer_u, peer_v):
    B, S, D = x.shape
    T = B * S
    for l in range(w_ada.shape[0]):
        mod = ada_modulation(c, w_ada[l], b_ada[l])
        sh1, sc1, g1, sh2, sc2, g2 = jnp.split(mod, 6, axis=-1)
        hg, q, k, iw, iq, ik, v, gate = in_projection(x, norm1_g[l], sc1, sh1, pack_w_in(w_in[l]))
        ya = hgrn2(hg, hg_lb, hg_onorm_g[l], l)
        yb = dsa_attention(q, iq, iw, k, v, ik, q_norm_g[l], k_norm_g[l])
        x1, h2, qry = out_projection(ya, yb, gate, x, g1, norm2_g[l], sc2, sh2,
                                     w_a[l], w_b[l], w_out[l], peer_wq[l])
        eidx, gates = peer_route(qry.reshape(T, -1), peer_subkeys[l])
        x = peer_experts(h2.reshape(T, D), peer_coefficients(eidx, gates), peer_u[l], peer_v[l],
                         x1.reshape(T, D), g2, S).reshape(B, S, D)
    return x
```

```python
import functools

import jax
import jax.numpy as jnp
import numpy as np
from jax import lax
from jax.experimental import pallas as pl
from jax.experimental.pallas import tpu as pltpu

D_MODEL = 1024
HG_HEADS = 4
HG_DK = 128
HG_WIDTH = HG_HEADS * HG_DK
HG_CHUNK = 64
ATT_HEADS = 8
ATT_GROUPS = 2
ATT_HEAD_DIM = 64
ATT_REP = ATT_HEADS // ATT_GROUPS
IDX_HEADS = 8
IDX_DIM = 64
IDX_TOPK_MAX = 256
PEER_HEADS = 8
PEER_NKEYS = 128
PEER_TOPK = 16
RMS_EPS = 1e-6
IN_SPLIT = (HG_WIDTH, HG_WIDTH, HG_WIDTH, HG_WIDTH,
            ATT_HEADS * ATT_HEAD_DIM, ATT_GROUPS * ATT_HEAD_DIM, ATT_GROUPS * ATT_HEAD_DIM,
            IDX_HEADS * IDX_DIM, IDX_DIM, IDX_HEADS,
            D_MODEL, D_MODEL)

_VMEM_LIMIT = 48 * 1024 * 1024


def _ada_kernel(c_ref, w_ref, b_ref, o_ref):
    c = c_ref[...]
    cs = c * (1.0 / (1.0 + jnp.exp(-c)))
    o_ref[...] = jnp.dot(cs, w_ref[...], preferred_element_type=jnp.float32,
                         precision=lax.Precision.HIGHEST) + b_ref[...]


def ada_modulation(c, w, b):
    B, D = c.shape
    N = w.shape[1]
    tn = 1536
    assert N % tn == 0
    return pl.pallas_call(
        _ada_kernel,
        grid=(N // tn,),
        in_specs=[pl.BlockSpec((B, D), lambda j: (0, 0)),
                  pl.BlockSpec((D, tn), lambda j: (0, j)),
                  pl.BlockSpec((1, tn), lambda j: (0, j))],
        out_specs=pl.BlockSpec((B, tn), lambda j: (0, j)),
        out_shape=jax.ShapeDtypeStruct((B, N), jnp.float32),
        compiler_params=pltpu.CompilerParams(dimension_semantics=("arbitrary",),
                                             vmem_limit_bytes=_VMEM_LIMIT),
        name="ada_modulation",
    )(c, w, b.reshape(1, N))


_PROJ_SEGS = (("hg", 2048, jnp.float32), ("q", 1024, jnp.float32), ("k", 256, jnp.float32),
              ("iw", 128, jnp.float32), ("iq", 1024, jnp.bfloat16), ("ik", 128, jnp.bfloat16),
              ("v", 128, jnp.bfloat16), ("gate", 2048, jnp.float32))
PROJ_TM = 256


def pack_w_in(w_in):
    D = w_in.shape[0]
    offs = np.cumsum((0,) + IN_SPLIT)
    hq, hf, hi, hg, aq, ak, av, iq, ik, iw, ga, gb = [w_in[:, offs[i]:offs[i + 1]] for i in range(12)]

    def pad_heads(w, nh):
        w = w.reshape(D, nh, 64)
        return jnp.concatenate([w, jnp.zeros_like(w)], axis=-1).reshape(D, nh * 128)

    iwp = jnp.concatenate([iw, jnp.zeros((D, 128 - IDX_HEADS), w_in.dtype)], axis=1)
    cols = [hq, hf, hi, hg, pad_heads(aq, ATT_HEADS), pad_heads(ak, ATT_GROUPS), iwp,
            pad_heads(iq, IDX_HEADS), pad_heads(ik, 1),
            av, ga, gb]
    return jnp.concatenate(cols, axis=1).astype(jnp.bfloat16)


def _proj_kernel(x_ref, g_ref, sc_ref, sh_ref, w_ref, *o_refs):
    x = x_ref[0]
    ms = jnp.mean(x * x, axis=-1, keepdims=True)
    h = x * lax.rsqrt(ms + RMS_EPS) * g_ref[...] * (1.0 + sc_ref[0]) + sh_ref[0]
    hb = h.astype(jnp.bfloat16)
    c0 = 0
    for (_, width, _), o_ref in zip(_PROJ_SEGS, o_refs):
        for s in range(0, width, 512):
            e = min(s + 512, width)
            o_ref[0, :, s:e] = jnp.dot(hb, w_ref[:, c0 + s:c0 + e],
                                       preferred_element_type=jnp.float32).astype(o_ref.dtype)
        c0 += width


def in_projection(x, norm_g, sc, sh, w_packed):
    B, S, D = x.shape
    tm = PROJ_TM
    assert S % tm == 0
    ntot = sum(w for _, w, _ in _PROJ_SEGS)
    assert w_packed.shape == (D, ntot)
    return pl.pallas_call(
        _proj_kernel,
        grid=(B, S // tm),
        in_specs=[
            pl.BlockSpec((1, tm, D), lambda b, i: (b, i, 0)),
            pl.BlockSpec((1, D), lambda b, i: (0, 0)),
            pl.BlockSpec((1, 1, D), lambda b, i: (b, 0, 0)),
            pl.BlockSpec((1, 1, D), lambda b, i: (b, 0, 0)),
            pl.BlockSpec((D, ntot), lambda b, i: (0, 0), pipeline_mode=pl.Buffered(1)),
        ],
        out_specs=[pl.BlockSpec((1, tm, w), lambda b, i: (b, i, 0)) for _, w, _ in _PROJ_SEGS],
        out_shape=[jax.ShapeDtypeStruct((B, S, w), dt) for _, w, dt in _PROJ_SEGS],
        compiler_params=pltpu.CompilerParams(dimension_semantics=("arbitrary", "arbitrary"),
                                             vmem_limit_bytes=_VMEM_LIMIT),
        name="in_projection",
    )(x, norm_g.reshape(1, D), sc.reshape(B, 1, D), sh.reshape(B, 1, D), w_packed)


HG_LT = 512


def _split3_bf16(x):
    a = x.astype(jnp.bfloat16)
    r = x - a.astype(jnp.float32)
    b = r.astype(jnp.bfloat16)
    c = (r - b.astype(jnp.float32)).astype(jnp.bfloat16)
    return a, b, c


def _hgrn_kernel(q_ref, f_ref, i_ref, g_ref, lb_ref, og_ref, o_ref, st_ref, *, layer):
    C = HG_CHUNK
    bf = jnp.bfloat16

    @pl.when(pl.program_id(1) == 0)
    def _():
        st_ref[...] = jnp.zeros(st_ref.shape, jnp.float32)

    lbr = lb_ref[...]
    e = jnp.exp(lbr - jnp.max(lbr, axis=0, keepdims=True))
    sm = e / jnp.sum(e, axis=0, keepdims=True)
    lb_all = jnp.sum(sm[0:layer + 1], axis=0, keepdims=True)

    r_i = lax.broadcasted_iota(jnp.int32, (C, C), 0)
    c_i = lax.broadcasted_iota(jnp.int32, (C, C), 1)
    causal = c_i <= r_i
    tri = jnp.where(causal, 1.0, 0.0).astype(bf)
    og = og_ref[...]

    def chunk(c, carry):
        r0 = pl.multiple_of(c * C, C)
        H = range(HG_HEADS)
        cols = [slice(h * HG_DK, (h + 1) * HG_DK) for h in H]
        dot = functools.partial(jnp.dot, preferred_element_type=jnp.float32)
        nt = (((1,), (1,)), ((), ()))
        tn = (((0,), (0,)), ((), ()))
        qr = [q_ref[0, pl.ds(r0, C), cs] for cs in cols]
        q = [x * (1.0 / (1.0 + jnp.exp(-x))) for x in qr]
        f = [lb_all[:, cs] + (1.0 - lb_all[:, cs]) * (1.0 / (1.0 + jnp.exp(-f_ref[0, pl.ds(r0, C), cs])))
             for cs in cols]
        k = [1.0 - x for x in f]
        vb = [i_ref[0, pl.ds(r0, C), cs].astype(bf) for cs in cols]
        lg = [_split3_bf16(jnp.log(x)) for x in f]
        b = [dot(tri, l1) + dot(tri, l2) + dot(tri, l3) for l1, l2, l3 in lg]
        bm = [x[C // 2 - 1:C // 2, :] for x in b]
        bl = [x[C - 1:C, :] for x in b]
        a = [lax.dot_general((q[h] * jnp.exp(b[h] - bm[h])).astype(bf), (k[h] * jnp.exp(bm[h] - b[h])).astype(bf),
                             nt, preferred_element_type=jnp.float32) for h in H]
        st = [st_ref[h] for h in H]
        o_inter = [lax.dot_general((q[h] * jnp.exp(b[h])).astype(bf), st[h].astype(bf), nt,
                                   preferred_element_type=jnp.float32) for h in H]
        ds = [lax.dot_general(vb[h], (k[h] * jnp.exp(bl[h] - b[h])).astype(bf), tn,
                              preferred_element_type=jnp.float32) for h in H]
        for h in H:
            st_ref[h] = st[h] * jnp.exp(bl[h]) + ds[h]
        o = [dot(jnp.where(causal, a[h], 0.0).astype(bf), vb[h]) + o_inter[h] for h in H]
        for h in H:
            on = o[h] * lax.rsqrt(jnp.mean(o[h] * o[h], axis=-1, keepdims=True) + RMS_EPS) * og
            gr = g_ref[0, pl.ds(r0, C), cols[h]]
            o_ref[0, pl.ds(r0, C), cols[h]] = (on * (gr * (1.0 / (1.0 + jnp.exp(-gr))))).astype(o_ref.dtype)
        return carry
    lax.fori_loop(0, q_ref.shape[1] // C, chunk, 0)


def hgrn2(hg, hg_lb, onorm_g, layer):
    B, S, _ = hg.shape
    LT = HG_LT
    assert S % LT == 0

    def piece(p):
        return pl.BlockSpec((1, LT, HG_WIDTH), lambda b, i, p=p: (b, i, p))
    return pl.pallas_call(
        functools.partial(_hgrn_kernel, layer=layer),
        grid=(B, S // LT),
        in_specs=[piece(0), piece(1), piece(2), piece(3),
                  pl.BlockSpec(hg_lb.shape, lambda b, i: (0, 0)),
                  pl.BlockSpec((1, HG_DK), lambda b, i: (0, 0))],
        out_specs=pl.BlockSpec((1, LT, HG_WIDTH), lambda b, i: (b, i, 0)),
        out_shape=jax.ShapeDtypeStruct((B, S, HG_WIDTH), jnp.bfloat16),
        scratch_shapes=[pltpu.VMEM((HG_HEADS, HG_DK, HG_DK), jnp.float32)],
        compiler_params=pltpu.CompilerParams(dimension_semantics=("arbitrary", "arbitrary"),
                                             vmem_limit_bytes=_VMEM_LIMIT),
        name="hgrn2",
    )(hg, hg, hg, hg, hg_lb, onorm_g.reshape(1, HG_DK))


OUT_TM = 256


def _out_kernel(ya_ref, yb_ref, ga_ref, gb_ref, x_ref, g1_ref, n2_ref, sc_ref, sh_ref,
                wa_ref, wb_ref, wo_ref, wq_ref, x1_ref, h2_ref, qry_ref):
    bf = jnp.bfloat16

    def sig(z):
        return 1.0 / (1.0 + jnp.exp(-z))
    ma = jnp.dot(ya_ref[0], wa_ref[...], preferred_element_type=jnp.float32)
    mb = jnp.dot(yb_ref[0], wb_ref[...], preferred_element_type=jnp.float32)
    merged = sig(ga_ref[0]) * ma + sig(gb_ref[0]) * mb
    x1 = x_ref[0] + g1_ref[0] * jnp.dot(merged.astype(bf), wo_ref[...], preferred_element_type=jnp.float32)
    x1_ref[0] = x1
    ms = jnp.mean(x1 * x1, axis=-1, keepdims=True)
    h2 = (x1 * lax.rsqrt(ms + RMS_EPS) * n2_ref[...] * (1.0 + sc_ref[0]) + sh_ref[0]).astype(bf)
    h2_ref[0] = h2
    qry_ref[0] = jnp.dot(h2, wq_ref[...], preferred_element_type=jnp.float32)


def out_projection(ya, yb, gate, x, g1, norm2_g, sc2, sh2, w_a, w_b, w_out, wq):
    B, S, D = x.shape
    tm = OUT_TM
    bf = jnp.bfloat16
    NQ = wq.shape[1]

    def tok(width, col=0):
        return pl.BlockSpec((1, tm, width), lambda b, i, col=col: (b, i, col))

    def per_batch():
        return pl.BlockSpec((1, 1, D), lambda b, i: (b, 0, 0))

    def const(shape):
        return pl.BlockSpec(shape, lambda b, i: (0,) * len(shape), pipeline_mode=pl.Buffered(1))
    return pl.pallas_call(
        _out_kernel,
        grid=(B, S // tm),
        in_specs=[tok(HG_WIDTH), tok(512), tok(D, 0), tok(D, 1), tok(D), per_batch(),
                  const((1, D)), per_batch(), per_batch(),
                  const(w_a.shape), const(w_b.shape), const(w_out.shape), const(wq.shape)],
        out_specs=[tok(D), tok(D), tok(NQ)],
        out_shape=[jax.ShapeDtypeStruct((B, S, D), jnp.float32),
                   jax.ShapeDtypeStruct((B, S, D), jnp.bfloat16),
                   jax.ShapeDtypeStruct((B, S, NQ), jnp.float32)],
        compiler_params=pltpu.CompilerParams(dimension_semantics=("arbitrary", "arbitrary"),
                                             vmem_limit_bytes=_VMEM_LIMIT),
        name="out_projection",
    )(ya, yb, gate, gate, x, g1.reshape(B, 1, D), norm2_g.reshape(1, D), sc2.reshape(B, 1, D),
      sh2.reshape(B, 1, D), w_a.astype(bf), w_b.astype(bf), w_out.astype(bf), wq.astype(bf))


DSA_TQ = 128
DSA_KC = 512
DSA_KA = 256
_INT_MIN = -2 ** 31
_FILL_KEY = -2139095041
_NEG_BIG = -1e30


def _sortable_key(x):
    b = lax.bitcast_convert_type(x, jnp.int32)
    return b ^ ((b >> 31) & jnp.int32(0x7FFFFFFF))


def _dsa_kernel(q_ref, iq_ref, iw_ref, k_ref, v_ref, ik_ref, qg_ref, kg_ref, o_ref,
                kn_ref, vt_ref, qs_ref, keys_ref, lim_ref, m_ref, l_ref, acc_ref, sc_ref, *, seq, topk):
    TQ, KC, KA = DSA_TQ, DSA_KC, DSA_KA
    bf = jnp.bfloat16
    j = pl.program_id(1)
    n_kc = (j * TQ + TQ + KC - 1) // KC
    idx_scale = (IDX_DIM ** -0.5) * (IDX_HEADS ** -0.5)
    att_scale = ATT_HEAD_DIM ** -0.5
    nt = (((1,), (1,)), ((), ()))

    lane = lax.broadcasted_iota(jnp.int32, (1, 128), 1)
    krow = lax.broadcasted_iota(jnp.int32, (KC, 1), 0)

    def kfeat(kpos):
        a = (kpos // 64).astype(jnp.float32)
        b = (kpos % 64).astype(jnp.float32)
        return jnp.where(lane == ATT_HEAD_DIM, a, jnp.where(lane == ATT_HEAD_DIM + 1, b, 0.0))

    def qfeat(slope):
        return jnp.where(lane == ATT_HEAD_DIM, 64.0 * slope, jnp.where(lane == ATT_HEAD_DIM + 1, slope, 0.0))

    @pl.when(j == 0)
    def _():
        def body(c, carry):
            r0 = pl.multiple_of(c * KC, KC)
            for g in range(ATT_GROUPS):
                kk = k_ref[0, pl.ds(r0, KC), g * 128:(g + 1) * 128]
                ms = jnp.sum(kk * kk, axis=-1, keepdims=True) * (1.0 / ATT_HEAD_DIM)
                kn = kk * lax.rsqrt(ms + RMS_EPS) * kg_ref[...]
                kn_ref[g, pl.ds(r0, KC), :] = (kn + kfeat(r0 + krow)).astype(bf)
            vt_ref[c] = v_ref[0, pl.ds(r0, KC), :].astype(jnp.float32).T.astype(bf)
            return carry
        lax.fori_loop(0, seq // KC, body, 0)

    for h in range(ATT_HEADS):
        qq = q_ref[0, :, h * 128:(h + 1) * 128]
        ms = jnp.sum(qq * qq, axis=-1, keepdims=True) * (1.0 / ATT_HEAD_DIM)
        slope = float(2.0 ** (-8.0 * (h + 1) / ATT_HEADS))
        qs_ref[h] = (qq * lax.rsqrt(ms + RMS_EPS) * qg_ref[...] * att_scale + qfeat(slope)).astype(bf)
    iwt = iw_ref[0].T

    qpos = j * TQ + lax.broadcasted_iota(jnp.int32, (1, TQ), 1)

    def idx_body(c, carry):
        r0 = pl.multiple_of(c * KC, KC)
        ikc = ik_ref[0, pl.ds(r0, KC), :]
        acc = jnp.zeros((KC, TQ), jnp.float32)
        for h in range(IDX_HEADS):
            s = lax.dot_general(ikc, iq_ref[0, :, h * 128:(h + 1) * 128], nt,
                                preferred_element_type=jnp.float32)
            acc = acc + iwt[h:h + 1, :] * jnp.maximum(s, 0.0)
        score = acc * idx_scale
        score = jnp.where(score == 0.0, 0.0, score)
        score = jnp.where(c * KC + krow <= qpos, score, -jnp.inf)
        keys_ref[c] = _sortable_key(score)
        return carry
    lax.fori_loop(0, n_kc, idx_body, 0)

    def row_fold(m):
        parts = [m[8 * i:8 * i + 8] for i in range(m.shape[0] // 8)]
        while len(parts) > 1:
            parts = [parts[i] + parts[i + 1] for i in range(0, len(parts), 2)]
        return parts[0]

    def count(pred):
        def body(c, acc):
            return acc + row_fold(jnp.where(pred(c, keys_ref[c]), 1.0, 0.0))
        acc = lax.fori_loop(0, n_kc, body, jnp.zeros((8, TQ), jnp.float32))
        return jnp.sum(acc, axis=0, keepdims=True)

    def count_ge(cand):
        return count(lambda c, key: key >= cand)

    kf = float(topk)
    ans0 = jnp.where(count_ge(jnp.zeros((1, TQ), jnp.int32)) >= kf, jnp.int32(0), jnp.int32(_INT_MIN))

    def bit_body(i, ans):
        cand = ans | jnp.left_shift(jnp.int32(1), 30 - i)
        return jnp.where(count_ge(cand) >= kf, cand, ans)
    thr = lax.fori_loop(0, 31, bit_body, ans0)

    n_gt = count_ge(thr + 1)
    n_ge = count_ge(thr)
    need = kf - n_gt
    tie = jnp.logical_and(n_ge - n_gt > need, thr != _FILL_KEY)
    lim_ref[...] = jnp.full((1, TQ), seq, jnp.int32)

    @pl.when(jnp.max(jnp.where(tie, 1.0, 0.0)) > 0.0)
    def _():
        nbits = max(1, int(np.ceil(np.log2(seq))))

        def bit2(i, lo):
            cand = lo | jnp.left_shift(jnp.int32(1), nbits - 1 - i)
            n_before = count(lambda c, key: jnp.logical_and(key == thr, c * KC + krow < cand))
            return jnp.where(n_before < need, cand, lo)
        lo = lax.fori_loop(0, nbits, bit2, jnp.zeros((1, TQ), jnp.int32))
        lim_ref[...] = jnp.where(tie, lo, jnp.int32(seq))

    lim = lim_ref[...]

    m_ref[...] = jnp.full(m_ref.shape, _NEG_BIG, jnp.float32)
    l_ref[...] = jnp.zeros(l_ref.shape, jnp.float32)
    acc_ref[...] = jnp.zeros(acc_ref.shape, jnp.float32)
    hd = ATT_HEAD_DIM

    def att_body(c, carry):
        vtc = vt_ref[c]
        for half in range(KC // KA):
            h0 = half * KA
            r0 = pl.multiple_of(c * KC, KC) + h0
            kpos = c * KC + h0 + krow[0:KA]
            key = keys_ref[c, h0:h0 + KA, :]
            sel = jnp.logical_or(key > thr, jnp.logical_and(key == thr, kpos <= lim))
            sel = jnp.logical_and(sel, kpos <= qpos)
            madd = jnp.where(sel, 0.0, _NEG_BIG)
            mx = []
            for h in range(ATT_HEADS):
                kc = kn_ref[h // ATT_REP, pl.ds(r0, KA), :]
                sc = lax.dot_general(kc, qs_ref[h], nt, preferred_element_type=jnp.float32) + madd
                sc_ref[h] = sc
                mx.append(jnp.max(sc, axis=0, keepdims=True))
            for h in range(ATT_HEADS):
                g = h // ATT_REP
                m_old = m_ref[h]
                m_new = jnp.maximum(m_old, mx[h])
                p = jnp.exp(sc_ref[h] - m_new)
                alpha = jnp.exp(m_old - m_new)
                l_ref[h] = alpha * l_ref[h] + jnp.sum(p, axis=0, keepdims=True)
                m_ref[h] = m_new
                pv = jnp.dot(vtc[g * hd:(g + 1) * hd, h0:h0 + KA], p.astype(bf),
                             preferred_element_type=jnp.float32)
                rs = slice((h % 2) * hd, (h % 2 + 1) * hd)
                acc_ref[h // 2, rs, :] = acc_ref[h // 2, rs, :] * alpha + pv
        return carry
    lax.fori_loop(0, n_kc, att_body, 0)

    for pair in range(ATT_HEADS // 2):
        den = jnp.concatenate([jnp.broadcast_to(l_ref[2 * pair], (hd, TQ)),
                               jnp.broadcast_to(l_ref[2 * pair + 1], (hd, TQ))], axis=0)
        o_ref[0, :, pair * 128:(pair + 1) * 128] = (acc_ref[pair] / den).T.astype(o_ref.dtype)


def dsa_attention(q, iq, iw, k, v, ik, qg, kg):
    B, S, _ = q.shape
    TQ, KC = DSA_TQ, DSA_KC
    assert S % KC == 0 and S % TQ == 0
    topk = min(IDX_TOPK_MAX, S // 4)
    kern = functools.partial(_dsa_kernel, seq=S, topk=topk)

    def pad_gain(g):
        return jnp.concatenate([g, jnp.zeros_like(g)]).reshape(1, 128)
    return pl.pallas_call(
        kern,
        grid=(B, S // TQ),
        in_specs=[
            pl.BlockSpec((1, TQ, ATT_HEADS * 128), lambda b, j: (b, j, 0)),
            pl.BlockSpec((1, TQ, IDX_HEADS * 128), lambda b, j: (b, j, 0)),
            pl.BlockSpec((1, TQ, 128), lambda b, j: (b, j, 0)),
            pl.BlockSpec((1, S, ATT_GROUPS * 128), lambda b, j: (b, 0, 0)),
            pl.BlockSpec((1, S, ATT_GROUPS * ATT_HEAD_DIM), lambda b, j: (b, 0, 0)),
            pl.BlockSpec((1, S, 128), lambda b, j: (b, 0, 0)),
            pl.BlockSpec((1, 128), lambda b, j: (0, 0)),
            pl.BlockSpec((1, 128), lambda b, j: (0, 0)),
        ],
        out_specs=pl.BlockSpec((1, TQ, ATT_HEADS * ATT_HEAD_DIM), lambda b, j: (b, j, 0)),
        out_shape=jax.ShapeDtypeStruct((B, S, ATT_HEADS * ATT_HEAD_DIM), jnp.bfloat16),
        scratch_shapes=[
            pltpu.VMEM((ATT_GROUPS, S, 128), jnp.bfloat16),
            pltpu.VMEM((S // KC, ATT_GROUPS * ATT_HEAD_DIM, KC), jnp.bfloat16),
            pltpu.VMEM((ATT_HEADS, TQ, 128), jnp.bfloat16),
            pltpu.VMEM((S // KC, KC, TQ), jnp.int32),
            pltpu.VMEM((1, TQ), jnp.int32),
            pltpu.VMEM((ATT_HEADS, 1, TQ), jnp.float32),
            pltpu.VMEM((ATT_HEADS, 1, TQ), jnp.float32),
            pltpu.VMEM((ATT_HEADS // 2, 2 * ATT_HEAD_DIM, TQ), jnp.float32),
            pltpu.VMEM((ATT_HEADS, DSA_KA, TQ), jnp.float32),
        ],
        compiler_params=pltpu.CompilerParams(
            dimension_semantics=("arbitrary", "arbitrary"),
            vmem_limit_bytes=_VMEM_LIMIT),
        name="dsa_attention",
    )(q, iq, iw, k, v, ik, pad_gain(qg), pad_gain(kg))


PEER_RT = 256
_EDGES = PEER_HEADS * PEER_TOPK


def _extract_topk_rows(work, n_out, val_ref, pay_ref, order=None, payload=None):
    if order is None:
        order = lax.broadcasted_iota(jnp.int32, work.shape, 0)
    big = jnp.int32(2 ** 30)
    for it in range(n_out):
        m = jnp.max(work, axis=0, keepdims=True)
        am = jnp.min(jnp.where(work == m, order, big), axis=0, keepdims=True)
        hit = order == am
        val_ref[it:it + 1, :] = m
        if payload is None:
            pay_ref[it:it + 1, :] = am
        else:
            pay_ref[it:it + 1, :] = jnp.max(jnp.where(hit, payload, -1), axis=0, keepdims=True)
        work = jnp.where(hit, -jnp.inf, work)


def _peer_route_kernel(q_ref, sk_ref, e_ref, g_ref, s1_ref, i1_ref, s2_ref, i2_ref, ts_ref, te_ref, et_ref,
                       gt_ref):
    K = PEER_TOPK
    RT = q_ref.shape[0]
    b8 = lax.broadcasted_iota(jnp.int32, (8, RT), 0)
    b16 = lax.broadcasted_iota(jnp.int32, (K, RT), 0)
    order = jnp.concatenate([b16] + [a * K + b8 for a in range(1, 8)] + [(8 + b8) * K], axis=0)
    for h in range(PEER_HEADS):
        for p, (s_ref, i_ref) in enumerate(((s1_ref, i1_ref), (s2_ref, i2_ref))):
            col = (h * 2 + p) * 128
            qs = q_ref[:, col:col + 128].astype(jnp.bfloat16)
            sk = sk_ref[h, p].astype(jnp.bfloat16)
            sc = lax.dot_general(sk, qs, (((1,), (1,)), ((), ())),
                                 preferred_element_type=jnp.float32)
            _extract_topk_rows(sc, K, s_ref, i_ref)
        s2 = s2_ref[...]
        i2 = i2_ref[...]
        cand = [s1_ref[0:1, :] + s2]
        cidx = [i1_ref[0:1, :] * PEER_NKEYS + i2]
        for a in range(1, 8):
            cand.append(jnp.where(b8 < K // (a + 1), s1_ref[a:a + 1, :] + s2[0:8], -jnp.inf))
            cidx.append(i1_ref[a:a + 1, :] * PEER_NKEYS + i2[0:8])
        cand.append(s1_ref[8:K, :] + s2[0:1])
        cidx.append(i1_ref[8:K, :] * PEER_NKEYS + i2[0:1])
        _extract_topk_rows(jnp.concatenate(cand, axis=0), K, ts_ref, te_ref, order, jnp.concatenate(cidx, axis=0))
        ts = ts_ref[...]
        ex = jnp.exp(ts - jnp.max(ts, axis=0, keepdims=True))
        gt_ref[h * K:(h + 1) * K, :] = ex / jnp.sum(ex, axis=0, keepdims=True)
        et_ref[h * K:(h + 1) * K, :] = te_ref[...]
    e_ref[...] = et_ref[...].T
    g_ref[...] = gt_ref[...].T


def peer_route(qry, subkeys):
    T = qry.shape[0]
    RT = PEER_RT
    assert T % RT == 0
    K = PEER_TOPK
    return pl.pallas_call(
        _peer_route_kernel,
        grid=(T // RT,),
        in_specs=[
            pl.BlockSpec((RT, qry.shape[1]), lambda i: (i, 0)),
            pl.BlockSpec(subkeys.shape, lambda i: (0, 0, 0, 0)),
        ],
        out_specs=[pl.BlockSpec((RT, _EDGES), lambda i: (i, 0)),
                   pl.BlockSpec((RT, _EDGES), lambda i: (i, 0))],
        out_shape=[jax.ShapeDtypeStruct((T, _EDGES), jnp.int32),
                   jax.ShapeDtypeStruct((T, _EDGES), jnp.float32)],
        scratch_shapes=[
            pltpu.VMEM((K, RT), jnp.float32), pltpu.VMEM((K, RT), jnp.int32),
            pltpu.VMEM((K, RT), jnp.float32), pltpu.VMEM((K, RT), jnp.int32),
            pltpu.VMEM((K, RT), jnp.float32), pltpu.VMEM((K, RT), jnp.int32),
            pltpu.VMEM((_EDGES, RT), jnp.int32),
            pltpu.VMEM((_EDGES, RT), jnp.float32),
        ],
        compiler_params=pltpu.CompilerParams(dimension_semantics=("arbitrary",)),
        name="peer_route",
    )(qry, subkeys)


PEER_CT = 128
PEER_DT = 1024
PEER_DE = 1024
PEER_DS = 1024
_COEF_UNROLL = 32


def _cast_kernel(t_ref, o_ref):
    o_ref[...] = t_ref[...].astype(o_ref.dtype)


def _to_bf16(tab):
    n, d = tab.shape
    tr = 512
    assert n % tr == 0
    return pl.pallas_call(
        _cast_kernel,
        grid=(n // tr,),
        in_specs=[pl.BlockSpec((tr, d), lambda i: (i, 0))],
        out_specs=pl.BlockSpec((tr, d), lambda i: (i, 0)),
        out_shape=jax.ShapeDtypeStruct((n, d), jnp.bfloat16),
        compiler_params=pltpu.CompilerParams(dimension_semantics=("arbitrary",)),
        name="table_to_bf16",
    )(tab)


def _peer_coef_kernel(e_ref, g_ref, c_ref):
    nk = PEER_NKEYS
    rows = lax.broadcasted_iota(jnp.int32, (nk, _EDGES), 0)
    shift = nk.bit_length() - 1

    def body(tt, carry):
        ts = [tt * _COEF_UNROLL + u for u in range(_COEF_UNROLL)]
        es = [e_ref[pl.ds(t, 1), :] for t in ts]
        a = [jnp.where(rows == (e >> shift), g_ref[pl.ds(t, 1), :], 0.0).astype(jnp.bfloat16)
             for t, e in zip(ts, es)]
        b = [jnp.where(rows == (e & (nk - 1)), 1.0, 0.0).astype(jnp.bfloat16) for e in es]
        c = [lax.dot_general(x, y, (((1,), (1,)), ((), ())), preferred_element_type=jnp.float32)
             for x, y in zip(a, b)]
        for t, x in zip(ts, c):
            c_ref[t] = x.astype(c_ref.dtype)
        return carry
    lax.fori_loop(0, e_ref.shape[0] // _COEF_UNROLL, body, 0)


def peer_coefficients(eidx, gates):
    T = eidx.shape[0]
    CT = PEER_CT
    assert T % CT == 0 and PEER_NKEYS & (PEER_NKEYS - 1) == 0
    return pl.pallas_call(
        _peer_coef_kernel,
        grid=(T // CT,),
        in_specs=[pl.BlockSpec((CT, _EDGES), lambda i: (i, 0)),
                  pl.BlockSpec((CT, _EDGES), lambda i: (i, 0))],
        out_specs=pl.BlockSpec((CT, PEER_NKEYS, PEER_NKEYS), lambda i: (i, 0, 0)),
        out_shape=jax.ShapeDtypeStruct((T, PEER_NKEYS, PEER_NKEYS), jnp.bfloat16),
        compiler_params=pltpu.CompilerParams(dimension_semantics=("arbitrary",)),
        name="peer_coef",
    )(eidx, gates)


def _peer_dense_kernel(h_ref, c_ref, u_ref, v_ref, x1_ref, g2_ref, y_ref, acc_ref):
    j = pl.program_id(1)
    bf = jnp.bfloat16

    @pl.when(j == 0)
    def _():
        acc_ref[...] = jnp.zeros(acc_ref.shape, jnp.float32)

    h = h_ref[...]
    dt, de = h.shape[0], u_ref.shape[0]
    coef = c_ref[...].reshape(dt, de)
    for s in range(0, de, PEER_DS):
        a = lax.dot_general(h, u_ref[s:s + PEER_DS, :], (((1,), (1,)), ((), ())),
                            preferred_element_type=jnp.float32)
        act = 0.5 * a * (1.0 + lax.erf(a * (2.0 ** -0.5)))
        w = (coef[:, s:s + PEER_DS].astype(jnp.float32) * act).astype(bf)
        acc_ref[...] += jnp.dot(w, v_ref[s:s + PEER_DS, :], preferred_element_type=jnp.float32)

    @pl.when(j == pl.num_programs(1) - 1)
    def _():
        y_ref[...] = x1_ref[...] + g2_ref[0] * acc_ref[...]


def peer_experts(h2, coef, u, vtab, x1, g2, seq):
    T, D = h2.shape
    N = u.shape[0]
    DT, DE = PEER_DT, PEER_DE
    rows = DE // PEER_NKEYS
    assert T % DT == 0 and N % DE == 0 and seq % DT == 0 and N == PEER_NKEYS * PEER_NKEYS
    return pl.pallas_call(
        _peer_dense_kernel,
        grid=(T // DT, N // DE),
        in_specs=[
            pl.BlockSpec((DT, D), lambda i, j: (i, 0)),
            pl.BlockSpec((DT, rows, PEER_NKEYS), lambda i, j: (i, j, 0)),
            pl.BlockSpec((DE, D), lambda i, j: (j, 0)),
            pl.BlockSpec((DE, D), lambda i, j: (j, 0)),
            pl.BlockSpec((DT, D), lambda i, j: (i, 0)),
            pl.BlockSpec((1, 1, D), lambda i, j: (i * DT // seq, 0, 0)),
        ],
        out_specs=pl.BlockSpec((DT, D), lambda i, j: (i, 0)),
        out_shape=jax.ShapeDtypeStruct((T, D), jnp.float32),
        scratch_shapes=[pltpu.VMEM((DT, D), jnp.float32)],
        compiler_params=pltpu.CompilerParams(dimension_semantics=("arbitrary", "arbitrary"),
                                             vmem_limit_bytes=56 * 1024 * 1024),
        name="peer_dense",
    )(h2, coef, _to_bf16(u), _to_bf16(vtab), x1, g2.reshape(g2.shape[0], 1, D))


def kernel(x, c, w_ada, b_ada, norm1_g, norm2_g, w_in, hg_lb, hg_onorm_g, q_norm_g, k_norm_g,
           w_a, w_b, w_out, peer_wq, peer_subkeys, peer_u, peer_v):
    B, S, D = x.shape
    T = B * S
    for l in range(w_ada.shape[0]):
        mod = ada_modulation(c, w_ada[l], b_ada[l])
        sh1, sc1, g1, sh2, sc2, g2 = jnp.split(mod, 6, axis=-1)
        hg, q, k, iw, iq, ik, v, gate = in_projection(x, norm1_g[l], sc1, sh1, pack_w_in(w_in[l]))
        ya = hgrn2(hg, hg_lb, hg_onorm_g[l], l)
        yb = dsa_attention(q, iq, iw, k, v, ik, q_norm_g[l], k_norm_g[l])
        x1, h2, qry = out_projection(ya, yb, gate, x, g1, norm2_g[l], sc2, sh2,
                                     w_a[l], w_b[l], w_out[l], peer_wq[l])
        eidx, gates = peer_route(qry.reshape(T, -1), peer_subkeys[l])
        x = peer_experts(h2.reshape(T, D), peer_coefficients(eidx, gates), peer_u[l], peer_v[l],
                         x1.reshape(T, D), g2, S).reshape(B, S, D)
    return x
```

```python
import functools

import jax
import jax.numpy as jnp
import numpy as np
from jax import lax
from jax.experimental import pallas as pl
from jax.experimental.pallas import tpu as pltpu

D_MODEL = 1024
HG_HEADS = 4
HG_DK = 128
HG_WIDTH = HG_HEADS * HG_DK
HG_CHUNK = 64
ATT_HEADS = 8
ATT_GROUPS = 2
ATT_HEAD_DIM = 64
ATT_REP = ATT_HEADS // ATT_GROUPS
IDX_HEADS = 8
IDX_DIM = 64
IDX_TOPK_MAX = 256
PEER_HEADS = 8
PEER_NKEYS = 128
PEER_TOPK = 16
RMS_EPS = 1e-6
IN_SPLIT = (HG_WIDTH, HG_WIDTH, HG_WIDTH, HG_WIDTH,
            ATT_HEADS * ATT_HEAD_DIM, ATT_GROUPS * ATT_HEAD_DIM, ATT_GROUPS * ATT_HEAD_DIM,
            IDX_HEADS * IDX_DIM, IDX_DIM, IDX_HEADS,
            D_MODEL, D_MODEL)

_VMEM_LIMIT = 48 * 1024 * 1024


def _ada_kernel(c_ref, w_ref, b_ref, o_ref):
    c = c_ref[...]
    cs = c * (1.0 / (1.0 + jnp.exp(-c)))
    o_ref[...] = jnp.dot(cs, w_ref[...], preferred_element_type=jnp.float32,
                         precision=lax.Precision.HIGHEST) + b_ref[...]


def ada_modulation(c, w, b):
    B, D = c.shape
    N = w.shape[1]
    tn = 1536
    assert N % tn == 0
    return pl.pallas_call(
        _ada_kernel,
        grid=(N // tn,),
        in_specs=[pl.BlockSpec((B, D), lambda j: (0, 0)),
                  pl.BlockSpec((D, tn), lambda j: (0, j)),
                  pl.BlockSpec((1, tn), lambda j: (0, j))],
        out_specs=pl.BlockSpec((B, tn), lambda j: (0, j)),
        out_shape=jax.ShapeDtypeStruct((B, N), jnp.float32),
        compiler_params=pltpu.CompilerParams(dimension_semantics=("arbitrary",),
                                             vmem_limit_bytes=_VMEM_LIMIT),
        name="ada_modulation",
    )(c, w, b.reshape(1, N))


_PROJ_SEGS = (("hg", 2048, jnp.float32), ("q", 1024, jnp.float32), ("k", 256, jnp.float32),
              ("iw", 128, jnp.float32), ("iq", 1024, jnp.bfloat16), ("ik", 128, jnp.bfloat16),
              ("v", 128, jnp.bfloat16), ("gate", 2048, jnp.float32))
PROJ_TM = 512


def pack_w_in(w_in):
    D = w_in.shape[0]
    offs = np.cumsum((0,) + IN_SPLIT)
    hq, hf, hi, hg, aq, ak, av, iq, ik, iw, ga, gb = [w_in[:, offs[i]:offs[i + 1]] for i in range(12)]

    def pad_heads(w, nh):
        w = w.reshape(D, nh, 64)
        return jnp.concatenate([w, jnp.zeros_like(w)], axis=-1).reshape(D, nh * 128)

    iwp = jnp.concatenate([iw, jnp.zeros((D, 128 - IDX_HEADS), w_in.dtype)], axis=1)
    cols = [hq, hf, hi, hg, pad_heads(aq, ATT_HEADS), pad_heads(ak, ATT_GROUPS), iwp,
            pad_heads(iq, IDX_HEADS), pad_heads(ik, 1),
            av, ga, gb]
    return jnp.concatenate(cols, axis=1).astype(jnp.bfloat16)


def _proj_kernel(x_ref, g_ref, sc_ref, sh_ref, w_ref, *o_refs):
    x = x_ref[0]
    ms = jnp.mean(x * x, axis=-1, keepdims=True)
    h = x * lax.rsqrt(ms + RMS_EPS) * g_ref[...] * (1.0 + sc_ref[0]) + sh_ref[0]
    hb = h.astype(jnp.bfloat16)
    c0 = 0
    for (_, width, _), o_ref in zip(_PROJ_SEGS, o_refs):
        for s in range(0, width, 512):
            e = min(s + 512, width)
            o_ref[0, :, s:e] = jnp.dot(hb, w_ref[:, c0 + s:c0 + e],
                                       preferred_element_type=jnp.float32).astype(o_ref.dtype)
        c0 += width


def in_projection(x, norm_g, sc, sh, w_packed):
    B, S, D = x.shape
    tm = PROJ_TM
    assert S % tm == 0
    ntot = sum(w for _, w, _ in _PROJ_SEGS)
    assert w_packed.shape == (D, ntot)
    return pl.pallas_call(
        _proj_kernel,
        grid=(B, S // tm),
        in_specs=[
            pl.BlockSpec((1, tm, D), lambda b, i: (b, i, 0)),
            pl.BlockSpec((1, D), lambda b, i: (0, 0)),
            pl.BlockSpec((1, 1, D), lambda b, i: (b, 0, 0)),
            pl.BlockSpec((1, 1, D), lambda b, i: (b, 0, 0)),
            pl.BlockSpec((D, ntot), lambda b, i: (0, 0), pipeline_mode=pl.Buffered(1)),
        ],
        out_specs=[pl.BlockSpec((1, tm, w), lambda b, i: (b, i, 0)) for _, w, _ in _PROJ_SEGS],
        out_shape=[jax.ShapeDtypeStruct((B, S, w), dt) for _, w, dt in _PROJ_SEGS],
        compiler_params=pltpu.CompilerParams(dimension_semantics=("arbitrary", "arbitrary"),
                                             vmem_limit_bytes=_VMEM_LIMIT),
        name="in_projection",
    )(x, norm_g.reshape(1, D), sc.reshape(B, 1, D), sh.reshape(B, 1, D), w_packed)


HG_LT = 512


def _split3_bf16(x):
    a = x.astype(jnp.bfloat16)
    r = x - a.astype(jnp.float32)
    b = r.astype(jnp.bfloat16)
    c = (r - b.astype(jnp.float32)).astype(jnp.bfloat16)
    return a, b, c


def _hgrn_kernel(q_ref, f_ref, i_ref, g_ref, lb_ref, og_ref, o_ref, st_ref, *, layer):
    C = HG_CHUNK
    bf = jnp.bfloat16

    @pl.when(pl.program_id(1) == 0)
    def _():
        st_ref[...] = jnp.zeros(st_ref.shape, jnp.float32)

    lbr = lb_ref[...]
    e = jnp.exp(lbr - jnp.max(lbr, axis=0, keepdims=True))
    sm = e / jnp.sum(e, axis=0, keepdims=True)
    lb_all = jnp.sum(sm[0:layer + 1], axis=0, keepdims=True)

    r_i = lax.broadcasted_iota(jnp.int32, (C, C), 0)
    c_i = lax.broadcasted_iota(jnp.int32, (C, C), 1)
    causal = c_i <= r_i
    tri = jnp.where(causal, 1.0, 0.0).astype(bf)
    og = og_ref[...]

    def chunk(c, carry):
        r0 = pl.multiple_of(c * C, C)
        H = range(HG_HEADS)
        cols = [slice(h * HG_DK, (h + 1) * HG_DK) for h in H]
        dot = functools.partial(jnp.dot, preferred_element_type=jnp.float32)
        nt = (((1,), (1,)), ((), ()))
        tn = (((0,), (0,)), ((), ()))
        qr = [q_ref[0, pl.ds(r0, C), cs] for cs in cols]
        q = [x * (1.0 / (1.0 + jnp.exp(-x))) for x in qr]
        f = [lb_all[:, cs] + (1.0 - lb_all[:, cs]) * (1.0 / (1.0 + jnp.exp(-f_ref[0, pl.ds(r0, C), cs])))
             for cs in cols]
        k = [1.0 - x for x in f]
        vb = [i_ref[0, pl.ds(r0, C), cs].astype(bf) for cs in cols]
        lg = [_split3_bf16(jnp.log(x)) for x in f]
        b = [dot(tri, l1) + dot(tri, l2) + dot(tri, l3) for l1, l2, l3 in lg]
        bm = [x[C // 2 - 1:C // 2, :] for x in b]
        bl = [x[C - 1:C, :] for x in b]
        a = [lax.dot_general((q[h] * jnp.exp(b[h] - bm[h])).astype(bf), (k[h] * jnp.exp(bm[h] - b[h])).astype(bf),
                             nt, preferred_element_type=jnp.float32) for h in H]
        st = [st_ref[h] for h in H]
        o_inter = [lax.dot_general((q[h] * jnp.exp(b[h])).astype(bf), st[h].astype(bf), nt,
                                   preferred_element_type=jnp.float32) for h in H]
        ds = [lax.dot_general(vb[h], (k[h] * jnp.exp(bl[h] - b[h])).astype(bf), tn,
                              preferred_element_type=jnp.float32) for h in H]
        for h in H:
            st_ref[h] = st[h] * jnp.exp(bl[h]) + ds[h]
        o = [dot(jnp.where(causal, a[h], 0.0).astype(bf), vb[h]) + o_inter[h] for h in H]
        for h in H:
            on = o[h] * lax.rsqrt(jnp.mean(o[h] * o[h], axis=-1, keepdims=True) + RMS_EPS) * og
            gr = g_ref[0, pl.ds(r0, C), cols[h]]
            o_ref[0, pl.ds(r0, C), cols[h]] = (on * (gr * (1.0 / (1.0 + jnp.exp(-gr))))).astype(o_ref.dtype)
        return carry
    lax.fori_loop(0, q_ref.shape[1] // C, chunk, 0)


def hgrn2(hg, hg_lb, onorm_g, layer):
    B, S, _ = hg.shape
    LT = HG_LT
    assert S % LT == 0

    def piece(p):
        return pl.BlockSpec((1, LT, HG_WIDTH), lambda b, i, p=p: (b, i, p))
    return pl.pallas_call(
        functools.partial(_hgrn_kernel, layer=layer),
        grid=(B, S // LT),
        in_specs=[piece(0), piece(1), piece(2), piece(3),
                  pl.BlockSpec(hg_lb.shape, lambda b, i: (0, 0)),
                  pl.BlockSpec((1, HG_DK), lambda b, i: (0, 0))],
        out_specs=pl.BlockSpec((1, LT, HG_WIDTH), lambda b, i: (b, i, 0)),
        out_shape=jax.ShapeDtypeStruct((B, S, HG_WIDTH), jnp.bfloat16),
        scratch_shapes=[pltpu.VMEM((HG_HEADS, HG_DK, HG_DK), jnp.float32)],
        compiler_params=pltpu.CompilerParams(dimension_semantics=("arbitrary", "arbitrary"),
                                             vmem_limit_bytes=_VMEM_LIMIT),
        name="hgrn2",
    )(hg, hg, hg, hg, hg_lb, onorm_g.reshape(1, HG_DK))


OUT_TM = 512


def _out_kernel(ya_ref, yb_ref, ga_ref, gb_ref, x_ref, g1_ref, n2_ref, sc_ref, sh_ref,
                wa_ref, wb_ref, wo_ref, wq_ref, x1_ref, h2_ref, qry_ref):
    bf = jnp.bfloat16

    def sig(z):
        return 1.0 / (1.0 + jnp.exp(-z))
    ma = jnp.dot(ya_ref[0], wa_ref[...], preferred_element_type=jnp.float32)
    mb = jnp.dot(yb_ref[0], wb_ref[...], preferred_element_type=jnp.float32)
    merged = sig(ga_ref[0]) * ma + sig(gb_ref[0]) * mb
    x1 = x_ref[0] + g1_ref[0] * jnp.dot(merged.astype(bf), wo_ref[...], preferred_element_type=jnp.float32)
    x1_ref[0] = x1
    ms = jnp.mean(x1 * x1, axis=-1, keepdims=True)
    h2 = (x1 * lax.rsqrt(ms + RMS_EPS) * n2_ref[...] * (1.0 + sc_ref[0]) + sh_ref[0]).astype(bf)
    h2_ref[0] = h2
    qry_ref[0] = jnp.dot(h2, wq_ref[...], preferred_element_type=jnp.float32)


def out_projection(ya, yb, gate, x, g1, norm2_g, sc2, sh2, w_a, w_b, w_out, wq):
    B, S, D = x.shape
    tm = OUT_TM
    bf = jnp.bfloat16
    NQ = wq.shape[1]

    def tok(width, col=0):
        return pl.BlockSpec((1, tm, width), lambda b, i, col=col: (b, i, col))

    def per_batch():
        return pl.BlockSpec((1, 1, D), lambda b, i: (b, 0, 0))

    def const(shape):
        return pl.BlockSpec(shape, lambda b, i: (0,) * len(shape), pipeline_mode=pl.Buffered(1))
    return pl.pallas_call(
        _out_kernel,
        grid=(B, S // tm),
        in_specs=[tok(HG_WIDTH), tok(512), tok(D, 0), tok(D, 1), tok(D), per_batch(),
                  const((1, D)), per_batch(), per_batch(),
                  const(w_a.shape), const(w_b.shape), const(w_out.shape), const(wq.shape)],
        out_specs=[tok(D), tok(D), tok(NQ)],
        out_shape=[jax.ShapeDtypeStruct((B, S, D), jnp.float32),
                   jax.ShapeDtypeStruct((B, S, D), jnp.bfloat16),
                   jax.ShapeDtypeStruct((B, S, NQ), jnp.float32)],
        compiler_params=pltpu.CompilerParams(dimension_semantics=("arbitrary", "arbitrary"),
                                             vmem_limit_bytes=_VMEM_LIMIT),
        name="out_projection",
    )(ya, yb, gate, gate, x, g1.reshape(B, 1, D), norm2_g.reshape(1, D), sc2.reshape(B, 1, D),
      sh2.reshape(B, 1, D), w_a.astype(bf), w_b.astype(bf), w_out.astype(bf), wq.astype(bf))


DSA_TQ = 128
DSA_KC = 512
DSA_KA = 512
_INT_MIN = -2 ** 31
_FILL_KEY = -2139095041
_NEG_BIG = -1e30


_MIN_NORMAL = 2.0 ** -126


def _key_to_float(c):
    f = lax.bitcast_convert_type(c ^ ((c >> 31) & jnp.int32(0x7FFFFFFF)), jnp.float32)
    return jnp.where(jnp.logical_and(c > 0, c < 0x00800000), jnp.float32(_MIN_NORMAL), f)


def _dsa_kernel(q_ref, iq_ref, iw_ref, k_ref, v_ref, ik_ref, qg_ref, kg_ref, o_ref,
                kn_ref, vt_ref, qs_ref, keys_ref, lim_ref, m_ref, l_ref, acc_ref, sc_ref, *, seq, topk):
    TQ, KC, KA = DSA_TQ, DSA_KC, DSA_KA
    bf = jnp.bfloat16
    j = pl.program_id(1)
    n_kc = (j * TQ + TQ + KC - 1) // KC
    idx_scale = (IDX_DIM ** -0.5) * (IDX_HEADS ** -0.5)
    att_scale = ATT_HEAD_DIM ** -0.5
    nt = (((1,), (1,)), ((), ()))

    lane = lax.broadcasted_iota(jnp.int32, (1, 128), 1)
    krow = lax.broadcasted_iota(jnp.int32, (KC, 1), 0)

    def kfeat(kpos):
        a = (kpos // 64).astype(jnp.float32)
        b = (kpos % 64).astype(jnp.float32)
        return jnp.where(lane == ATT_HEAD_DIM, a, jnp.where(lane == ATT_HEAD_DIM + 1, b, 0.0))

    def qfeat(slope):
        return jnp.where(lane == ATT_HEAD_DIM, 64.0 * slope, jnp.where(lane == ATT_HEAD_DIM + 1, slope, 0.0))

    @pl.when(j == 0)
    def _():
        def body(c, carry):
            r0 = pl.multiple_of(c * KC, KC)
            for g in range(ATT_GROUPS):
                kk = k_ref[0, pl.ds(r0, KC), g * 128:(g + 1) * 128]
                ms = jnp.sum(kk * kk, axis=-1, keepdims=True) * (1.0 / ATT_HEAD_DIM)
                kn = kk * lax.rsqrt(ms + RMS_EPS) * kg_ref[...]
                kn_ref[g, pl.ds(r0, KC), :] = (kn + kfeat(r0 + krow)).astype(bf)
            vt_ref[c] = v_ref[0, pl.ds(r0, KC), :].astype(jnp.float32).T.astype(bf)
            return carry
        lax.fori_loop(0, seq // KC, body, 0)

    for h in range(ATT_HEADS):
        qq = q_ref[0, :, h * 128:(h + 1) * 128]
        ms = jnp.sum(qq * qq, axis=-1, keepdims=True) * (1.0 / ATT_HEAD_DIM)
        slope = float(2.0 ** (-8.0 * (h + 1) / ATT_HEADS))
        qs_ref[h] = (qq * lax.rsqrt(ms + RMS_EPS) * qg_ref[...] * att_scale + qfeat(slope)).astype(bf)
    iwt = iw_ref[0].T

    qpos = j * TQ + lax.broadcasted_iota(jnp.int32, (1, TQ), 1)

    def idx_body(c, carry):
        r0 = pl.multiple_of(c * KC, KC)
        ikc = ik_ref[0, pl.ds(r0, KC), :]
        acc = jnp.zeros((KC, TQ), jnp.float32)
        for h in range(IDX_HEADS):
            s = lax.dot_general(ikc, iq_ref[0, :, h * 128:(h + 1) * 128], nt,
                                preferred_element_type=jnp.float32)
            acc = acc + iwt[h:h + 1, :] * jnp.maximum(s, 0.0)
        score = acc * idx_scale
        score = jnp.where(jnp.abs(score) < _MIN_NORMAL, 0.0, score)
        keys_ref[c] = jnp.where(c * KC + krow <= qpos, score, -jnp.inf)
        return carry
    lax.fori_loop(0, n_kc, idx_body, 0)

    def row_fold(m):
        parts = [m[8 * i:8 * i + 8] for i in range(m.shape[0] // 8)]
        while len(parts) > 1:
            parts = [parts[i] + parts[i + 1] for i in range(0, len(parts), 2)]
        return parts[0]

    def count(pred):
        def body(c, acc):
            return acc + row_fold(jnp.where(pred(c, keys_ref[c]), 1.0, 0.0))
        acc = lax.fori_loop(0, n_kc, body, jnp.zeros((8, TQ), jnp.float32))
        return jnp.sum(acc, axis=0, keepdims=True)

    def count_ge(cand):
        cf = _key_to_float(cand)
        return count(lambda c, key: jnp.logical_or(key >= cf, cand < _FILL_KEY))

    kf = float(topk)
    ans0 = jnp.where(count_ge(jnp.zeros((1, TQ), jnp.int32)) >= kf, jnp.int32(0), jnp.int32(_INT_MIN))

    def bit_body(i, ans):
        cand = ans | jnp.left_shift(jnp.int32(1), 30 - i)
        return jnp.where(count_ge(cand) >= kf, cand, ans)
    thr = lax.fori_loop(0, 31, bit_body, ans0)

    thr_i = thr
    thr = _key_to_float(thr_i)
    n_gt = count(lambda c, key: key > thr)
    n_ge = count(lambda c, key: key >= thr)
    need = kf - n_gt
    tie = jnp.logical_and(n_ge - n_gt > need, thr_i != _FILL_KEY)
    lim_ref[...] = jnp.full((1, TQ), seq, jnp.int32)

    @pl.when(jnp.max(jnp.where(tie, 1.0, 0.0)) > 0.0)
    def _():
        nbits = max(1, int(np.ceil(np.log2(seq))))

        def bit2(i, lo):
            cand = lo | jnp.left_shift(jnp.int32(1), nbits - 1 - i)
            n_before = count(lambda c, key: jnp.logical_and(key == thr, c * KC + krow < cand))
            return jnp.where(n_before < need, cand, lo)
        lo = lax.fori_loop(0, nbits, bit2, jnp.zeros((1, TQ), jnp.int32))
        lim_ref[...] = jnp.where(tie, lo, jnp.int32(seq))

    lim = lim_ref[...]

    m_ref[...] = jnp.full(m_ref.shape, _NEG_BIG, jnp.float32)
    l_ref[...] = jnp.zeros(l_ref.shape, jnp.float32)
    acc_ref[...] = jnp.zeros(acc_ref.shape, jnp.float32)
    hd = ATT_HEAD_DIM

    def att_body(c, carry):
        vtc = vt_ref[c]
        for half in range(KC // KA):
            h0 = half * KA
            r0 = pl.multiple_of(c * KC, KC) + h0
            kpos = c * KC + h0 + krow[0:KA]
            key = keys_ref[c, h0:h0 + KA, :]
            sel = jnp.logical_or(key > thr, jnp.logical_and(key == thr, kpos <= lim))
            sel = jnp.logical_and(sel, kpos <= qpos)
            madd = jnp.where(sel, 0.0, _NEG_BIG)
            mx = []
            for h in range(ATT_HEADS):
                kc = kn_ref[h // ATT_REP, pl.ds(r0, KA), :]
                sc = lax.dot_general(kc, qs_ref[h], nt, preferred_element_type=jnp.float32) + madd
                sc_ref[h] = sc
                mx.append(jnp.max(sc, axis=0, keepdims=True))
            for h in range(ATT_HEADS):
                g = h // ATT_REP
                m_old = m_ref[h]
                m_new = jnp.maximum(m_old, mx[h])
                p = jnp.exp(sc_ref[h] - m_new)
                alpha = jnp.exp(m_old - m_new)
                l_ref[h] = alpha * l_ref[h] + jnp.sum(p, axis=0, keepdims=True)
                m_ref[h] = m_new
                pv = jnp.dot(vtc[g * hd:(g + 1) * hd, h0:h0 + KA], p.astype(bf),
                             preferred_element_type=jnp.float32)
                rs = slice((h % 2) * hd, (h % 2 + 1) * hd)
                acc_ref[h // 2, rs, :] = acc_ref[h // 2, rs, :] * alpha + pv
        return carry
    lax.fori_loop(0, n_kc, att_body, 0)

    for pair in range(ATT_HEADS // 2):
        den = jnp.concatenate([jnp.broadcast_to(l_ref[2 * pair], (hd, TQ)),
                               jnp.broadcast_to(l_ref[2 * pair + 1], (hd, TQ))], axis=0)
        o_ref[0, :, pair * 128:(pair + 1) * 128] = (acc_ref[pair] / den).T.astype(o_ref.dtype)


def dsa_attention(q, iq, iw, k, v, ik, qg, kg):
    B, S, _ = q.shape
    TQ, KC = DSA_TQ, DSA_KC
    assert S % KC == 0 and S % TQ == 0
    topk = min(IDX_TOPK_MAX, S // 4)
    kern = functools.partial(_dsa_kernel, seq=S, topk=topk)

    def pad_gain(g):
        return jnp.concatenate([g, jnp.zeros_like(g)]).reshape(1, 128)
    return pl.pallas_call(
        kern,
        grid=(B, S // TQ),
        in_specs=[
            pl.BlockSpec((1, TQ, ATT_HEADS * 128), lambda b, j: (b, j, 0)),
            pl.BlockSpec((1, TQ, IDX_HEADS * 128), lambda b, j: (b, j, 0)),
            pl.BlockSpec((1, TQ, 128), lambda b, j: (b, j, 0)),
            pl.BlockSpec((1, S, ATT_GROUPS * 128), lambda b, j: (b, 0, 0)),
            pl.BlockSpec((1, S, ATT_GROUPS * ATT_HEAD_DIM), lambda b, j: (b, 0, 0)),
            pl.BlockSpec((1, S, 128), lambda b, j: (b, 0, 0)),
            pl.BlockSpec((1, 128), lambda b, j: (0, 0)),
            pl.BlockSpec((1, 128), lambda b, j: (0, 0)),
        ],
        out_specs=pl.BlockSpec((1, TQ, ATT_HEADS * ATT_HEAD_DIM), lambda b, j: (b, j, 0)),
        out_shape=jax.ShapeDtypeStruct((B, S, ATT_HEADS * ATT_HEAD_DIM), jnp.bfloat16),
        scratch_shapes=[
            pltpu.VMEM((ATT_GROUPS, S, 128), jnp.bfloat16),
            pltpu.VMEM((S // KC, ATT_GROUPS * ATT_HEAD_DIM, KC), jnp.bfloat16),
            pltpu.VMEM((ATT_HEADS, TQ, 128), jnp.bfloat16),
            pltpu.VMEM((S // KC, KC, TQ), jnp.float32),
            pltpu.VMEM((1, TQ), jnp.int32),
            pltpu.VMEM((ATT_HEADS, 1, TQ), jnp.float32),
            pltpu.VMEM((ATT_HEADS, 1, TQ), jnp.float32),
            pltpu.VMEM((ATT_HEADS // 2, 2 * ATT_HEAD_DIM, TQ), jnp.float32),
            pltpu.VMEM((ATT_HEADS, DSA_KA, TQ), jnp.float32),
        ],
        compiler_params=pltpu.CompilerParams(
            dimension_semantics=("arbitrary", "arbitrary"),
            vmem_limit_bytes=_VMEM_LIMIT),
        name="dsa_attention",
    )(q, iq, iw, k, v, ik, pad_gain(qg), pad_gain(kg))


PEER_RT = 256
_EDGES = PEER_HEADS * PEER_TOPK


def _extract_topk_rows(work, n_out, val_ref, pay_ref, order=None, payload=None):
    if order is None:
        order = lax.broadcasted_iota(jnp.int32, work.shape, 0)
    big = jnp.int32(2 ** 30)
    for it in range(n_out):
        m = jnp.max(work, axis=0, keepdims=True)
        am = jnp.min(jnp.where(work == m, order, big), axis=0, keepdims=True)
        hit = order == am
        val_ref[it:it + 1, :] = m
        if payload is None:
            pay_ref[it:it + 1, :] = am
        else:
            pay_ref[it:it + 1, :] = jnp.max(jnp.where(hit, payload, -1), axis=0, keepdims=True)
        work = jnp.where(hit, -jnp.inf, work)


def _peer_route_kernel(q_ref, sk_ref, e_ref, g_ref, s1_ref, i1_ref, s2_ref, i2_ref, ts_ref, te_ref, et_ref,
                       gt_ref):
    K = PEER_TOPK
    RT = q_ref.shape[0]
    b8 = lax.broadcasted_iota(jnp.int32, (8, RT), 0)
    b16 = lax.broadcasted_iota(jnp.int32, (K, RT), 0)
    order = jnp.concatenate([b16] + [a * K + b8 for a in range(1, 8)] + [(8 + b8) * K], axis=0)
    for h in range(PEER_HEADS):
        for p, (s_ref, i_ref) in enumerate(((s1_ref, i1_ref), (s2_ref, i2_ref))):
            col = (h * 2 + p) * 128
            qs = q_ref[:, col:col + 128].astype(jnp.bfloat16)
            sk = sk_ref[h, p].astype(jnp.bfloat16)
            sc = lax.dot_general(sk, qs, (((1,), (1,)), ((), ())),
                                 preferred_element_type=jnp.float32)
            _extract_topk_rows(sc, K, s_ref, i_ref)
        s2 = s2_ref[...]
        i2 = i2_ref[...]
        cand = [s1_ref[0:1, :] + s2]
        cidx = [i1_ref[0:1, :] * PEER_NKEYS + i2]
        for a in range(1, 8):
            cand.append(jnp.where(b8 < K // (a + 1), s1_ref[a:a + 1, :] + s2[0:8], -jnp.inf))
            cidx.append(i1_ref[a:a + 1, :] * PEER_NKEYS + i2[0:8])
        cand.append(s1_ref[8:K, :] + s2[0:1])
        cidx.append(i1_ref[8:K, :] * PEER_NKEYS + i2[0:1])
        _extract_topk_rows(jnp.concatenate(cand, axis=0), K, ts_ref, te_ref, order, jnp.concatenate(cidx, axis=0))
        ts = ts_ref[...]
        ex = jnp.exp(ts - jnp.max(ts, axis=0, keepdims=True))
        gt_ref[h * K:(h + 1) * K, :] = ex / jnp.sum(ex, axis=0, keepdims=True)
        et_ref[h * K:(h + 1) * K, :] = te_ref[...]
    e_ref[...] = et_ref[...].T
    g_ref[...] = gt_ref[...].T


def peer_route(qry, subkeys):
    T = qry.shape[0]
    RT = PEER_RT
    assert T % RT == 0
    K = PEER_TOPK
    return pl.pallas_call(
        _peer_route_kernel,
        grid=(T // RT,),
        in_specs=[
            pl.BlockSpec((RT, qry.shape[1]), lambda i: (i, 0)),
            pl.BlockSpec(subkeys.shape, lambda i: (0, 0, 0, 0)),
        ],
        out_specs=[pl.BlockSpec((RT, _EDGES), lambda i: (i, 0)),
                   pl.BlockSpec((RT, _EDGES), lambda i: (i, 0))],
        out_shape=[jax.ShapeDtypeStruct((T, _EDGES), jnp.int32),
                   jax.ShapeDtypeStruct((T, _EDGES), jnp.float32)],
        scratch_shapes=[
            pltpu.VMEM((K, RT), jnp.float32), pltpu.VMEM((K, RT), jnp.int32),
            pltpu.VMEM((K, RT), jnp.float32), pltpu.VMEM((K, RT), jnp.int32),
            pltpu.VMEM((K, RT), jnp.float32), pltpu.VMEM((K, RT), jnp.int32),
            pltpu.VMEM((_EDGES, RT), jnp.int32),
            pltpu.VMEM((_EDGES, RT), jnp.float32),
        ],
        compiler_params=pltpu.CompilerParams(dimension_semantics=("arbitrary",)),
        name="peer_route",
    )(qry, subkeys)


PEER_CT = 256
PEER_DT = 1024
PEER_DE = 1024
PEER_DS = 1024
_COEF_UNROLL = 64


def _cast_kernel(t_ref, o_ref):
    o_ref[...] = t_ref[...].astype(o_ref.dtype)


def _to_bf16(tab):
    n, d = tab.shape
    tr = 512
    assert n % tr == 0
    return pl.pallas_call(
        _cast_kernel,
        grid=(n // tr,),
        in_specs=[pl.BlockSpec((tr, d), lambda i: (i, 0))],
        out_specs=pl.BlockSpec((tr, d), lambda i: (i, 0)),
        out_shape=jax.ShapeDtypeStruct((n, d), jnp.bfloat16),
        compiler_params=pltpu.CompilerParams(dimension_semantics=("arbitrary",)),
        name="table_to_bf16",
    )(tab)


def _peer_coef_kernel(e_ref, g_ref, c_ref):
    nk = PEER_NKEYS
    rows = lax.broadcasted_iota(jnp.int32, (nk, _EDGES), 0)
    shift = nk.bit_length() - 1

    def body(tt, carry):
        ts = [tt * _COEF_UNROLL + u for u in range(_COEF_UNROLL)]
        es = [e_ref[pl.ds(t, 1), :] for t in ts]
        a = [jnp.where(rows == (e >> shift), g_ref[pl.ds(t, 1), :], 0.0).astype(jnp.bfloat16)
             for t, e in zip(ts, es)]
        b = [jnp.where(rows == (e & (nk - 1)), 1.0, 0.0).astype(jnp.bfloat16) for e in es]
        c = [lax.dot_general(x, y, (((1,), (1,)), ((), ())), preferred_element_type=jnp.float32)
             for x, y in zip(a, b)]
        for t, x in zip(ts, c):
            c_ref[t] = x.astype(c_ref.dtype)
        return carry
    lax.fori_loop(0, e_ref.shape[0] // _COEF_UNROLL, body, 0)


def peer_coefficients(eidx, gates):
    T = eidx.shape[0]
    CT = PEER_CT
    assert T % CT == 0 and PEER_NKEYS & (PEER_NKEYS - 1) == 0
    return pl.pallas_call(
        _peer_coef_kernel,
        grid=(T // CT,),
        in_specs=[pl.BlockSpec((CT, _EDGES), lambda i: (i, 0)),
                  pl.BlockSpec((CT, _EDGES), lambda i: (i, 0))],
        out_specs=pl.BlockSpec((CT, PEER_NKEYS, PEER_NKEYS), lambda i: (i, 0, 0)),
        out_shape=jax.ShapeDtypeStruct((T, PEER_NKEYS, PEER_NKEYS), jnp.bfloat16),
        compiler_params=pltpu.CompilerParams(dimension_semantics=("arbitrary",)),
        name="peer_coef",
    )(eidx, gates)


def _peer_dense_kernel(h_ref, c_ref, u_ref, v_ref, x1_ref, g2_ref, y_ref, acc_ref):
    j = pl.program_id(1)
    bf = jnp.bfloat16

    @pl.when(j == 0)
    def _():
        acc_ref[...] = jnp.zeros(acc_ref.shape, jnp.float32)

    h = h_ref[...]
    dt, de = h.shape[0], u_ref.shape[0]
    coef = c_ref[...].reshape(dt, de)
    for s in range(0, de, PEER_DS):
        a = lax.dot_general(h, u_ref[s:s + PEER_DS, :], (((1,), (1,)), ((), ())),
                            preferred_element_type=jnp.float32)
        act = 0.5 * a * (1.0 + lax.erf(a * (2.0 ** -0.5)))
        w = (coef[:, s:s + PEER_DS].astype(jnp.float32) * act).astype(bf)
        acc_ref[...] += jnp.dot(w, v_ref[s:s + PEER_DS, :], preferred_element_type=jnp.float32)

    @pl.when(j == pl.num_programs(1) - 1)
    def _():
        y_ref[...] = x1_ref[...] + g2_ref[0] * acc_ref[...]


def peer_experts(h2, coef, u, vtab, x1, g2, seq):
    T, D = h2.shape
    N = u.shape[0]
    DT, DE = PEER_DT, PEER_DE
    rows = DE // PEER_NKEYS
    assert T % DT == 0 and N % DE == 0 and seq % DT == 0 and N == PEER_NKEYS * PEER_NKEYS
    return pl.pallas_call(
        _peer_dense_kernel,
        grid=(T // DT, N // DE),
        in_specs=[
            pl.BlockSpec((DT, D), lambda i, j: (i, 0)),
            pl.BlockSpec((DT, rows, PEER_NKEYS), lambda i, j: (i, j, 0)),
            pl.BlockSpec((DE, D), lambda i, j: (j, 0)),
            pl.BlockSpec((DE, D), lambda i, j: (j, 0)),
            pl.BlockSpec((DT, D), lambda i, j: (i, 0)),
            pl.BlockSpec((1, 1, D), lambda i, j: (i * DT // seq, 0, 0)),
        ],
        out_specs=pl.BlockSpec((DT, D), lambda i, j: (i, 0)),
        out_shape=jax.ShapeDtypeStruct((T, D), jnp.float32),
        scratch_shapes=[pltpu.VMEM((DT, D), jnp.float32)],
        compiler_params=pltpu.CompilerParams(dimension_semantics=("arbitrary", "arbitrary"),
                                             vmem_limit_bytes=56 * 1024 * 1024),
        name="peer_dense",
    )(h2, coef, _to_bf16(u), _to_bf16(vtab), x1, g2.reshape(g2.shape[0], 1, D))


def kernel(x, c, w_ada, b_ada, norm1_g, norm2_g, w_in, hg_lb, hg_onorm_g, q_norm_g, k_norm_g,
           w_a, w_b, w_out, peer_wq, peer_subkeys, peer_u, peer_v):
    B, S, D = x.shape
    T = B * S
    for l in range(w_ada.shape[0]):
        mod = ada_modulation(c, w_ada[l], b_ada[l])
        sh1, sc1, g1, sh2, sc2, g2 = jnp.split(mod, 6, axis=-1)
        hg, q, k, iw, iq, ik, v, gate = in_projection(x, norm1_g[l], sc1, sh1, pack_w_in(w_in[l]))
        ya = hgrn2(hg, hg_lb, hg_onorm_g[l], l)
        yb = dsa_attention(q, iq, iw, k, v, ik, q_norm_g[l], k_norm_g[l])
        x1, h2, qry = out_projection(ya, yb, gate, x, g1, norm2_g[l], sc2, sh2,
                                     w_a[l], w_b[l], w_out[l], peer_wq[l])
        eidx, gates = peer_route(qry.reshape(T, -1), peer_subkeys[l])
        x = peer_experts(h2.reshape(T, D), peer_coefficients(eidx, gates), peer_u[l], peer_v[l],
                         x1.reshape(T, D), g2, S).reshape(B, S, D)
    return x
```

```python
import functools

import jax
import jax.numpy as jnp
import numpy as np
from jax import lax
from jax.experimental import pallas as pl
from jax.experimental.pallas import tpu as pltpu

D_MODEL = 1024
HG_HEADS = 4
HG_DK = 128
HG_WIDTH = HG_HEADS * HG_DK
HG_CHUNK = 64
ATT_HEADS = 8
ATT_GROUPS = 2
ATT_HEAD_DIM = 64
ATT_REP = ATT_HEADS // ATT_GROUPS
IDX_HEADS = 8
IDX_DIM = 64
IDX_TOPK_MAX = 256
PEER_HEADS = 8
PEER_NKEYS = 128
PEER_TOPK = 16
RMS_EPS = 1e-6
IN_SPLIT = (HG_WIDTH, HG_WIDTH, HG_WIDTH, HG_WIDTH,
            ATT_HEADS * ATT_HEAD_DIM, ATT_GROUPS * ATT_HEAD_DIM, ATT_GROUPS * ATT_HEAD_DIM,
            IDX_HEADS * IDX_DIM, IDX_DIM, IDX_HEADS,
            D_MODEL, D_MODEL)

_VMEM_LIMIT = 48 * 1024 * 1024


def _ada_kernel(c_ref, w_ref, b_ref, o_ref):
    c = c_ref[...]
    cs = c * (1.0 / (1.0 + jnp.exp(-c)))
    o_ref[...] = jnp.dot(cs, w_ref[...], preferred_element_type=jnp.float32,
                         precision=lax.Precision.HIGHEST) + b_ref[...]


def ada_modulation(c, w, b):
    B, D = c.shape
    N = w.shape[1]
    tn = 1536
    assert N % tn == 0
    return pl.pallas_call(
        _ada_kernel,
        grid=(N // tn,),
        in_specs=[pl.BlockSpec((B, D), lambda j: (0, 0)),
                  pl.BlockSpec((D, tn), lambda j: (0, j)),
                  pl.BlockSpec((1, tn), lambda j: (0, j))],
        out_specs=pl.BlockSpec((B, tn), lambda j: (0, j)),
        out_shape=jax.ShapeDtypeStruct((B, N), jnp.float32),
        compiler_params=pltpu.CompilerParams(dimension_semantics=("arbitrary",),
                                             vmem_limit_bytes=_VMEM_LIMIT),
        name="ada_modulation",
    )(c, w, b.reshape(1, N))


_PROJ_SEGS = (("hg", 2048, jnp.float32), ("q", 1024, jnp.float32), ("k", 256, jnp.float32),
              ("iw", 128, jnp.float32), ("iq", 1024, jnp.bfloat16), ("ik", 128, jnp.bfloat16),
              ("v", 128, jnp.bfloat16), ("gate", 2048, jnp.float32))
PROJ_TM = 512


def pack_w_in(w_in):
    D = w_in.shape[0]
    offs = np.cumsum((0,) + IN_SPLIT)
    hq, hf, hi, hg, aq, ak, av, iq, ik, iw, ga, gb = [w_in[:, offs[i]:offs[i + 1]] for i in range(12)]

    def pad_heads(w, nh):
        w = w.reshape(D, nh, 64)
        return jnp.concatenate([w, jnp.zeros_like(w)], axis=-1).reshape(D, nh * 128)

    iwp = jnp.concatenate([iw, jnp.zeros((D, 128 - IDX_HEADS), w_in.dtype)], axis=1)
    cols = [hq, hf, hi, hg, pad_heads(aq, ATT_HEADS), pad_heads(ak, ATT_GROUPS), iwp,
            pad_heads(iq, IDX_HEADS), pad_heads(ik, 1),
            av, ga, gb]
    return jnp.concatenate(cols, axis=1).astype(jnp.bfloat16)


def _proj_kernel(x_ref, g_ref, sc_ref, sh_ref, w_ref, *o_refs):
    x = x_ref[0]
    ms = jnp.mean(x * x, axis=-1, keepdims=True)
    h = x * lax.rsqrt(ms + RMS_EPS) * g_ref[...] * (1.0 + sc_ref[0]) + sh_ref[0]
    hb = h.astype(jnp.bfloat16)
    c0 = 0
    for (_, width, _), o_ref in zip(_PROJ_SEGS, o_refs):
        for s in range(0, width, 512):
            e = min(s + 512, width)
            o_ref[0, :, s:e] = jnp.dot(hb, w_ref[:, c0 + s:c0 + e],
                                       preferred_element_type=jnp.float32).astype(o_ref.dtype)
        c0 += width


def in_projection(x, norm_g, sc, sh, w_packed):
    B, S, D = x.shape
    tm = PROJ_TM
    assert S % tm == 0
    ntot = sum(w for _, w, _ in _PROJ_SEGS)
    assert w_packed.shape == (D, ntot)
    return pl.pallas_call(
        _proj_kernel,
        grid=(B, S // tm),
        in_specs=[
            pl.BlockSpec((1, tm, D), lambda b, i: (b, i, 0)),
            pl.BlockSpec((1, D), lambda b, i: (0, 0)),
            pl.BlockSpec((1, 1, D), lambda b, i: (b, 0, 0)),
            pl.BlockSpec((1, 1, D), lambda b, i: (b, 0, 0)),
            pl.BlockSpec((D, ntot), lambda b, i: (0, 0), pipeline_mode=pl.Buffered(1)),
        ],
        out_specs=[pl.BlockSpec((1, tm, w), lambda b, i: (b, i, 0)) for _, w, _ in _PROJ_SEGS],
        out_shape=[jax.ShapeDtypeStruct((B, S, w), dt) for _, w, dt in _PROJ_SEGS],
        compiler_params=pltpu.CompilerParams(dimension_semantics=("arbitrary", "arbitrary"),
                                             vmem_limit_bytes=_VMEM_LIMIT),
        name="in_projection",
    )(x, norm_g.reshape(1, D), sc.reshape(B, 1, D), sh.reshape(B, 1, D), w_packed)


HG_LT = 512


def _split3_bf16(x):
    a = x.astype(jnp.bfloat16)
    r = x - a.astype(jnp.float32)
    b = r.astype(jnp.bfloat16)
    c = (r - b.astype(jnp.float32)).astype(jnp.bfloat16)
    return a, b, c


def _hgrn_kernel(q_ref, f_ref, i_ref, g_ref, lb_ref, og_ref, o_ref, st_ref, *, layer):
    C = HG_CHUNK
    bf = jnp.bfloat16

    @pl.when(pl.program_id(1) == 0)
    def _():
        st_ref[...] = jnp.zeros(st_ref.shape, jnp.float32)

    lbr = lb_ref[...]
    e = jnp.exp(lbr - jnp.max(lbr, axis=0, keepdims=True))
    sm = e / jnp.sum(e, axis=0, keepdims=True)
    lb_all = jnp.sum(sm[0:layer + 1], axis=0, keepdims=True)

    r_i = lax.broadcasted_iota(jnp.int32, (C, C), 0)
    c_i = lax.broadcasted_iota(jnp.int32, (C, C), 1)
    causal = c_i <= r_i
    tri = jnp.where(causal, 1.0, 0.0).astype(bf)
    og = og_ref[...]

    def chunk(c, carry):
        r0 = pl.multiple_of(c * C, C)
        H = range(HG_HEADS)
        cols = [slice(h * HG_DK, (h + 1) * HG_DK) for h in H]
        dot = functools.partial(jnp.dot, preferred_element_type=jnp.float32)
        nt = (((1,), (1,)), ((), ()))
        tn = (((0,), (0,)), ((), ()))
        qr = [q_ref[0, pl.ds(r0, C), cs] for cs in cols]
        q = [x * (1.0 / (1.0 + jnp.exp(-x))) for x in qr]
        f = [lb_all[:, cs] + (1.0 - lb_all[:, cs]) * (1.0 / (1.0 + jnp.exp(-f_ref[0, pl.ds(r0, C), cs])))
             for cs in cols]
        k = [1.0 - x for x in f]
        vb = [i_ref[0, pl.ds(r0, C), cs].astype(bf) for cs in cols]
        lg = [_split3_bf16(jnp.log(x)) for x in f]
        b = [dot(tri, l1) + dot(tri, l2) + dot(tri, l3) for l1, l2, l3 in lg]
        bm = [x[C // 2 - 1:C // 2, :] for x in b]
        bl = [x[C - 1:C, :] for x in b]
        a = [lax.dot_general((q[h] * jnp.exp(b[h] - bm[h])).astype(bf), (k[h] * jnp.exp(bm[h] - b[h])).astype(bf),
                             nt, preferred_element_type=jnp.float32) for h in H]
        st = [st_ref[h] for h in H]
        o_inter = [lax.dot_general((q[h] * jnp.exp(b[h])).astype(bf), st[h].astype(bf), nt,
                                   preferred_element_type=jnp.float32) for h in H]
        ds = [lax.dot_general(vb[h], (k[h] * jnp.exp(bl[h] - b[h])).astype(bf), tn,
                              preferred_element_type=jnp.float32) for h in H]
        for h in H:
            st_ref[h] = st[h] * jnp.exp(bl[h]) + ds[h]
        o = [dot(jnp.where(causal, a[h], 0.0).astype(bf), vb[h]) + o_inter[h] for h in H]
        for h in H:
            on = o[h] * lax.rsqrt(jnp.mean(o[h] * o[h], axis=-1, keepdims=True) + RMS_EPS) * og
            gr = g_ref[0, pl.ds(r0, C), cols[h]]
            o_ref[0, pl.ds(r0, C), cols[h]] = (on * (gr * (1.0 / (1.0 + jnp.exp(-gr))))).astype(o_ref.dtype)
        return carry
    lax.fori_loop(0, q_ref.shape[1] // C, chunk, 0)


def hgrn2(hg, hg_lb, onorm_g, layer):
    B, S, _ = hg.shape
    LT = HG_LT
    assert S % LT == 0

    def piece(p):
        return pl.BlockSpec((1, LT, HG_WIDTH), lambda b, i, p=p: (b, i, p))
    return pl.pallas_call(
        functools.partial(_hgrn_kernel, layer=layer),
        grid=(B, S // LT),
        in_specs=[piece(0), piece(1), piece(2), piece(3),
                  pl.BlockSpec(hg_lb.shape, lambda b, i: (0, 0)),
                  pl.BlockSpec((1, HG_DK), lambda b, i: (0, 0))],
        out_specs=pl.BlockSpec((1, LT, HG_WIDTH), lambda b, i: (b, i, 0)),
        out_shape=jax.ShapeDtypeStruct((B, S, HG_WIDTH), jnp.bfloat16),
        scratch_shapes=[pltpu.VMEM((HG_HEADS, HG_DK, HG_DK), jnp.float32)],
        compiler_params=pltpu.CompilerParams(dimension_semantics=("arbitrary", "arbitrary"),
                                             vmem_limit_bytes=_VMEM_LIMIT),
        name="hgrn2",
    )(hg, hg, hg, hg, hg_lb, onorm_g.reshape(1, HG_DK))


OUT_TM = 512


def _out_kernel(ya_ref, yb_ref, ga_ref, gb_ref, x_ref, g1_ref, n2_ref, sc_ref, sh_ref,
                wa_ref, wb_ref, wo_ref, wq_ref, x1_ref, h2_ref, qry_ref):
    bf = jnp.bfloat16

    def sig(z):
        return 1.0 / (1.0 + jnp.exp(-z))
    ma = jnp.dot(ya_ref[0], wa_ref[...], preferred_element_type=jnp.float32)
    mb = jnp.dot(yb_ref[0], wb_ref[...], preferred_element_type=jnp.float32)
    merged = sig(ga_ref[0]) * ma + sig(gb_ref[0]) * mb
    x1 = x_ref[0] + g1_ref[0] * jnp.dot(merged.astype(bf), wo_ref[...], preferred_element_type=jnp.float32)
    x1_ref[0] = x1
    ms = jnp.mean(x1 * x1, axis=-1, keepdims=True)
    h2 = (x1 * lax.rsqrt(ms + RMS_EPS) * n2_ref[...] * (1.0 + sc_ref[0]) + sh_ref[0]).astype(bf)
    h2_ref[0] = h2
    qry_ref[0] = jnp.dot(h2, wq_ref[...], preferred_element_type=jnp.float32)


def out_projection(ya, yb, gate, x, g1, norm2_g, sc2, sh2, w_a, w_b, w_out, wq):
    B, S, D = x.shape
    tm = OUT_TM
    bf = jnp.bfloat16
    NQ = wq.shape[1]

    def tok(width, col=0):
        return pl.BlockSpec((1, tm, width), lambda b, i, col=col: (b, i, col))

    def per_batch():
        return pl.BlockSpec((1, 1, D), lambda b, i: (b, 0, 0))

    def const(shape):
        return pl.BlockSpec(shape, lambda b, i: (0,) * len(shape), pipeline_mode=pl.Buffered(1))
    return pl.pallas_call(
        _out_kernel,
        grid=(B, S // tm),
        in_specs=[tok(HG_WIDTH), tok(512), tok(D, 0), tok(D, 1), tok(D), per_batch(),
                  const((1, D)), per_batch(), per_batch(),
                  const(w_a.shape), const(w_b.shape), const(w_out.shape), const(wq.shape)],
        out_specs=[tok(D), tok(D), tok(NQ)],
        out_shape=[jax.ShapeDtypeStruct((B, S, D), jnp.float32),
                   jax.ShapeDtypeStruct((B, S, D), jnp.bfloat16),
                   jax.ShapeDtypeStruct((B, S, NQ), jnp.float32)],
        compiler_params=pltpu.CompilerParams(dimension_semantics=("arbitrary", "arbitrary"),
                                             vmem_limit_bytes=_VMEM_LIMIT),
        name="out_projection",
    )(ya, yb, gate, gate, x, g1.reshape(B, 1, D), norm2_g.reshape(1, D), sc2.reshape(B, 1, D),
      sh2.reshape(B, 1, D), w_a.astype(bf), w_b.astype(bf), w_out.astype(bf), wq.astype(bf))


DSA_TQ = 128
DSA_KC = 512
DSA_KA = 512
_INT_MIN = -2 ** 31
_FILL_KEY = -2139095041
_NEG_BIG = -1e30


_MIN_NORMAL = 2.0 ** -126


def _key_to_float(c):
    f = lax.bitcast_convert_type(c ^ ((c >> 31) & jnp.int32(0x7FFFFFFF)), jnp.float32)
    return jnp.where(jnp.logical_and(c > 0, c < 0x00800000), jnp.float32(_MIN_NORMAL), f)


def _dsa_kernel(q_ref, iq_ref, iw_ref, k_ref, v_ref, ik_ref, qg_ref, kg_ref, o_ref,
                kn_ref, vt_ref, qs_ref, keys_ref, lim_ref, m_ref, l_ref, acc_ref, sc_ref, *, seq, topk):
    TQ, KC, KA = DSA_TQ, DSA_KC, DSA_KA
    bf = jnp.bfloat16
    j = pl.program_id(1)
    n_kc = (j * TQ + TQ + KC - 1) // KC
    idx_scale = (IDX_DIM ** -0.5) * (IDX_HEADS ** -0.5)
    att_scale = ATT_HEAD_DIM ** -0.5
    nt = (((1,), (1,)), ((), ()))

    lane = lax.broadcasted_iota(jnp.int32, (1, 128), 1)
    krow = lax.broadcasted_iota(jnp.int32, (KC, 1), 0)

    def kfeat(kpos):
        a = (kpos // 64).astype(jnp.float32)
        b = (kpos % 64).astype(jnp.float32)
        return jnp.where(lane == ATT_HEAD_DIM, a, jnp.where(lane == ATT_HEAD_DIM + 1, b, 0.0))

    def qfeat(slope):
        return jnp.where(lane == ATT_HEAD_DIM, 64.0 * slope, jnp.where(lane == ATT_HEAD_DIM + 1, slope, 0.0))

    @pl.when(j == 0)
    def _():
        def body(c, carry):
            r0 = pl.multiple_of(c * KC, KC)
            for g in range(ATT_GROUPS):
                kk = k_ref[0, pl.ds(r0, KC), g * 128:(g + 1) * 128]
                ms = jnp.sum(kk * kk, axis=-1, keepdims=True) * (1.0 / ATT_HEAD_DIM)
                kn = kk * lax.rsqrt(ms + RMS_EPS) * kg_ref[...]
                kn_ref[g, pl.ds(r0, KC), :] = (kn + kfeat(r0 + krow)).astype(bf)
            vt_ref[c] = v_ref[0, pl.ds(r0, KC), :].astype(jnp.float32).T.astype(bf)
            return carry
        lax.fori_loop(0, seq // KC, body, 0)

    for h in range(ATT_HEADS):
        qq = q_ref[0, :, h * 128:(h + 1) * 128]
        ms = jnp.sum(qq * qq, axis=-1, keepdims=True) * (1.0 / ATT_HEAD_DIM)
        slope = float(2.0 ** (-8.0 * (h + 1) / ATT_HEADS))
        qs_ref[h] = (qq * lax.rsqrt(ms + RMS_EPS) * qg_ref[...] * att_scale + qfeat(slope)).astype(bf)
    iwt = iw_ref[0].T

    qpos = j * TQ + lax.broadcasted_iota(jnp.int32, (1, TQ), 1)

    def idx_body(c, carry):
        r0 = pl.multiple_of(c * KC, KC)
        ikc = ik_ref[0, pl.ds(r0, KC), :]
        acc = jnp.zeros((KC, TQ), jnp.float32)
        for h in range(IDX_HEADS):
            s = lax.dot_general(ikc, iq_ref[0, :, h * 128:(h + 1) * 128], nt,
                                preferred_element_type=jnp.float32)
            acc = acc + iwt[h:h + 1, :] * jnp.maximum(s, 0.0)
        score = acc * idx_scale
        score = jnp.where(jnp.abs(score) < _MIN_NORMAL, 0.0, score)
        keys_ref[c] = jnp.where(c * KC + krow <= qpos, score, -jnp.inf)
        return carry
    lax.fori_loop(0, n_kc, idx_body, 0)

    def row_fold(m):
        parts = [m[8 * i:8 * i + 8] for i in range(m.shape[0] // 8)]
        while len(parts) > 1:
            parts = [parts[i] + parts[i + 1] for i in range(0, len(parts), 2)]
        return parts[0]

    def count(pred):
        def body(c, acc):
            return acc + row_fold(jnp.where(pred(c, keys_ref[c]), 1.0, 0.0))
        acc = lax.fori_loop(0, n_kc, body, jnp.zeros((8, TQ), jnp.float32))
        return jnp.sum(acc, axis=0, keepdims=True)

    def count_ge(cand):
        cf = jnp.where(cand < _FILL_KEY, -jnp.inf, _key_to_float(cand))
        return count(lambda c, key: key >= cf)

    kf = float(topk)
    ans0 = jnp.where(count_ge(jnp.zeros((1, TQ), jnp.int32)) >= kf, jnp.int32(0), jnp.int32(_INT_MIN))

    def bit_body(i, ans):
        cand = ans | jnp.left_shift(jnp.int32(1), 30 - i)
        return jnp.where(count_ge(cand) >= kf, cand, ans)
    thr = lax.fori_loop(0, 31, bit_body, ans0)

    thr_i = thr
    thr = _key_to_float(thr_i)
    n_gt = count(lambda c, key: key > thr)
    n_ge = count(lambda c, key: key >= thr)
    need = kf - n_gt
    tie = jnp.logical_and(n_ge - n_gt > need, thr_i != _FILL_KEY)
    lim_ref[...] = jnp.full((1, TQ), seq, jnp.int32)

    @pl.when(jnp.max(jnp.where(tie, 1.0, 0.0)) > 0.0)
    def _():
        nbits = max(1, int(np.ceil(np.log2(seq))))

        def bit2(i, lo):
            cand = lo | jnp.left_shift(jnp.int32(1), nbits - 1 - i)
            n_before = count(lambda c, key: jnp.logical_and(key == thr, c * KC + krow < cand))
            return jnp.where(n_before < need, cand, lo)
        lo = lax.fori_loop(0, nbits, bit2, jnp.zeros((1, TQ), jnp.int32))
        lim_ref[...] = jnp.where(tie, lo, jnp.int32(seq))

    lim = lim_ref[...]

    m_ref[...] = jnp.full(m_ref.shape, _NEG_BIG, jnp.float32)
    l_ref[...] = jnp.zeros(l_ref.shape, jnp.float32)
    acc_ref[...] = jnp.zeros(acc_ref.shape, jnp.float32)
    hd = ATT_HEAD_DIM

    def att_body(c, carry):
        vtc = vt_ref[c]
        for half in range(KC // KA):
            h0 = half * KA
            r0 = pl.multiple_of(c * KC, KC) + h0
            kpos = c * KC + h0 + krow[0:KA]
            key = keys_ref[c, h0:h0 + KA, :]
            sel = jnp.logical_or(key > thr, jnp.logical_and(key == thr, kpos <= lim))
            sel = jnp.logical_and(sel, kpos <= qpos)
            madd = jnp.where(sel, 0.0, _NEG_BIG)
            mx = []
            for h in range(ATT_HEADS):
                kc = kn_ref[h // ATT_REP, pl.ds(r0, KA), :]
                sc = lax.dot_general(kc, qs_ref[h], nt, preferred_element_type=jnp.float32) + madd
                sc_ref[h] = sc
                mx.append(jnp.max(sc, axis=0, keepdims=True))
            for h in range(ATT_HEADS):
                g = h // ATT_REP
                m_old = m_ref[h]
                m_new = jnp.maximum(m_old, mx[h])
                p = jnp.exp(sc_ref[h] - m_new)
                alpha = jnp.exp(m_old - m_new)
                l_ref[h] = alpha * l_ref[h] + jnp.sum(p, axis=0, keepdims=True)
                m_ref[h] = m_new
                pv = jnp.dot(vtc[g * hd:(g + 1) * hd, h0:h0 + KA], p.astype(bf),
                             preferred_element_type=jnp.float32)
                rs = slice((h % 2) * hd, (h % 2 + 1) * hd)
                acc_ref[h // 2, rs, :] = acc_ref[h // 2, rs, :] * alpha + pv
        return carry
    lax.fori_loop(0, n_kc, att_body, 0)

    for pair in range(ATT_HEADS // 2):
        den = jnp.concatenate([jnp.broadcast_to(l_ref[2 * pair], (hd, TQ)),
                               jnp.broadcast_to(l_ref[2 * pair + 1], (hd, TQ))], axis=0)
        o_ref[0, :, pair * 128:(pair + 1) * 128] = (acc_ref[pair] / den).T.astype(o_ref.dtype)


def dsa_attention(q, iq, iw, k, v, ik, qg, kg):
    B, S, _ = q.shape
    TQ, KC = DSA_TQ, DSA_KC
    assert S % KC == 0 and S % TQ == 0
    topk = min(IDX_TOPK_MAX, S // 4)
    kern = functools.partial(_dsa_kernel, seq=S, topk=topk)

    def pad_gain(g):
        return jnp.concatenate([g, jnp.zeros_like(g)]).reshape(1, 128)
    return pl.pallas_call(
        kern,
        grid=(B, S // TQ),
        in_specs=[
            pl.BlockSpec((1, TQ, ATT_HEADS * 128), lambda b, j: (b, j, 0)),
            pl.BlockSpec((1, TQ, IDX_HEADS * 128), lambda b, j: (b, j, 0)),
            pl.BlockSpec((1, TQ, 128), lambda b, j: (b, j, 0)),
            pl.BlockSpec((1, S, ATT_GROUPS * 128), lambda b, j: (b, 0, 0)),
            pl.BlockSpec((1, S, ATT_GROUPS * ATT_HEAD_DIM), lambda b, j: (b, 0, 0)),
            pl.BlockSpec((1, S, 128), lambda b, j: (b, 0, 0)),
            pl.BlockSpec((1, 128), lambda b, j: (0, 0)),
            pl.BlockSpec((1, 128), lambda b, j: (0, 0)),
        ],
        out_specs=pl.BlockSpec((1, TQ, ATT_HEADS * ATT_HEAD_DIM), lambda b, j: (b, j, 0)),
        out_shape=jax.ShapeDtypeStruct((B, S, ATT_HEADS * ATT_HEAD_DIM), jnp.bfloat16),
        scratch_shapes=[
            pltpu.VMEM((ATT_GROUPS, S, 128), jnp.bfloat16),
            pltpu.VMEM((S // KC, ATT_GROUPS * ATT_HEAD_DIM, KC), jnp.bfloat16),
            pltpu.VMEM((ATT_HEADS, TQ, 128), jnp.bfloat16),
            pltpu.VMEM((S // KC, KC, TQ), jnp.float32),
            pltpu.VMEM((1, TQ), jnp.int32),
            pltpu.VMEM((ATT_HEADS, 1, TQ), jnp.float32),
            pltpu.VMEM((ATT_HEADS, 1, TQ), jnp.float32),
            pltpu.VMEM((ATT_HEADS // 2, 2 * ATT_HEAD_DIM, TQ), jnp.float32),
            pltpu.VMEM((ATT_HEADS, DSA_KA, TQ), jnp.float32),
        ],
        compiler_params=pltpu.CompilerParams(
            dimension_semantics=("arbitrary", "arbitrary"),
            vmem_limit_bytes=_VMEM_LIMIT),
        name="dsa_attention",
    )(q, iq, iw, k, v, ik, pad_gain(qg), pad_gain(kg))


PEER_RT = 256
_EDGES = PEER_HEADS * PEER_TOPK


def _extract_topk_rows(work, n_out, val_ref, pay_ref, order=None, payload=None):
    if order is None:
        order = lax.broadcasted_iota(jnp.int32, work.shape, 0)
    big = jnp.int32(2 ** 30)
    for it in range(n_out):
        m = jnp.max(work, axis=0, keepdims=True)
        am = jnp.min(jnp.where(work == m, order, big), axis=0, keepdims=True)
        hit = order == am
        val_ref[it:it + 1, :] = m
        if payload is None:
            pay_ref[it:it + 1, :] = am
        else:
            pay_ref[it:it + 1, :] = jnp.max(jnp.where(hit, payload, -1), axis=0, keepdims=True)
        work = jnp.where(hit, -jnp.inf, work)


def _peer_route_kernel(q_ref, sk_ref, e_ref, g_ref, s1_ref, i1_ref, s2_ref, i2_ref, ts_ref, te_ref, et_ref,
                       gt_ref):
    K = PEER_TOPK
    RT = q_ref.shape[0]
    b8 = lax.broadcasted_iota(jnp.int32, (8, RT), 0)
    b16 = lax.broadcasted_iota(jnp.int32, (K, RT), 0)
    order = jnp.concatenate([b16] + [a * K + b8 for a in range(1, 8)] + [(8 + b8) * K], axis=0)
    for h in range(PEER_HEADS):
        for p, (s_ref, i_ref) in enumerate(((s1_ref, i1_ref), (s2_ref, i2_ref))):
            col = (h * 2 + p) * 128
            qs = q_ref[:, col:col + 128].astype(jnp.bfloat16)
            sk = sk_ref[h, p].astype(jnp.bfloat16)
            sc = lax.dot_general(sk, qs, (((1,), (1,)), ((), ())),
                                 preferred_element_type=jnp.float32)
            _extract_topk_rows(sc, K, s_ref, i_ref)
        s2 = s2_ref[...]
        i2 = i2_ref[...]
        cand = [s1_ref[0:1, :] + s2]
        cidx = [i1_ref[0:1, :] * PEER_NKEYS + i2]
        for a in range(1, 8):
            cand.append(jnp.where(b8 < K // (a + 1), s1_ref[a:a + 1, :] + s2[0:8], -jnp.inf))
            cidx.append(i1_ref[a:a + 1, :] * PEER_NKEYS + i2[0:8])
        cand.append(s1_ref[8:K, :] + s2[0:1])
        cidx.append(i1_ref[8:K, :] * PEER_NKEYS + i2[0:1])
        _extract_topk_rows(jnp.concatenate(cand, axis=0), K, ts_ref, te_ref, order, jnp.concatenate(cidx, axis=0))
        ts = ts_ref[...]
        ex = jnp.exp(ts - jnp.max(ts, axis=0, keepdims=True))
        gt_ref[h * K:(h + 1) * K, :] = ex / jnp.sum(ex, axis=0, keepdims=True)
        et_ref[h * K:(h + 1) * K, :] = te_ref[...]
    e_ref[...] = et_ref[...].T
    g_ref[...] = gt_ref[...].T


def peer_route(qry, subkeys):
    T = qry.shape[0]
    RT = PEER_RT
    assert T % RT == 0
    K = PEER_TOPK
    return pl.pallas_call(
        _peer_route_kernel,
        grid=(T // RT,),
        in_specs=[
            pl.BlockSpec((RT, qry.shape[1]), lambda i: (i, 0)),
            pl.BlockSpec(subkeys.shape, lambda i: (0, 0, 0, 0)),
        ],
        out_specs=[pl.BlockSpec((RT, _EDGES), lambda i: (i, 0)),
                   pl.BlockSpec((RT, _EDGES), lambda i: (i, 0))],
        out_shape=[jax.ShapeDtypeStruct((T, _EDGES), jnp.int32),
                   jax.ShapeDtypeStruct((T, _EDGES), jnp.float32)],
        scratch_shapes=[
            pltpu.VMEM((K, RT), jnp.float32), pltpu.VMEM((K, RT), jnp.int32),
            pltpu.VMEM((K, RT), jnp.float32), pltpu.VMEM((K, RT), jnp.int32),
            pltpu.VMEM((K, RT), jnp.float32), pltpu.VMEM((K, RT), jnp.int32),
            pltpu.VMEM((_EDGES, RT), jnp.int32),
            pltpu.VMEM((_EDGES, RT), jnp.float32),
        ],
        compiler_params=pltpu.CompilerParams(dimension_semantics=("arbitrary",)),
        name="peer_route",
    )(qry, subkeys)


PEER_CT = 256
PEER_DT = 1024
PEER_DE = 1024
PEER_DS = 1024
_COEF_UNROLL = 64


def _cast_kernel(t_ref, o_ref):
    o_ref[...] = t_ref[...].astype(o_ref.dtype)


def _to_bf16(tab):
    n, d = tab.shape
    tr = 512
    assert n % tr == 0
    return pl.pallas_call(
        _cast_kernel,
        grid=(n // tr,),
        in_specs=[pl.BlockSpec((tr, d), lambda i: (i, 0))],
        out_specs=pl.BlockSpec((tr, d), lambda i: (i, 0)),
        out_shape=jax.ShapeDtypeStruct((n, d), jnp.bfloat16),
        compiler_params=pltpu.CompilerParams(dimension_semantics=("arbitrary",)),
        name="table_to_bf16",
    )(tab)


def _peer_coef_kernel(e_ref, g_ref, c_ref):
    nk = PEER_NKEYS
    rows = lax.broadcasted_iota(jnp.int32, (nk, _EDGES), 0)
    shift = nk.bit_length() - 1

    def body(tt, carry):
        ts = [tt * _COEF_UNROLL + u for u in range(_COEF_UNROLL)]
        es = [e_ref[pl.ds(t, 1), :] for t in ts]
        a = [jnp.where(rows == (e >> shift), g_ref[pl.ds(t, 1), :], 0.0).astype(jnp.bfloat16)
             for t, e in zip(ts, es)]
        b = [jnp.where(rows == (e & (nk - 1)), 1.0, 0.0).astype(jnp.bfloat16) for e in es]
        c = [lax.dot_general(x, y, (((1,), (1,)), ((), ())), preferred_element_type=jnp.float32)
             for x, y in zip(a, b)]
        for t, x in zip(ts, c):
            c_ref[t] = x.astype(c_ref.dtype)
        return carry
    lax.fori_loop(0, e_ref.shape[0] // _COEF_UNROLL, body, 0)


def peer_coefficients(eidx, gates):
    T = eidx.shape[0]
    CT = PEER_CT
    assert T % CT == 0 and PEER_NKEYS & (PEER_NKEYS - 1) == 0
    return pl.pallas_call(
        _peer_coef_kernel,
        grid=(T // CT,),
        in_specs=[pl.BlockSpec((CT, _EDGES), lambda i: (i, 0)),
                  pl.BlockSpec((CT, _EDGES), lambda i: (i, 0))],
        out_specs=pl.BlockSpec((CT, PEER_NKEYS, PEER_NKEYS), lambda i: (i, 0, 0)),
        out_shape=jax.ShapeDtypeStruct((T, PEER_NKEYS, PEER_NKEYS), jnp.bfloat16),
        compiler_params=pltpu.CompilerParams(dimension_semantics=("arbitrary",)),
        name="peer_coef",
    )(eidx, gates)


def _peer_dense_kernel(h_ref, c_ref, u_ref, v_ref, x1_ref, g2_ref, y_ref, acc_ref):
    j = pl.program_id(1)
    bf = jnp.bfloat16

    @pl.when(j == 0)
    def _():
        acc_ref[...] = jnp.zeros(acc_ref.shape, jnp.float32)

    h = h_ref[...]
    dt, de = h.shape[0], u_ref.shape[0]
    coef = c_ref[...].reshape(dt, de)
    for s in range(0, de, PEER_DS):
        a = lax.dot_general(h, u_ref[s:s + PEER_DS, :], (((1,), (1,)), ((), ())),
                            preferred_element_type=jnp.float32)
        act = 0.5 * a * (1.0 + lax.erf(a * (2.0 ** -0.5)))
        w = (coef[:, s:s + PEER_DS].astype(jnp.float32) * act).astype(bf)
        acc_ref[...] += jnp.dot(w, v_ref[s:s + PEER_DS, :], preferred_element_type=jnp.float32)

    @pl.when(j == pl.num_programs(1) - 1)
    def _():
        y_ref[...] = x1_ref[...] + g2_ref[0] * acc_ref[...]


def peer_experts(h2, coef, u, vtab, x1, g2, seq):
    T, D = h2.shape
    N = u.shape[0]
    DT, DE = PEER_DT, PEER_DE
    rows = DE // PEER_NKEYS
    assert T % DT == 0 and N % DE == 0 and seq % DT == 0 and N == PEER_NKEYS * PEER_NKEYS
    return pl.pallas_call(
        _peer_dense_kernel,
        grid=(T // DT, N // DE),
        in_specs=[
            pl.BlockSpec((DT, D), lambda i, j: (i, 0)),
            pl.BlockSpec((DT, rows, PEER_NKEYS), lambda i, j: (i, j, 0)),
            pl.BlockSpec((DE, D), lambda i, j: (j, 0)),
            pl.BlockSpec((DE, D), lambda i, j: (j, 0)),
            pl.BlockSpec((DT, D), lambda i, j: (i, 0)),
            pl.BlockSpec((1, 1, D), lambda i, j: (i * DT // seq, 0, 0)),
        ],
        out_specs=pl.BlockSpec((DT, D), lambda i, j: (i, 0)),
        out_shape=jax.ShapeDtypeStruct((T, D), jnp.float32),
        scratch_shapes=[pltpu.VMEM((DT, D), jnp.float32)],
        compiler_params=pltpu.CompilerParams(dimension_semantics=("arbitrary", "arbitrary"),
                                             vmem_limit_bytes=56 * 1024 * 1024),
        name="peer_dense",
    )(h2, coef, _to_bf16(u), _to_bf16(vtab), x1, g2.reshape(g2.shape[0], 1, D))


def kernel(x, c, w_ada, b_ada, norm1_g, norm2_g, w_in, hg_lb, hg_onorm_g, q_norm_g, k_norm_g,
           w_a, w_b, w_out, peer_wq, peer_subkeys, peer_u, peer_v):
    B, S, D = x.shape
    T = B * S
    for l in range(w_ada.shape[0]):
        mod = ada_modulation(c, w_ada[l], b_ada[l])
        sh1, sc1, g1, sh2, sc2, g2 = jnp.split(mod, 6, axis=-1)
        hg, q, k, iw, iq, ik, v, gate = in_projection(x, norm1_g[l], sc1, sh1, pack_w_in(w_in[l]))
        ya = hgrn2(hg, hg_lb, hg_onorm_g[l], l)
        yb = dsa_attention(q, iq, iw, k, v, ik, q_norm_g[l], k_norm_g[l])
        x1, h2, qry = out_projection(ya, yb, gate, x, g1, norm2_g[l], sc2, sh2,
                                     w_a[l], w_b[l], w_out[l], peer_wq[l])
        eidx, gates = peer_route(qry.reshape(T, -1), peer_subkeys[l])
        x = peer_experts(h2.reshape(T, D), peer_coefficients(eidx, gates), peer_u[l], peer_v[l],
                         x1.reshape(T, D), g2, S).reshape(B, S, D)
    return x
```
